```python
import math
import jax, jax.numpy as jnp
from jax import lax
import numpy as np

D_MODEL = 2048
BATCH = 2
SEQ = 16384
DEPTH = 2
DEC_BATCH = 2
DEC_SEQ = 4096
PAST_LEN = 128

N_MIXERS = 2
N_FOURIER = (DEPTH + 1) // 2
N_ATTN = DEPTH // 2
FFT_GROUPS = 8
FFT_GROUP_DIM = D_MODEL // FFT_GROUPS
HEAD_DIM = 64
N_HEADS = D_MODEL // HEAD_DIM
N_KV = 4
GROUP = N_HEADS // N_KV
Q_DIM = N_HEADS * HEAD_DIM
KV_DIM = N_KV * HEAD_DIM
QKV_DIM = Q_DIM + 2 * KV_DIM
WINDOW = 128
BLOCK = 128
ROPE_THETA = 500000.0
ROT_DIM = HEAD_DIM // 4
ROT_HALF = ROT_DIM // 2
D_FF = 4 * D_MODEL
EPS = 1e-6
ATTN_SCALE = 1.0 / math.sqrt(HEAD_DIM)
NEG = -1e30

kernel_name = "fnet_swa_sink_hybrid_encoder"


def _rmsnorm(x, g):
    xf = x.astype(jnp.float32)
    y = xf * lax.rsqrt(jnp.mean(xf * xf, axis=-1, keepdims=True) + EPS)
    return (y * g.astype(jnp.float32)).astype(x.dtype)


def _rope_tables(s):
    inv_freq = ROPE_THETA ** (-jnp.arange(0, ROT_DIM, 2, dtype=jnp.float32) / ROT_DIM)
    ang = jnp.arange(s, dtype=jnp.float32)[:, None] * inv_freq[None, :]
    return jnp.cos(ang), jnp.sin(ang)


def _partial_rope(t, cos, sin):
    tf = t.astype(jnp.float32)
    c = cos[None, :, None, :]
    sn = sin[None, :, None, :]
    x1 = tf[..., :ROT_HALF]
    x2 = tf[..., ROT_HALF:ROT_DIM]
    out = jnp.concatenate([x1 * c - x2 * sn, x2 * c + x1 * sn, tf[..., ROT_DIM:]], axis=-1)
    return out.astype(t.dtype)


def _fourier_mixer(h, w_out):
    b, s, d = h.shape
    hg = h.astype(jnp.float32).reshape(b, s, FFT_GROUPS, FFT_GROUP_DIM)
    f = jnp.fft.fft2(hg, axes=(1, 3), norm="ortho").real
    return f.reshape(b, s, d).astype(h.dtype) @ w_out


def _window_attention(h, w_qkv, q_gain, k_gain, sink, w_o):
    b, s, _ = h.shape
    qkv = h @ w_qkv
    q = qkv[..., :Q_DIM].reshape(b, s, N_HEADS, HEAD_DIM)
    k = qkv[..., Q_DIM:Q_DIM + KV_DIM].reshape(b, s, N_KV, HEAD_DIM)
    v = qkv[..., Q_DIM + KV_DIM:].reshape(b, s, N_KV, HEAD_DIM)
    q = _rmsnorm(q, q_gain)
    k = _rmsnorm(k, k_gain)
    cos, sin = _rope_tables(s)
    q = _partial_rope(q, cos, sin)
    k = _partial_rope(k, cos, sin)

    nb = s // BLOCK
    qb = jnp.moveaxis(q.reshape(b, nb, BLOCK, N_KV, GROUP, HEAD_DIM), 1, 0)
    pad = ((0, 0), (BLOCK, BLOCK), (0, 0), (0, 0))
    kp = jnp.pad(k, pad).reshape(b, nb + 2, BLOCK, N_KV, HEAD_DIM)
    vp = jnp.pad(v, pad).reshape(b, nb + 2, BLOCK, N_KV, HEAD_DIM)
    kw = jnp.moveaxis(jnp.concatenate([kp[:, :-2], kp[:, 1:-1], kp[:, 2:]], axis=2), 1, 0)
    vw = jnp.moveaxis(jnp.concatenate([vp[:, :-2], vp[:, 1:-1], vp[:, 2:]], axis=2), 1, 0)

    qi = jnp.arange(BLOCK)[:, None]
    kj = jnp.arange(3 * BLOCK)[None, :]
    band = jnp.abs(qi + BLOCK - kj) <= WINDOW
    sink_f = sink.astype(jnp.float32).reshape(N_KV, GROUP)[None, :, :, None, None]

    def block_fn(args):
        blk, q_b, k_b, v_b = args
        key_pos = blk * BLOCK - BLOCK + kj
        mask = band & (key_pos >= 0) & (key_pos < s)
        sc = jnp.einsum('bqhgd,bkhd->bhgqk', q_b.astype(jnp.float32), k_b.astype(jnp.float32)) * ATTN_SCALE
        sc = jnp.where(mask, sc, NEG)
        m = jnp.maximum(jnp.max(sc, axis=-1, keepdims=True), sink_f)
        p = jnp.exp(sc - m)
        denom = jnp.sum(p, axis=-1, keepdims=True) + jnp.exp(sink_f - m)
        o = jnp.einsum('bhgqk,bkhd->bqhgd', p / denom, v_b.astype(jnp.float32))
        return o.astype(h.dtype)

    o = lax.map(block_fn, (jnp.arange(nb), qb, kw, vw))
    o = jnp.moveaxis(o, 0, 1).reshape(b, s, Q_DIM)
    return o @ w_o


def _sqrelu_mlp(h, w_up, w_down):
    a = jax.nn.relu(h @ w_up)
    return (a * a) @ w_down


def _trunk(x, fourier_norm, fourier_w_out, attn_norm, attn_w_qkv, attn_q_norm, attn_k_norm,
           attn_sink, attn_w_o, mlp_norm, mlp_w_up, mlp_w_down):
    for i in range(DEPTH):
        j = i // N_MIXERS
        if i % N_MIXERS == 0:
            x = x + _fourier_mixer(_rmsnorm(x, fourier_norm[j]), fourier_w_out[j])
        else:
            x = x + _window_attention(_rmsnorm(x, attn_norm[j]), attn_w_qkv[j], attn_q_norm[j],
                                      attn_k_norm[j], attn_sink[j], attn_w_o[j])
        x = x + _sqrelu_mlp(_rmsnorm(x, mlp_norm[i]), mlp_w_up[i], mlp_w_down[i])
    return x


def setup_inputs(seed: int = 0) -> dict:
    key = jax.random.key(seed)
    ks = jax.random.split(key, 14)
    f32 = jnp.float32
    nrm = jax.random.normal
    return {
        "x_prompt": nrm(ks[0], (BATCH, SEQ, D_MODEL), f32),
        "x_sample": nrm(ks[1], (DEC_BATCH, DEC_SEQ, D_MODEL), f32),
        "fourier_norm": 1.0 + 0.02 * nrm(ks[2], (N_FOURIER, D_MODEL), f32),
        "fourier_w_out": nrm(ks[3], (N_FOURIER, D_MODEL, D_MODEL), f32) * D_MODEL ** -0.5,
        "attn_norm": 1.0 + 0.02 * nrm(ks[4], (N_ATTN, D_MODEL), f32),
        "attn_w_qkv": nrm(ks[5], (N_ATTN, D_MODEL, QKV_DIM), f32) * D_MODEL ** -0.5,
        "attn_q_norm": 1.0 + 0.02 * nrm(ks[6], (N_ATTN, HEAD_DIM), f32),
        "attn_k_norm": 1.0 + 0.02 * nrm(ks[7], (N_ATTN, HEAD_DIM), f32),
        "attn_sink": 0.5 * nrm(ks[8], (N_ATTN, N_HEADS), f32),
        "attn_w_o": nrm(ks[9], (N_ATTN, Q_DIM, D_MODEL), f32) * Q_DIM ** -0.5,
        "mlp_norm": 1.0 + 0.02 * nrm(ks[10], (DEPTH, D_MODEL), f32),
        "mlp_w_up": nrm(ks[11], (DEPTH, D_MODEL, D_FF), f32) * D_MODEL ** -0.5,
        "mlp_w_down": nrm(ks[12], (DEPTH, D_FF, D_MODEL), f32) * D_FF ** -0.5,
    }


def reference(x_prompt, x_sample, fourier_norm, fourier_w_out, attn_norm, attn_w_qkv, attn_q_norm,
              attn_k_norm, attn_sink, attn_w_o, mlp_norm, mlp_w_up, mlp_w_down):
    y_prompt = _trunk(x_prompt, fourier_norm, fourier_w_out, attn_norm, attn_w_qkv, attn_q_norm,
                      attn_k_norm, attn_sink, attn_w_o, mlp_norm, mlp_w_up, mlp_w_down)
    y_sample = _trunk(x_sample, fourier_norm, fourier_w_out, attn_norm, attn_w_qkv, attn_q_norm,
                      attn_k_norm, attn_sink, attn_w_o, mlp_norm, mlp_w_up, mlp_w_down)
    return (y_prompt, y_sample)
```

```python
import functools
import math

import numpy as np
import jax
import jax.numpy as jnp
from jax import lax
from jax.experimental import pallas as pl
from jax.experimental.pallas import tpu as pltpu

F32 = jnp.float32
BF16 = jnp.bfloat16

FFT_GROUPS = 8
HEAD_DIM = 64
N_KV = 4
GROUP = 8
WINDOW_BLOCK = 128
ROPE_THETA = 500000.0
ROT_DIM = HEAD_DIM // 4
ROT_HALF = ROT_DIM // 2
EPS = 1e-6
NEG = -1e30
LANES = 128
MXU_DIM = 256
VMEM_LIMIT = 60 * 1024 * 1024


def _cparams(sem):
    return pltpu.CompilerParams(dimension_semantics=sem, vmem_limit_bytes=VMEM_LIMIT)


def _rms_rows(x, gain):
    ms = jnp.mean(x * x, axis=-1, keepdims=True)
    return x * lax.rsqrt(ms + EPS) * gain


def _seq_split(s):
    n1 = 1 << ((s.bit_length() - 1 + 1) // 2)
    n2 = s // n1
    assert n1 * n2 == s
    return n1, n2


def _norm_chdft_kernel(x_ref, g_ref, cs_ref, zr_ref, zi_ref, *, gd):
    h = _rms_rows(x_ref[...], g_ref[...]).astype(BF16)
    cs = cs_ref[...]
    for g in range(FFT_GROUPS):
        r = jnp.dot(h[:, g * gd:(g + 1) * gd], cs, preferred_element_type=F32)
        zr_ref[:, g * gd:(g + 1) * gd] = r[:, :gd].astype(BF16)
        zi_ref[:, g * gd:(g + 1) * gd] = r[:, gd:].astype(BF16)


def _norm_chdft(x2d, gain, cs, tm):
    t, d = x2d.shape
    gd = d // FFT_GROUPS
    return pl.pallas_call(
        functools.partial(_norm_chdft_kernel, gd=gd),
        grid=(t // tm,),
        in_specs=[
            pl.BlockSpec((tm, d), lambda i: (i, 0)),
            pl.BlockSpec((1, d), lambda i: (0, 0)),
            pl.BlockSpec((gd, 2 * gd), lambda i: (0, 0)),
        ],
        out_specs=[
            pl.BlockSpec((tm, d), lambda i: (i, 0)),
            pl.BlockSpec((tm, d), lambda i: (i, 0)),
        ],
        out_shape=[jax.ShapeDtypeStruct((t, d), BF16)] * 2,
        compiler_params=_cparams(("parallel",)),
        name="norm_chdft",
    )(x2d, gain, cs)


def _seq_stage_kernel(tab_ref, z_ref, o_ref, *, tj, per_step_table):
    for j in range(tj):
        tab = tab_ref[j] if per_step_table else tab_ref[...]
        o_ref[0, j] = jnp.dot(tab, z_ref[0, j], preferred_element_type=F32).astype(o_ref.dtype)


def _seq_stage(tab, z, tj, per_step_table):
    b, nj, k, d = z.shape
    m = tab.shape[-2]
    if per_step_table:
        tab_spec = pl.BlockSpec((tj, m, k), lambda bi, ji: (ji, 0, 0))
    else:
        tab_spec = pl.BlockSpec((m, k), lambda bi, ji: (0, 0))
    return pl.pallas_call(
        functools.partial(_seq_stage_kernel, tj=tj, per_step_table=per_step_table),
        grid=(b, nj // tj),
        in_specs=[tab_spec, pl.BlockSpec((1, tj, k, d), lambda bi, ji: (bi, ji, 0, 0))],
        out_specs=pl.BlockSpec((1, tj, m, d), lambda bi, ji: (bi, ji, 0, 0)),
        out_shape=jax.ShapeDtypeStruct((b, nj, m, d), BF16),
        compiler_params=_cparams(("parallel", "parallel")),
        name="seq_dft_stage1" if per_step_table else "seq_dft_stage2",
    )(tab, z)


def _proj_norm_kernel(a_ref, x_ref, w_ref, g_ref, x1_ref, h_ref):
    x1 = x_ref[...] + jnp.dot(a_ref[...], w_ref[...], preferred_element_type=F32)
    x1_ref[...] = x1
    h_ref[...] = _rms_rows(x1, g_ref[...]).astype(BF16)


def _proj_norm(a, x2d, w, gain, tm):
    t, d = x2d.shape
    return pl.pallas_call(
        _proj_norm_kernel,
        grid=(t // tm,),
        in_specs=[
            pl.BlockSpec((tm, d), lambda i: (i, 0)),
            pl.BlockSpec((tm, d), lambda i: (i, 0)),
            pl.BlockSpec((d, d), lambda i: (0, 0)),
            pl.BlockSpec((1, d), lambda i: (0, 0)),
        ],
        out_specs=[
            pl.BlockSpec((tm, d), lambda i: (i, 0)),
            pl.BlockSpec((tm, d), lambda i: (i, 0)),
        ],
        out_shape=[jax.ShapeDtypeStruct((t, d), F32), jax.ShapeDtypeStruct((t, d), BF16)],
        compiler_params=_cparams(("parallel",)),
        name="proj_norm",
    )(a, x2d, w, gain)


def _mlp_kernel(h_ref, x_ref, wu_ref, wd_ref, o_ref):
    @pl.when(pl.program_id(1) == 0)
    def _():
        o_ref[...] = x_ref[...]

    a = jnp.maximum(jnp.dot(h_ref[...], wu_ref[...], preferred_element_type=F32), 0.0)
    o_ref[...] += jnp.dot((a * a).astype(BF16), wd_ref[...], preferred_element_type=F32)


def _mlp(h, x1, w_up, w_down, tm, tf):
    t, d = x1.shape
    dff = w_up.shape[1]
    return pl.pallas_call(
        _mlp_kernel,
        grid=(t // tm, dff // tf),
        in_specs=[
            pl.BlockSpec((tm, d), lambda i, j: (i, 0)),
            pl.BlockSpec((tm, d), lambda i, j: (i, 0)),
            pl.BlockSpec((d, tf), lambda i, j: (0, j)),
            pl.BlockSpec((tf, d), lambda i, j: (j, 0)),
        ],
        out_specs=pl.BlockSpec((tm, d), lambda i, j: (i, 0)),
        out_shape=jax.ShapeDtypeStruct((t, d), F32),
        compiler_params=_cparams(("parallel", "arbitrary")),
        name="sqrelu_mlp",
    )(h, x1, w_up, w_down)


def _qkv_kernel(x_ref, g_ref, w_ref, e_ref, qg_ref, kg_ref, cf_ref, s1_ref, s2_ref,
                q_ref, k_ref, v_ref, *, d, scale):
    h = _rms_rows(x_ref[...], g_ref[...]).astype(BF16)
    qkv = jnp.dot(h, w_ref[...], preferred_element_type=F32)
    e = e_ref[...]
    cf, s1, s2 = cf_ref[...], s1_ref[...], s2_ref[...]
    kd = k_ref.shape[1]

    def head_norm_rope(blk, gain, out_scale):
        ss = jnp.dot((blk * blk).astype(BF16), e, preferred_element_type=F32)
        n = blk * lax.rsqrt(ss * (1.0 / HEAD_DIM) + EPS) * gain
        outs = []
        for u in range(MXU_DIM // LANES):
            xb = n[:, u * LANES:(u + 1) * LANES]
            r = (xb * cf + pltpu.roll(xb, LANES - ROT_HALF, axis=1) * s1
                 + pltpu.roll(xb, ROT_HALF, axis=1) * s2)
            outs.append(r * out_scale if out_scale != 1.0 else r)
        return jnp.concatenate(outs, axis=1)

    for c in range(d // MXU_DIM):
        blk = qkv[:, c * MXU_DIM:(c + 1) * MXU_DIM]
        q_ref[:, c * MXU_DIM:(c + 1) * MXU_DIM] = head_norm_rope(blk, qg_ref[...], scale).astype(BF16)
    for c in range(kd // MXU_DIM):
        blk = qkv[:, d + c * MXU_DIM:d + (c + 1) * MXU_DIM]
        k_ref[:, c * MXU_DIM:(c + 1) * MXU_DIM] = head_norm_rope(blk, kg_ref[...], 1.0).astype(BF16)
    v_ref[...] = qkv[:, d + kd:].astype(BF16)


def _qkv(x2d, gain, w, e, qg, kg, cf, s1, s2, seq, tm):
    t, d = x2d.shape
    kd = N_KV * LANES
    nt = seq // tm
    tab_spec = pl.BlockSpec((tm, LANES), lambda i: (i % nt, 0))
    const = lambda shape: pl.BlockSpec(shape, lambda i: (0, 0))
    return pl.pallas_call(
        functools.partial(_qkv_kernel, d=d, scale=1.0 / math.sqrt(HEAD_DIM)),
        grid=(t // tm,),
        in_specs=[
            pl.BlockSpec((tm, d), lambda i: (i, 0)),
            const((1, d)),
            const((d, d + 2 * kd)),
            const((MXU_DIM, MXU_DIM)),
            const((1, MXU_DIM)),
            const((1, MXU_DIM)),
            tab_spec, tab_spec, tab_spec,
        ],
        out_specs=[
            pl.BlockSpec((tm, d), lambda i: (i, 0)),
            pl.BlockSpec((tm, kd), lambda i: (i, 0)),
            pl.BlockSpec((tm, kd), lambda i: (i, 0)),
        ],
        out_shape=[
            jax.ShapeDtypeStruct((t, d), BF16),
            jax.ShapeDtypeStruct((t, kd), BF16),
            jax.ShapeDtypeStruct((t, kd), BF16),
        ],
        compiler_params=_cparams(("parallel",)),
        name="qkv_norm_rope",
    )(x2d, gain, w, e, qg, kg, cf, s1, s2)


def _attn_kernel(sink_ref, q_ref, kp_ref, kc_ref, kn_ref, vp_ref, vc_ref, vn_ref, o_ref, *, nb):
    j = pl.program_id(1)
    bq = WINDOW_BLOCK
    npair = GROUP // 2
    lane = lax.broadcasted_iota(jnp.int32, (bq, LANES), 1)
    lo = lane < HEAD_DIM
    row = lax.broadcasted_iota(jnp.int32, (bq, 3 * bq), 0)
    col = lax.broadcasted_iota(jnp.int32, (bq, 3 * bq), 1)
    seg = col // bq
    kk = col - seg * bq
    valid = ((seg == 1)
             | ((seg == 0) & (kk >= row) & (j > 0))
             | ((seg == 2) & (kk <= row) & (j < nb - 1)))
    zero = jnp.zeros((bq, LANES), BF16)
    lo3 = lax.broadcasted_iota(jnp.int32, (3 * bq, LANES), 1) < HEAD_DIM
    zero3 = jnp.zeros((3 * bq, LANES), BF16)

    for h in range(N_KV):
        ks = slice(h * LANES, (h + 1) * LANES)
        k2 = jnp.concatenate([kp_ref[0, :, ks], kc_ref[0, :, ks], kn_ref[0, :, ks]], axis=0)
        v2 = jnp.concatenate([vp_ref[0, :, ks], vc_ref[0, :, ks], vn_ref[0, :, ks]], axis=0)
        qparts = []
        for parity in range(2):
            for p in range(npair):
                c0 = (h * npair + p) * LANES
                blk = q_ref[0, :, c0:c0 + LANES]
                qparts.append(jnp.where(lo if parity == 0 else ~lo, blk, zero))
        qs = jnp.concatenate(qparts, axis=0)
        s = lax.dot_general(qs, k2, (((1,), (1,)), ((), ())), preferred_element_type=F32)
        ps, inv = [], []
        for r in range(GROUP):
            parity, p = divmod(r, npair)
            sink = sink_ref[h * GROUP + 2 * p + parity]
            sr = jnp.where(valid, s[r * bq:(r + 1) * bq], NEG)
            m = jnp.maximum(jnp.max(sr, axis=-1, keepdims=True), sink)
            e = jnp.exp(sr - m)
            denom = jnp.sum(e, axis=-1, keepdims=True) + jnp.exp(sink - m)
            ps.append(e.astype(BF16))
            inv.append(1.0 / denom)
        pe = jnp.concatenate(ps[:npair], axis=0)
        po = jnp.concatenate(ps[npair:], axis=0)
        o = (jnp.dot(pe, jnp.where(lo3, v2, zero3), preferred_element_type=F32)
             + jnp.dot(po, jnp.where(lo3, zero3, v2), preferred_element_type=F32))
        for p in range(npair):
            scale = jnp.where(lo, inv[p], inv[npair + p])
            c0 = (h * npair + p) * LANES
            o_ref[0, :, c0:c0 + LANES] = (o[p * bq:(p + 1) * bq] * scale).astype(BF16)


def _attention(sink, q, k, v):
    b, s, d = q.shape
    kd = k.shape[-1]
    bq = WINDOW_BLOCK
    nb = s // bq
    prev = lambda bi, j: (bi, jnp.maximum(j - 1, 0), 0)
    cur = lambda bi, j: (bi, j, 0)
    nxt = lambda bi, j: (bi, jnp.minimum(j + 1, nb - 1), 0)
    kv = lambda im: pl.BlockSpec((1, bq, kd), im)
    return pl.pallas_call(
        functools.partial(_attn_kernel, nb=nb),
        grid=(b, nb),
        in_specs=[
            pl.BlockSpec(memory_space=pltpu.SMEM),
            pl.BlockSpec((1, bq, d), cur),
            kv(prev), kv(cur), kv(nxt), kv(prev), kv(cur), kv(nxt),
        ],
        out_specs=pl.BlockSpec((1, bq, d), cur),
        out_shape=jax.ShapeDtypeStruct((b, s, d), BF16),
        compiler_params=_cparams(("parallel", "parallel")),
        name="window_attention",
    )(sink, q, k, k, k, v, v, v)


def _chan_table(gd):
    k = np.arange(gd)
    ang = 2.0 * np.pi * ((k[:, None] * k[None, :]) % gd) / gd
    sc = 1.0 / math.sqrt(gd)
    return jnp.asarray(np.concatenate([np.cos(ang) * sc, -np.sin(ang) * sc], axis=1), BF16)


def _stage1_table(n1, n2):
    n = n1 * n2
    k1 = jnp.arange(n1, dtype=jnp.int32)
    pos = n2 * jnp.arange(n1, dtype=jnp.int32)[None, None, :] + jnp.arange(n2, dtype=jnp.int32)[:, None, None]
    ang = ((k1[None, :, None] * pos) % n).astype(F32) * (2.0 * math.pi / n)
    c, s = jnp.cos(ang), jnp.sin(ang)
    top = jnp.concatenate([c, s], axis=-1)
    bot = jnp.concatenate([-s, c], axis=-1)
    return jnp.concatenate([top, bot], axis=-2).astype(BF16)


def _stage2_table(n1, n2):
    k = np.arange(n2)
    ang = 2.0 * np.pi * ((k[:, None] * k[None, :]) % n2) / n2
    sc = 1.0 / math.sqrt(n1 * n2)
    return jnp.asarray(np.concatenate([np.cos(ang) * sc, np.sin(ang) * sc], axis=1), BF16)


def _rope_lane_tables(s):
    inv_freq = ROPE_THETA ** (-jnp.arange(0, ROT_DIM, 2, dtype=F32) / ROT_DIM)
    ang = jnp.arange(s, dtype=F32)[:, None] * inv_freq[None, :]
    cos, sin = jnp.cos(ang), jnp.sin(ang)
    pad = HEAD_DIM - ROT_DIM
    ones = jnp.ones((s, pad), F32)
    zeros = jnp.zeros((s, pad), F32)
    zh = jnp.zeros((s, ROT_HALF), F32)
    cf = jnp.concatenate([cos, cos, ones], axis=1)
    s1 = jnp.concatenate([-sin, zh, zeros], axis=1)
    s2 = jnp.concatenate([zh, sin, zeros], axis=1)
    rep = LANES // HEAD_DIM
    return tuple(jnp.tile(a, (1, rep)) for a in (cf, s1, s2))


def _dup_heads(w):
    d = w.shape[0]
    w4 = w.reshape(d, N_KV, 1, HEAD_DIM)
    return jnp.broadcast_to(w4, (d, N_KV, LANES // HEAD_DIM, HEAD_DIM)).reshape(d, N_KV * LANES)


def _segment_ones():
    i = np.arange(MXU_DIM) // HEAD_DIM
    return jnp.asarray((i[:, None] == i[None, :]).astype(np.float32), BF16)


def _fourier_mixer(x, gain, w_out, mlp_gain, tm):
    b, s, d = x.shape
    gd = d // FFT_GROUPS
    n1, n2 = _seq_split(s)
    x2d = x.reshape(b * s, d)
    zr, zi = _norm_chdft(x2d, gain, _chan_table(gd), tm)
    z = jnp.stack([zr.reshape(b, n1, n2, d), zi.reshape(b, n1, n2, d)], axis=1)
    z = z.transpose(0, 3, 1, 2, 4).reshape(b, n2, 2 * n1, d)
    y = _seq_stage(_stage1_table(n1, n2), z, min(4, n2), True)
    y = y.reshape(b, n2, 2, n1, d).transpose(0, 3, 2, 1, 4).reshape(b, n1, 2 * n2, d)
    f = _seq_stage(_stage2_table(n1, n2), y, min(8, n1), False)
    f = f.transpose(0, 2, 1, 3).reshape(b * s, d)
    return _proj_norm(f, x2d, w_out, mlp_gain, tm)


def _attn_mixer(x2d, b, s, gain, w_qkv, q_gain, k_gain, sink, w_o, mlp_gain, tm):
    t, d = x2d.shape
    rep = MXU_DIM // HEAD_DIM
    cf, s1, s2 = _rope_lane_tables(s)
    q, k, v = _qkv(x2d, gain, w_qkv, _segment_ones(),
                   jnp.tile(q_gain, rep)[None, :], jnp.tile(k_gain, rep)[None, :],
                   cf, s1, s2, s, tm)
    kd = k.shape[-1]
    o = _attention(sink, q.reshape(b, s, d), k.reshape(b, s, kd), v.reshape(b, s, kd))
    return _proj_norm(o.reshape(t, d), x2d, w_o, mlp_gain, tm)


def _trunk(x, p, tm, tm_mlp, tf):
    b, s, d = x.shape
    x1, h = _fourier_mixer(x, p["fourier_norm"][0][None, :], p["fourier_w_out"][0],
                           p["mlp_norm"][0][None, :], tm)
    x2 = _mlp(h, x1, p["mlp_w_up"][0], p["mlp_w_down"][0], tm_mlp, tf)
    x3, h = _attn_mixer(x2, b, s, p["attn_norm"][0][None, :], p["attn_w_qkv"], p["attn_q_norm"][0],
                        p["attn_k_norm"][0], p["attn_sink"][0], p["attn_w_o"][0],
                        p["mlp_norm"][1][None, :], tm)
    x4 = _mlp(h, x3, p["mlp_w_up"][1], p["mlp_w_down"][1], tm_mlp, tf)
    return x4.reshape(b, s, d)


def kernel(x_prompt, x_sample, fourier_norm, fourier_w_out, attn_norm, attn_w_qkv, attn_q_norm,
           attn_k_norm, attn_sink, attn_w_o, mlp_norm, mlp_w_up, mlp_w_down):
    d = x_prompt.shape[-1]
    w_qkv = attn_w_qkv[0]
    kvd = N_KV * HEAD_DIM
    w_qkv = jnp.concatenate(
        [w_qkv[:, :d], _dup_heads(w_qkv[:, d:d + kvd]), _dup_heads(w_qkv[:, d + kvd:])], axis=1)
    p = dict(
        fourier_norm=fourier_norm, fourier_w_out=fourier_w_out.astype(BF16),
        attn_norm=attn_norm, attn_w_qkv=w_qkv.astype(BF16), attn_q_norm=attn_q_norm,
        attn_k_norm=attn_k_norm, attn_sink=attn_sink, attn_w_o=attn_w_o.astype(BF16),
        mlp_norm=mlp_norm, mlp_w_up=mlp_w_up.astype(BF16), mlp_w_down=mlp_w_down.astype(BF16),
    )
    outs = []
    for x in (x_prompt, x_sample):
        t = x.shape[0] * x.shape[1]
        tm = min(512, x.shape[1])
        tm_mlp = min(1024, t)
        tf = min(512, mlp_w_up.shape[-1])
        outs.append(_trunk(x, p, tm, tm_mlp, tf))
    return tuple(outs)
```

```python
import functools
import math

import numpy as np
import jax
import jax.numpy as jnp
from jax import lax
from jax.experimental import pallas as pl
from jax.experimental.pallas import tpu as pltpu

F32 = jnp.float32
BF16 = jnp.bfloat16

FFT_GROUPS = 8
HEAD_DIM = 64
N_KV = 4
GROUP = 8
WINDOW_BLOCK = 128
ROPE_THETA = 500000.0
ROT_DIM = HEAD_DIM // 4
ROT_HALF = ROT_DIM // 2
EPS = 1e-6
NEG = -1e30
LANES = 128
MXU_DIM = 256
VMEM_LIMIT = 60 * 1024 * 1024


def _cparams(sem):
    return pltpu.CompilerParams(dimension_semantics=sem, vmem_limit_bytes=VMEM_LIMIT)


def _rms_rows(x, gain):
    ms = jnp.mean(x * x, axis=-1, keepdims=True)
    return x * lax.rsqrt(ms + EPS) * gain


def _seq_split(s):
    n1 = 1 << ((s.bit_length() - 1 + 1) // 2)
    n2 = s // n1
    assert n1 * n2 == s
    return n1, n2


SUBLANES = 8


def _pack_complex(re, im):
    rb = lax.bitcast_convert_type(re.astype(BF16).astype(F32), jnp.uint32)
    ib = lax.bitcast_convert_type(im.astype(BF16).astype(F32), jnp.uint32)
    return rb | (ib >> 16)


def _unpack_complex(u):
    re = lax.bitcast_convert_type(u & jnp.uint32(0xFFFF0000), F32)
    im = lax.bitcast_convert_type(u << 16, F32)
    return re, im


def _fourier_stage1_kernel(x_ref, g_ref, cs_ref, tab_ref, o_ref, zs_ref, *, gd):
    n1 = x_ref.shape[1]
    rows = n1 * SUBLANES
    d = x_ref.shape[3]
    nblk = 2 * gd // LANES
    h = _rms_rows(x_ref[0].reshape(rows, d), g_ref[...]).astype(BF16)
    cs = cs_ref[...]
    for g in range(FFT_GROUPS):
        slot = g % 2
        r = jnp.dot(h[:, g * gd:(g + 1) * gd], cs, preferred_element_type=F32)
        for c in range(nblk):
            zs_ref[slot, c] = r[:, c * LANES:(c + 1) * LANES]
        for j in range(SUBLANES):
            parts = [zs_ref[slot, c, pl.ds(j, n1, stride=SUBLANES), :] for c in range(nblk)]
            zr = jnp.concatenate(parts[:nblk // 2], axis=1)
            zi = jnp.concatenate(parts[nblk // 2:], axis=1)
            zc = jnp.concatenate([zr, zi], axis=0).astype(BF16)
            y = jnp.dot(tab_ref[j], zc, preferred_element_type=F32)
            o_ref[0, j, :, g * gd:(g + 1) * gd] = _pack_complex(y[:n1], y[n1:])


def _fourier_stage1(x4, gain, cs, tab):
    b, n1, n2, d = x4.shape
    gd = d // FFT_GROUPS
    rows = n1 * SUBLANES
    return pl.pallas_call(
        functools.partial(_fourier_stage1_kernel, gd=gd),
        grid=(b, n2 // SUBLANES),
        in_specs=[
            pl.BlockSpec((1, n1, SUBLANES, d), lambda bi, j: (bi, 0, j, 0)),
            pl.BlockSpec((1, d), lambda bi, j: (0, 0)),
            pl.BlockSpec((gd, 2 * gd), lambda bi, j: (0, 0)),
            pl.BlockSpec((SUBLANES, 2 * n1, 2 * n1), lambda bi, j: (j, 0, 0)),
        ],
        out_specs=pl.BlockSpec((1, SUBLANES, n1, d), lambda bi, j: (bi, j, 0, 0)),
        out_shape=jax.ShapeDtypeStruct((b, n2, n1, d), jnp.uint32),
        scratch_shapes=[pltpu.VMEM((2, 2 * gd // LANES, rows, LANES), F32)],
        compiler_params=_cparams(("parallel", "parallel")),
        name="fourier_stage1",
    )(x4, gain, cs, tab)


def _fourier_stage2_kernel(y_ref, tab_ref, o_ref, ys_ref, fs_ref, *, cw):
    n2 = y_ref.shape[1]
    rows = n2 * SUBLANES
    d = y_ref.shape[3]
    nblk = cw // LANES
    tab = tab_ref[...]
    for ch in range(d // cw):
        slot = ch % 2
        for c in range(nblk):
            lanes = slice(ch * cw + c * LANES, ch * cw + (c + 1) * LANES)
            ys_ref[slot, c] = y_ref[0, :, :, lanes].reshape(rows, LANES)
        for k in range(SUBLANES):
            u = jnp.concatenate(
                [ys_ref[slot, c, pl.ds(k, n2, stride=SUBLANES), :] for c in range(nblk)], axis=1)
            re, im = _unpack_complex(u)
            zc = jnp.concatenate([re, im], axis=0).astype(BF16)
            f = jnp.dot(tab, zc, preferred_element_type=F32)
            for c in range(nblk):
                fs_ref[slot, c, pl.ds(k, n2, stride=SUBLANES), :] = f[:, c * LANES:(c + 1) * LANES]
        for c in range(nblk):
            lanes = slice(ch * cw + c * LANES, ch * cw + (c + 1) * LANES)
            o_ref[0, :, :, lanes] = fs_ref[slot, c].reshape(n2, SUBLANES, LANES)


def _fourier_stage2(y, tab):
    b, n2, n1, d = y.shape
    cw = MXU_DIM
    rows = n2 * SUBLANES
    return pl.pallas_call(
        functools.partial(_fourier_stage2_kernel, cw=cw),
        grid=(b, n1 // SUBLANES),
        in_specs=[
            pl.BlockSpec((1, n2, SUBLANES, d), lambda bi, k: (bi, 0, k, 0)),
            pl.BlockSpec((n2, 2 * n2), lambda bi, k: (0, 0)),
        ],
        out_specs=pl.BlockSpec((1, n2, SUBLANES, d), lambda bi, k: (bi, 0, k, 0)),
        out_shape=jax.ShapeDtypeStruct((b, n2, n1, d), F32),
        scratch_shapes=[pltpu.VMEM((2, cw // LANES, rows, LANES), jnp.uint32),
                        pltpu.VMEM((2, cw // LANES, rows, LANES), F32)],
        compiler_params=_cparams(("parallel", "parallel")),
        name="fourier_stage2",
    )(y, tab)


def _proj_norm_kernel(a_ref, x_ref, w_ref, g_ref, x1_ref, h_ref):
    x1 = x_ref[...] + jnp.dot(a_ref[...].astype(BF16), w_ref[...], preferred_element_type=F32)
    x1_ref[...] = x1
    h_ref[...] = _rms_rows(x1, g_ref[...]).astype(BF16)


def _proj_norm(a, x2d, w, gain, tm):
    t, d = x2d.shape
    return pl.pallas_call(
        _proj_norm_kernel,
        grid=(t // tm,),
        in_specs=[
            pl.BlockSpec((tm, d), lambda i: (i, 0)),
            pl.BlockSpec((tm, d), lambda i: (i, 0)),
            pl.BlockSpec((d, d), lambda i: (0, 0)),
            pl.BlockSpec((1, d), lambda i: (0, 0)),
        ],
        out_specs=[
            pl.BlockSpec((tm, d), lambda i: (i, 0)),
            pl.BlockSpec((tm, d), lambda i: (i, 0)),
        ],
        out_shape=[jax.ShapeDtypeStruct((t, d), F32), jax.ShapeDtypeStruct((t, d), BF16)],
        compiler_params=_cparams(("parallel",)),
        name="proj_norm",
    )(a, x2d, w, gain)


def _mlp_kernel(h_ref, x_ref, wu_ref, wd_ref, o_ref):
    @pl.when(pl.program_id(1) == 0)
    def _():
        o_ref[...] = x_ref[...]

    a = jnp.maximum(jnp.dot(h_ref[...], wu_ref[...], preferred_element_type=F32), 0.0)
    o_ref[...] += jnp.dot((a * a).astype(BF16), wd_ref[...], preferred_element_type=F32)


def _mlp(h, x1, w_up, w_down, tm, tf):
    t, d = x1.shape
    dff = w_up.shape[1]
    return pl.pallas_call(
        _mlp_kernel,
        grid=(t // tm, dff // tf),
        in_specs=[
            pl.BlockSpec((tm, d), lambda i, j: (i, 0)),
            pl.BlockSpec((tm, d), lambda i, j: (i, 0)),
            pl.BlockSpec((d, tf), lambda i, j: (0, j)),
            pl.BlockSpec((tf, d), lambda i, j: (j, 0)),
        ],
        out_specs=pl.BlockSpec((tm, d), lambda i, j: (i, 0)),
        out_shape=jax.ShapeDtypeStruct((t, d), F32),
        compiler_params=_cparams(("parallel", "arbitrary")),
        name="sqrelu_mlp",
    )(h, x1, w_up, w_down)


def _qkv_kernel(x_ref, g_ref, w_ref, e_ref, qg_ref, kg_ref, cf_ref, s1_ref, s2_ref,
                q_ref, k_ref, v_ref, *, d, scale):
    h = _rms_rows(x_ref[...], g_ref[...]).astype(BF16)
    qkv = jnp.dot(h, w_ref[...], preferred_element_type=F32)
    e = e_ref[...]
    cf, s1, s2 = cf_ref[...], s1_ref[...], s2_ref[...]
    kd = k_ref.shape[1]

    def head_norm_rope(blk, gain, out_scale):
        ss = jnp.dot((blk * blk).astype(BF16), e, preferred_element_type=F32)
        n = blk * lax.rsqrt(ss * (1.0 / HEAD_DIM) + EPS) * gain
        outs = []
        for u in range(MXU_DIM // LANES):
            xb = n[:, u * LANES:(u + 1) * LANES]
            r = (xb * cf + pltpu.roll(xb, LANES - ROT_HALF, axis=1) * s1
                 + pltpu.roll(xb, ROT_HALF, axis=1) * s2)
            outs.append(r * out_scale if out_scale != 1.0 else r)
        return jnp.concatenate(outs, axis=1)

    for c in range(d // MXU_DIM):
        blk = qkv[:, c * MXU_DIM:(c + 1) * MXU_DIM]
        q_ref[:, c * MXU_DIM:(c + 1) * MXU_DIM] = head_norm_rope(blk, qg_ref[...], scale).astype(BF16)
    for c in range(kd // MXU_DIM):
        blk = qkv[:, d + c * MXU_DIM:d + (c + 1) * MXU_DIM]
        k_ref[:, c * MXU_DIM:(c + 1) * MXU_DIM] = head_norm_rope(blk, kg_ref[...], 1.0).astype(BF16)
    v_ref[...] = qkv[:, d + kd:].astype(BF16)


def _qkv(x2d, gain, w, e, qg, kg, cf, s1, s2, seq, tm):
    t, d = x2d.shape
    kd = N_KV * LANES
    nt = seq // tm
    tab_spec = pl.BlockSpec((tm, LANES), lambda i: (i % nt, 0))
    const = lambda shape: pl.BlockSpec(shape, lambda i: (0, 0))
    return pl.pallas_call(
        functools.partial(_qkv_kernel, d=d, scale=1.0 / math.sqrt(HEAD_DIM)),
        grid=(t // tm,),
        in_specs=[
            pl.BlockSpec((tm, d), lambda i: (i, 0)),
            const((1, d)),
            const((d, d + 2 * kd)),
            const((MXU_DIM, MXU_DIM)),
            const((1, MXU_DIM)),
            const((1, MXU_DIM)),
            tab_spec, tab_spec, tab_spec,
        ],
        out_specs=[
            pl.BlockSpec((tm, d), lambda i: (i, 0)),
            pl.BlockSpec((tm, kd), lambda i: (i, 0)),
            pl.BlockSpec((tm, kd), lambda i: (i, 0)),
        ],
        out_shape=[
            jax.ShapeDtypeStruct((t, d), BF16),
            jax.ShapeDtypeStruct((t, kd), BF16),
            jax.ShapeDtypeStruct((t, kd), BF16),
        ],
        compiler_params=_cparams(("parallel",)),
        name="qkv_norm_rope",
    )(x2d, gain, w, e, qg, kg, cf, s1, s2)


def _attn_kernel(sink_ref, q_ref, kp_ref, kc_ref, kn_ref, vp_ref, vc_ref, vn_ref, o_ref, *, nb):
    j = pl.program_id(1)
    bq = WINDOW_BLOCK
    npair = GROUP // 2
    lane = lax.broadcasted_iota(jnp.int32, (bq, LANES), 1)
    lo = lane < HEAD_DIM
    row = lax.broadcasted_iota(jnp.int32, (bq, 3 * bq), 0)
    col = lax.broadcasted_iota(jnp.int32, (bq, 3 * bq), 1)
    seg = col // bq
    kk = col - seg * bq
    valid = ((seg == 1)
             | ((seg == 0) & (kk >= row) & (j > 0))
             | ((seg == 2) & (kk <= row) & (j < nb - 1)))
    zero = jnp.zeros((bq, LANES), BF16)
    lo3 = lax.broadcasted_iota(jnp.int32, (3 * bq, LANES), 1) < HEAD_DIM
    zero3 = jnp.zeros((3 * bq, LANES), BF16)

    for h in range(N_KV):
        ks = slice(h * LANES, (h + 1) * LANES)
        k2 = jnp.concatenate([kp_ref[0, :, ks], kc_ref[0, :, ks], kn_ref[0, :, ks]], axis=0)
        v2 = jnp.concatenate([vp_ref[0, :, ks], vc_ref[0, :, ks], vn_ref[0, :, ks]], axis=0)
        qparts = []
        for parity in range(2):
            for p in range(npair):
                c0 = (h * npair + p) * LANES
                blk = q_ref[0, :, c0:c0 + LANES]
                qparts.append(jnp.where(lo if parity == 0 else ~lo, blk, zero))
        qs = jnp.concatenate(qparts, axis=0)
        s = lax.dot_general(qs, k2, (((1,), (1,)), ((), ())), preferred_element_type=F32)
        ps, inv = [], []
        for r in range(GROUP):
            parity, p = divmod(r, npair)
            sink = sink_ref[h * GROUP + 2 * p + parity]
            sr = jnp.where(valid, s[r * bq:(r + 1) * bq], NEG)
            m = jnp.maximum(jnp.max(sr, axis=-1, keepdims=True), sink)
            e = jnp.exp(sr - m)
            denom = jnp.sum(e, axis=-1, keepdims=True) + jnp.exp(sink - m)
            ps.append(e.astype(BF16))
            inv.append(1.0 / denom)
        pe = jnp.concatenate(ps[:npair], axis=0)
        po = jnp.concatenate(ps[npair:], axis=0)
        o = (jnp.dot(pe, jnp.where(lo3, v2, zero3), preferred_element_type=F32)
             + jnp.dot(po, jnp.where(lo3, zero3, v2), preferred_element_type=F32))
        for p in range(npair):
            scale = jnp.where(lo, inv[p], inv[npair + p])
            c0 = (h * npair + p) * LANES
            o_ref[0, :, c0:c0 + LANES] = (o[p * bq:(p + 1) * bq] * scale).astype(BF16)


def _attention(sink, q, k, v):
    b, s, d = q.shape
    kd = k.shape[-1]
    bq = WINDOW_BLOCK
    nb = s // bq
    prev = lambda bi, j: (bi, jnp.maximum(j - 1, 0), 0)
    cur = lambda bi, j: (bi, j, 0)
    nxt = lambda bi, j: (bi, jnp.minimum(j + 1, nb - 1), 0)
    kv = lambda im: pl.BlockSpec((1, bq, kd), im)
    return pl.pallas_call(
        functools.partial(_attn_kernel, nb=nb),
        grid=(b, nb),
        in_specs=[
            pl.BlockSpec(memory_space=pltpu.SMEM),
            pl.BlockSpec((1, bq, d), cur),
            kv(prev), kv(cur), kv(nxt), kv(prev), kv(cur), kv(nxt),
        ],
        out_specs=pl.BlockSpec((1, bq, d), cur),
        out_shape=jax.ShapeDtypeStruct((b, s, d), BF16),
        compiler_params=_cparams(("parallel", "parallel")),
        name="window_attention",
    )(sink, q, k, k, k, v, v, v)


def _chan_table(gd):
    k = np.arange(gd)
    ang = 2.0 * np.pi * ((k[:, None] * k[None, :]) % gd) / gd
    sc = 1.0 / math.sqrt(gd)
    return jnp.asarray(np.concatenate([np.cos(ang) * sc, -np.sin(ang) * sc], axis=1), BF16)


def _stage1_table(n1, n2):
    n = n1 * n2
    k1 = jnp.arange(n1, dtype=jnp.int32)
    pos = n2 * jnp.arange(n1, dtype=jnp.int32)[None, None, :] + jnp.arange(n2, dtype=jnp.int32)[:, None, None]
    ang = ((k1[None, :, None] * pos) % n).astype(F32) * (2.0 * math.pi / n)
    c, s = jnp.cos(ang), jnp.sin(ang)
    top = jnp.concatenate([c, s], axis=-1)
    bot = jnp.concatenate([-s, c], axis=-1)
    return jnp.concatenate([top, bot], axis=-2).astype(BF16)


def _stage2_table(n1, n2):
    k = np.arange(n2)
    ang = 2.0 * np.pi * ((k[:, None] * k[None, :]) % n2) / n2
    sc = 1.0 / math.sqrt(n1 * n2)
    return jnp.asarray(np.concatenate([np.cos(ang) * sc, np.sin(ang) * sc], axis=1), BF16)


def _rope_lane_tables(s):
    inv_freq = ROPE_THETA ** (-jnp.arange(0, ROT_DIM, 2, dtype=F32) / ROT_DIM)
    ang = jnp.arange(s, dtype=F32)[:, None] * inv_freq[None, :]
    cos, sin = jnp.cos(ang), jnp.sin(ang)
    pad = HEAD_DIM - ROT_DIM
    ones = jnp.ones((s, pad), F32)
    zeros = jnp.zeros((s, pad), F32)
    zh = jnp.zeros((s, ROT_HALF), F32)
    cf = jnp.concatenate([cos, cos, ones], axis=1)
    s1 = jnp.concatenate([-sin, zh, zeros], axis=1)
    s2 = jnp.concatenate([zh, sin, zeros], axis=1)
    rep = LANES // HEAD_DIM
    return tuple(jnp.tile(a, (1, rep)) for a in (cf, s1, s2))


def _dup_heads(w):
    d = w.shape[0]
    w4 = w.reshape(d, N_KV, 1, HEAD_DIM)
    return jnp.broadcast_to(w4, (d, N_KV, LANES // HEAD_DIM, HEAD_DIM)).reshape(d, N_KV * LANES)


def _segment_ones():
    i = np.arange(MXU_DIM) // HEAD_DIM
    return jnp.asarray((i[:, None] == i[None, :]).astype(np.float32), BF16)


def _fourier_mixer(x, gain, w_out, mlp_gain, tm):
    b, s, d = x.shape
    gd = d // FFT_GROUPS
    n1, n2 = _seq_split(s)
    y = _fourier_stage1(x.reshape(b, n1, n2, d), gain, _chan_table(gd), _stage1_table(n1, n2))
    f = _fourier_stage2(y, _stage2_table(n1, n2))
    return _proj_norm(f.reshape(b * s, d), x.reshape(b * s, d), w_out, mlp_gain, tm)


def _attn_mixer(x2d, b, s, gain, w_qkv, q_gain, k_gain, sink, w_o, mlp_gain, tm):
    t, d = x2d.shape
    rep = MXU_DIM // HEAD_DIM
    cf, s1, s2 = _rope_lane_tables(s)
    q, k, v = _qkv(x2d, gain, w_qkv, _segment_ones(),
                   jnp.tile(q_gain, rep)[None, :], jnp.tile(k_gain, rep)[None, :],
                   cf, s1, s2, s, tm)
    kd = k.shape[-1]
    o = _attention(sink, q.reshape(b, s, d), k.reshape(b, s, kd), v.reshape(b, s, kd))
    return _proj_norm(o.reshape(t, d), x2d, w_o, mlp_gain, tm)


def _trunk(x, p, tm, tm_mlp, tf):
    b, s, d = x.shape
    x1, h = _fourier_mixer(x, p["fourier_norm"][0][None, :], p["fourier_w_out"][0],
                           p["mlp_norm"][0][None, :], tm)
    x2 = _mlp(h, x1, p["mlp_w_up"][0], p["mlp_w_down"][0], tm_mlp, tf)
    x3, h = _attn_mixer(x2, b, s, p["attn_norm"][0][None, :], p["attn_w_qkv"], p["attn_q_norm"][0],
                        p["attn_k_norm"][0], p["attn_sink"][0], p["attn_w_o"][0],
                        p["mlp_norm"][1][None, :], tm)
    x4 = _mlp(h, x3, p["mlp_w_up"][1], p["mlp_w_down"][1], tm_mlp, tf)
    return x4.reshape(b, s, d)


def kernel(x_prompt, x_sample, fourier_norm, fourier_w_out, attn_norm, attn_w_qkv, attn_q_norm,
           attn_k_norm, attn_sink, attn_w_o, mlp_norm, mlp_w_up, mlp_w_down):
    d = x_prompt.shape[-1]
    w_qkv = attn_w_qkv[0]
    kvd = N_KV * HEAD_DIM
    w_qkv = jnp.concatenate(
        [w_qkv[:, :d], _dup_heads(w_qkv[:, d:d + kvd]), _dup_heads(w_qkv[:, d + kvd:])], axis=1)
    p = dict(
        fourier_norm=fourier_norm, fourier_w_out=fourier_w_out.astype(BF16),
        attn_norm=attn_norm, attn_w_qkv=w_qkv.astype(BF16), attn_q_norm=attn_q_norm,
        attn_k_norm=attn_k_norm, attn_sink=attn_sink, attn_w_o=attn_w_o.astype(BF16),
        mlp_norm=mlp_norm, mlp_w_up=mlp_w_up.astype(BF16), mlp_w_down=mlp_w_down.astype(BF16),
    )
    outs = []
    for x in (x_prompt, x_sample):
        t = x.shape[0] * x.shape[1]
        tm = min(512, x.shape[1])
        tm_mlp = min(1024, t)
        tf = min(512, mlp_w_up.shape[-1])
        outs.append(_trunk(x, p, tm, tm_mlp, tf))
    return tuple(outs)
```

```python
import functools
import math

import numpy as np
import jax
import jax.numpy as jnp
from jax import lax
from jax.experimental import pallas as pl
from jax.experimental.pallas import tpu as pltpu

F32 = jnp.float32
BF16 = jnp.bfloat16

FFT_GROUPS = 8
HEAD_DIM = 64
N_KV = 4
GROUP = 8
WINDOW_BLOCK = 128
ROPE_THETA = 500000.0
ROT_DIM = HEAD_DIM // 4
ROT_HALF = ROT_DIM // 2
EPS = 1e-6
NEG = -1e30
LANES = 128
MXU_DIM = 256
VMEM_LIMIT = 60 * 1024 * 1024


def _cparams(sem):
    return pltpu.CompilerParams(dimension_semantics=sem, vmem_limit_bytes=VMEM_LIMIT)


def _rms_rows(x, gain):
    ms = jnp.mean(x * x, axis=-1, keepdims=True)
    return x * lax.rsqrt(ms + EPS) * gain


def _seq_split(s):
    n1 = 1 << ((s.bit_length() - 1 + 1) // 2)
    n2 = s // n1
    assert n1 * n2 == s
    return n1, n2


SUBLANES = 8


def _pack_complex(re, im):
    rb = lax.bitcast_convert_type(re.astype(BF16).astype(F32), jnp.uint32)
    ib = lax.bitcast_convert_type(im.astype(BF16).astype(F32), jnp.uint32)
    return rb | (ib >> 16)


def _unpack_complex(u):
    re = lax.bitcast_convert_type(u & jnp.uint32(0xFFFF0000), F32)
    im = lax.bitcast_convert_type(u << 16, F32)
    return re, im


def _fourier_stage1_kernel(x_ref, g_ref, cs_ref, tab_ref, o_ref, zs_ref, *, gd):
    n1 = x_ref.shape[1]
    rows = n1 * SUBLANES
    d = x_ref.shape[3]
    nblk = 2 * gd // LANES
    h = _rms_rows(x_ref[0].reshape(rows, d), g_ref[...]).astype(BF16)
    cs = cs_ref[...]
    for g in range(FFT_GROUPS):
        slot = g % 2
        r = jnp.dot(h[:, g * gd:(g + 1) * gd], cs, preferred_element_type=F32)
        for c in range(nblk):
            zs_ref[slot, c] = r[:, c * LANES:(c + 1) * LANES]
        for j in range(SUBLANES):
            parts = [zs_ref[slot, c, pl.ds(j, n1, stride=SUBLANES), :] for c in range(nblk)]
            zr = jnp.concatenate(parts[:nblk // 2], axis=1)
            zi = jnp.concatenate(parts[nblk // 2:], axis=1)
            zc = jnp.concatenate([zr, zi], axis=0).astype(BF16)
            y = jnp.dot(tab_ref[j], zc, preferred_element_type=F32)
            o_ref[0, j, :, g * gd:(g + 1) * gd] = _pack_complex(y[:n1], y[n1:])


def _fourier_stage1(x4, gain, cs, tab):
    b, n1, n2, d = x4.shape
    gd = d // FFT_GROUPS
    rows = n1 * SUBLANES
    return pl.pallas_call(
        functools.partial(_fourier_stage1_kernel, gd=gd),
        grid=(b, n2 // SUBLANES),
        in_specs=[
            pl.BlockSpec((1, n1, SUBLANES, d), lambda bi, j: (bi, 0, j, 0)),
            pl.BlockSpec((1, d), lambda bi, j: (0, 0)),
            pl.BlockSpec((gd, 2 * gd), lambda bi, j: (0, 0)),
            pl.BlockSpec((SUBLANES, 2 * n1, 2 * n1), lambda bi, j: (j, 0, 0)),
        ],
        out_specs=pl.BlockSpec((1, SUBLANES, n1, d), lambda bi, j: (bi, j, 0, 0)),
        out_shape=jax.ShapeDtypeStruct((b, n2, n1, d), jnp.uint32),
        scratch_shapes=[pltpu.VMEM((2, 2 * gd // LANES, rows, LANES), F32)],
        compiler_params=_cparams(("parallel", "parallel")),
        name="fourier_stage1",
    )(x4, gain, cs, tab)


def _fourier_stage2_kernel(y_ref, tab_ref, o_ref, ys_ref, fs_ref, *, cw):
    n2 = y_ref.shape[1]
    rows = n2 * SUBLANES
    d = y_ref.shape[3]
    nblk = cw // LANES
    tab = tab_ref[...]
    for ch in range(d // cw):
        slot = ch % 2
        for c in range(nblk):
            lanes = slice(ch * cw + c * LANES, ch * cw + (c + 1) * LANES)
            ys_ref[slot, c] = y_ref[0, :, :, lanes].reshape(rows, LANES)
        for k in range(SUBLANES):
            u = jnp.concatenate(
                [ys_ref[slot, c, pl.ds(k, n2, stride=SUBLANES), :] for c in range(nblk)], axis=1)
            re, im = _unpack_complex(u)
            zc = jnp.concatenate([re, im], axis=0).astype(BF16)
            f = jnp.dot(tab, zc, preferred_element_type=F32)
            for c in range(nblk):
                fs_ref[slot, c, pl.ds(k, n2, stride=SUBLANES), :] = f[:, c * LANES:(c + 1) * LANES]
        for c in range(nblk):
            lanes = slice(ch * cw + c * LANES, ch * cw + (c + 1) * LANES)
            o_ref[0, :, :, lanes] = fs_ref[slot, c].reshape(n2, SUBLANES, LANES)


def _fourier_stage2(y, tab):
    b, n2, n1, d = y.shape
    cw = MXU_DIM
    rows = n2 * SUBLANES
    return pl.pallas_call(
        functools.partial(_fourier_stage2_kernel, cw=cw),
        grid=(b, n1 // SUBLANES),
        in_specs=[
            pl.BlockSpec((1, n2, SUBLANES, d), lambda bi, k: (bi, 0, k, 0)),
            pl.BlockSpec((n2, 2 * n2), lambda bi, k: (0, 0)),
        ],
        out_specs=pl.BlockSpec((1, n2, SUBLANES, d), lambda bi, k: (bi, 0, k, 0)),
        out_shape=jax.ShapeDtypeStruct((b, n2, n1, d), F32),
        scratch_shapes=[pltpu.VMEM((2, cw // LANES, rows, LANES), jnp.uint32),
                        pltpu.VMEM((2, cw // LANES, rows, LANES), F32)],
        compiler_params=_cparams(("parallel", "parallel")),
        name="fourier_stage2",
    )(y, tab)


def _proj_norm_kernel(a_ref, x_ref, w_ref, g_ref, x1_ref, h_ref):
    x1 = x_ref[...] + jnp.dot(a_ref[...].astype(BF16), w_ref[...], preferred_element_type=F32)
    x1_ref[...] = x1
    h_ref[...] = _rms_rows(x1, g_ref[...]).astype(BF16)


def _proj_norm(a, x2d, w, gain, tm):
    t, d = x2d.shape
    return pl.pallas_call(
        _proj_norm_kernel,
        grid=(t // tm,),
        in_specs=[
            pl.BlockSpec((tm, d), lambda i: (i, 0)),
            pl.BlockSpec((tm, d), lambda i: (i, 0)),
            pl.BlockSpec((None, d, d), lambda i: (0, 0, 0)),
            pl.BlockSpec((1, d), lambda i: (0, 0)),
        ],
        out_specs=[
            pl.BlockSpec((tm, d), lambda i: (i, 0)),
            pl.BlockSpec((tm, d), lambda i: (i, 0)),
        ],
        out_shape=[jax.ShapeDtypeStruct((t, d), F32), jax.ShapeDtypeStruct((t, d), BF16)],
        compiler_params=_cparams(("parallel",)),
        name="proj_norm",
    )(a, x2d, w, gain)


def _mlp_kernel(h_ref, x_ref, wu_ref, wd_ref, o_ref):
    @pl.when(pl.program_id(1) == 0)
    def _():
        o_ref[...] = x_ref[...]

    a = jnp.maximum(jnp.dot(h_ref[...], wu_ref[...], preferred_element_type=F32), 0.0)
    o_ref[...] += jnp.dot((a * a).astype(BF16), wd_ref[...], preferred_element_type=F32)


def _mlp(h, x1, w_up, w_down, layer, tm, tf):
    t, d = x1.shape
    dff = w_up.shape[-1]
    return pl.pallas_call(
        _mlp_kernel,
        grid=(t // tm, dff // tf),
        in_specs=[
            pl.BlockSpec((tm, d), lambda i, j: (i, 0)),
            pl.BlockSpec((tm, d), lambda i, j: (i, 0)),
            pl.BlockSpec((None, d, tf), lambda i, j: (layer, 0, j)),
            pl.BlockSpec((None, tf, d), lambda i, j: (layer, j, 0)),
        ],
        out_specs=pl.BlockSpec((tm, d), lambda i, j: (i, 0)),
        out_shape=jax.ShapeDtypeStruct((t, d), F32),
        compiler_params=_cparams(("parallel", "arbitrary")),
        name="sqrelu_mlp",
    )(h, x1, w_up, w_down)


def _qkv_kernel(x_ref, g_ref, w_ref, e_ref, qg_ref, kg_ref, cf_ref, s1_ref, s2_ref,
                q_ref, k_ref, v_ref, *, d, scale):
    h = _rms_rows(x_ref[...], g_ref[...]).astype(BF16)
    e = e_ref[...]
    cf, s1, s2 = cf_ref[...], s1_ref[...], s2_ref[...]
    kvd = N_KV * HEAD_DIM
    lo = lax.broadcasted_iota(jnp.int32, (h.shape[0], LANES), 1) < HEAD_DIM

    def proj(c0, width):
        return jnp.dot(h, w_ref[:, c0:c0 + width], preferred_element_type=F32)

    def head_norm_rope(blk, gain, out_scale):
        ss = jnp.dot((blk * blk).astype(BF16), e, preferred_element_type=F32)
        n = blk * lax.rsqrt(ss * (1.0 / HEAD_DIM) + EPS) * gain
        outs = []
        for u in range(MXU_DIM // LANES):
            xb = n[:, u * LANES:(u + 1) * LANES]
            r = (xb * cf + pltpu.roll(xb, LANES - ROT_HALF, axis=1) * s1
                 + pltpu.roll(xb, ROT_HALF, axis=1) * s2)
            outs.append(r * out_scale if out_scale != 1.0 else r)
        return outs

    def store_duplicated(ref, halves):
        for u, xb in enumerate(halves):
            xr = pltpu.roll(xb, HEAD_DIM, axis=1)
            ref[:, (2 * u) * LANES:(2 * u + 1) * LANES] = jnp.where(lo, xb, xr).astype(BF16)
            ref[:, (2 * u + 1) * LANES:(2 * u + 2) * LANES] = jnp.where(lo, xr, xb).astype(BF16)

    pw = 2 * MXU_DIM
    for c in range(d // pw):
        blk = proj(c * pw, pw)
        outs = (head_norm_rope(blk[:, :MXU_DIM], qg_ref[...], scale)
                + head_norm_rope(blk[:, MXU_DIM:], qg_ref[...], scale))
        q_ref[:, c * pw:(c + 1) * pw] = jnp.concatenate(outs, axis=1).astype(BF16)
    assert kvd == MXU_DIM
    kv = proj(d, 2 * kvd)
    store_duplicated(k_ref, head_norm_rope(kv[:, :kvd], kg_ref[...], 1.0))
    store_duplicated(v_ref, [kv[:, kvd + u * LANES:kvd + (u + 1) * LANES] for u in range(kvd // LANES)])


def _qkv(x2d, gain, w, e, qg, kg, cf, s1, s2, seq, tm):
    t, d = x2d.shape
    kd = N_KV * LANES
    nt = seq // tm
    tab_spec = pl.BlockSpec((tm, LANES), lambda i: (i % nt, 0))
    const = lambda shape: pl.BlockSpec(shape, lambda i: (0, 0))
    return pl.pallas_call(
        functools.partial(_qkv_kernel, d=d, scale=1.0 / math.sqrt(HEAD_DIM)),
        grid=(t // tm,),
        in_specs=[
            pl.BlockSpec((tm, d), lambda i: (i, 0)),
            const((1, d)),
            pl.BlockSpec((None,) + w.shape[1:], lambda i: (0, 0, 0)),
            const((MXU_DIM, MXU_DIM)),
            const((1, MXU_DIM)),
            const((1, MXU_DIM)),
            tab_spec, tab_spec, tab_spec,
        ],
        out_specs=[
            pl.BlockSpec((tm, d), lambda i: (i, 0)),
            pl.BlockSpec((tm, kd), lambda i: (i, 0)),
            pl.BlockSpec((tm, kd), lambda i: (i, 0)),
        ],
        out_shape=[
            jax.ShapeDtypeStruct((t, d), BF16),
            jax.ShapeDtypeStruct((t, kd), BF16),
            jax.ShapeDtypeStruct((t, kd), BF16),
        ],
        compiler_params=_cparams(("parallel",)),
        name="qkv_norm_rope",
    )(x2d, gain, w, e, qg, kg, cf, s1, s2)


def _attn_kernel(sink_ref, q_ref, kp_ref, kc_ref, kn_ref, vp_ref, vc_ref, vn_ref, o_ref, *, nb):
    j = pl.program_id(1)
    bq = WINDOW_BLOCK
    npair = GROUP // 2
    lane = lax.broadcasted_iota(jnp.int32, (bq, LANES), 1)
    lo = lane < HEAD_DIM
    row = lax.broadcasted_iota(jnp.int32, (bq, 3 * bq), 0)
    col = lax.broadcasted_iota(jnp.int32, (bq, 3 * bq), 1)
    seg = col // bq
    kk = col - seg * bq
    valid = ((seg == 1)
             | ((seg == 0) & (kk >= row) & (j > 0))
             | ((seg == 2) & (kk <= row) & (j < nb - 1)))
    zero = jnp.zeros((bq, LANES), BF16)
    lo3 = lax.broadcasted_iota(jnp.int32, (3 * bq, LANES), 1) < HEAD_DIM
    zero3 = jnp.zeros((3 * bq, LANES), BF16)

    for h in range(N_KV):
        ks = slice(h * LANES, (h + 1) * LANES)
        k2 = jnp.concatenate([kp_ref[0, :, ks], kc_ref[0, :, ks], kn_ref[0, :, ks]], axis=0)
        v2 = jnp.concatenate([vp_ref[0, :, ks], vc_ref[0, :, ks], vn_ref[0, :, ks]], axis=0)
        qparts = []
        for parity in range(2):
            for p in range(npair):
                c0 = (h * npair + p) * LANES
                blk = q_ref[0, :, c0:c0 + LANES]
                qparts.append(jnp.where(lo if parity == 0 else ~lo, blk, zero))
        qs = jnp.concatenate(qparts, axis=0)
        s = lax.dot_general(qs, k2, (((1,), (1,)), ((), ())), preferred_element_type=F32)
        ps, inv = [], []
        for r in range(GROUP):
            parity, p = divmod(r, npair)
            sink = sink_ref[h * GROUP + 2 * p + parity]
            sr = jnp.where(valid, s[r * bq:(r + 1) * bq], NEG)
            m = jnp.maximum(jnp.max(sr, axis=-1, keepdims=True), sink)
            e = jnp.exp(sr - m)
            denom = jnp.sum(e, axis=-1, keepdims=True) + jnp.exp(sink - m)
            ps.append(e.astype(BF16))
            inv.append(1.0 / denom)
        pe = jnp.concatenate(ps[:npair], axis=0)
        po = jnp.concatenate(ps[npair:], axis=0)
        o = (jnp.dot(pe, jnp.where(lo3, v2, zero3), preferred_element_type=F32)
             + jnp.dot(po, jnp.where(lo3, zero3, v2), preferred_element_type=F32))
        for p in range(npair):
            scale = jnp.where(lo, inv[p], inv[npair + p])
            c0 = (h * npair + p) * LANES
            o_ref[0, :, c0:c0 + LANES] = (o[p * bq:(p + 1) * bq] * scale).astype(BF16)


def _attention(sink, q, k, v):
    b, s, d = q.shape
    kd = k.shape[-1]
    bq = WINDOW_BLOCK
    nb = s // bq
    prev = lambda bi, j: (bi, jnp.maximum(j - 1, 0), 0)
    cur = lambda bi, j: (bi, j, 0)
    nxt = lambda bi, j: (bi, jnp.minimum(j + 1, nb - 1), 0)
    kv = lambda im: pl.BlockSpec((1, bq, kd), im)
    return pl.pallas_call(
        functools.partial(_attn_kernel, nb=nb),
        grid=(b, nb),
        in_specs=[
            pl.BlockSpec(memory_space=pltpu.SMEM),
            pl.BlockSpec((1, bq, d), cur),
            kv(prev), kv(cur), kv(nxt), kv(prev), kv(cur), kv(nxt),
        ],
        out_specs=pl.BlockSpec((1, bq, d), cur),
        out_shape=jax.ShapeDtypeStruct((b, s, d), BF16),
        compiler_params=_cparams(("parallel", "parallel")),
        name="window_attention",
    )(sink, q, k, k, k, v, v, v)


def _chan_table(gd):
    k = np.arange(gd)
    ang = 2.0 * np.pi * ((k[:, None] * k[None, :]) % gd) / gd
    sc = 1.0 / math.sqrt(gd)
    return jnp.asarray(np.concatenate([np.cos(ang) * sc, -np.sin(ang) * sc], axis=1), BF16)


def _stage1_table(n1, n2):
    n = n1 * n2
    k1 = jnp.arange(n1, dtype=jnp.int32)
    pos = n2 * jnp.arange(n1, dtype=jnp.int32)[None, None, :] + jnp.arange(n2, dtype=jnp.int32)[:, None, None]
    ang = ((k1[None, :, None] * pos) % n).astype(F32) * (2.0 * math.pi / n)
    c, s = jnp.cos(ang), jnp.sin(ang)
    top = jnp.concatenate([c, s], axis=-1)
    bot = jnp.concatenate([-s, c], axis=-1)
    return jnp.concatenate([top, bot], axis=-2).astype(BF16)


def _stage2_table(n1, n2):
    k = np.arange(n2)
    ang = 2.0 * np.pi * ((k[:, None] * k[None, :]) % n2) / n2
    sc = 1.0 / math.sqrt(n1 * n2)
    return jnp.asarray(np.concatenate([np.cos(ang) * sc, np.sin(ang) * sc], axis=1), BF16)


def _rope_lane_tables(s):
    inv_freq = ROPE_THETA ** (-jnp.arange(0, ROT_DIM, 2, dtype=F32) / ROT_DIM)
    ang = jnp.arange(s, dtype=F32)[:, None] * inv_freq[None, :]
    cos, sin = jnp.cos(ang), jnp.sin(ang)
    pad = HEAD_DIM - ROT_DIM
    ones = jnp.ones((s, pad), F32)
    zeros = jnp.zeros((s, pad), F32)
    zh = jnp.zeros((s, ROT_HALF), F32)
    cf = jnp.concatenate([cos, cos, ones], axis=1)
    s1 = jnp.concatenate([-sin, zh, zeros], axis=1)
    s2 = jnp.concatenate([zh, sin, zeros], axis=1)
    rep = LANES // HEAD_DIM
    return tuple(jnp.tile(a, (1, rep)) for a in (cf, s1, s2))


def _segment_ones():
    i = np.arange(MXU_DIM) // HEAD_DIM
    return jnp.asarray((i[:, None] == i[None, :]).astype(np.float32), BF16)


def _fourier_mixer(x, gain, w_out, mlp_gain, tm):
    b, s, d = x.shape
    gd = d // FFT_GROUPS
    n1, n2 = _seq_split(s)
    y = _fourier_stage1(x.reshape(b, n1, n2, d), gain, _chan_table(gd), _stage1_table(n1, n2))
    f = _fourier_stage2(y, _stage2_table(n1, n2))
    return _proj_norm(f.reshape(b * s, d), x.reshape(b * s, d), w_out, mlp_gain, tm)


def _attn_mixer(x2d, b, s, gain, w_qkv, q_gain, k_gain, sink, w_o, mlp_gain, tm):
    t, d = x2d.shape
    rep = MXU_DIM // HEAD_DIM
    cf, s1, s2 = _rope_lane_tables(s)
    q, k, v = _qkv(x2d, gain, w_qkv, _segment_ones(),
                   jnp.tile(q_gain, rep)[None, :], jnp.tile(k_gain, rep)[None, :],
                   cf, s1, s2, s, tm)
    kd = k.shape[-1]
    o = _attention(sink, q.reshape(b, s, d), k.reshape(b, s, kd), v.reshape(b, s, kd))
    return _proj_norm(o.reshape(t, d), x2d, w_o, mlp_gain, tm)


def _trunk(x, p, tm, tm_mlp, tf):
    b, s, d = x.shape
    x1, h = _fourier_mixer(x, p["fourier_norm"][0][None, :], p["fourier_w_out"],
                           p["mlp_norm"][0][None, :], tm)
    x2 = _mlp(h, x1, p["mlp_w_up"], p["mlp_w_down"], 0, tm_mlp, tf)
    x3, h = _attn_mixer(x2, b, s, p["attn_norm"][0][None, :], p["attn_w_qkv"], p["attn_q_norm"][0],
                        p["attn_k_norm"][0], p["attn_sink"][0], p["attn_w_o"],
                        p["mlp_norm"][1][None, :], tm)
    x4 = _mlp(h, x3, p["mlp_w_up"], p["mlp_w_down"], 1, tm_mlp, tf)
    return x4.reshape(b, s, d)


def kernel(x_prompt, x_sample, fourier_norm, fourier_w_out, attn_norm, attn_w_qkv, attn_q_norm,
           attn_k_norm, attn_sink, attn_w_o, mlp_norm, mlp_w_up, mlp_w_down):
    p = dict(
        fourier_norm=fourier_norm, fourier_w_out=fourier_w_out.astype(BF16),
        attn_norm=attn_norm, attn_w_qkv=attn_w_qkv.astype(BF16), attn_q_norm=attn_q_norm,
        attn_k_norm=attn_k_norm, attn_sink=attn_sink, attn_w_o=attn_w_o.astype(BF16),
        mlp_norm=mlp_norm, mlp_w_up=mlp_w_up.astype(BF16), mlp_w_down=mlp_w_down.astype(BF16),
    )
    outs = []
    for x in (x_prompt, x_sample):
        t = x.shape[0] * x.shape[1]
        tm = min(512, x.shape[1])
        tm_mlp = min(1024, t)
        tf = min(512, mlp_w_up.shape[-1])
        outs.append(_trunk(x, p, tm, tm_mlp, tf))
    return tuple(outs)
```

```python
import functools
import math

import numpy as np
import jax
import jax.numpy as jnp
from jax import lax
from jax.experimental import pallas as pl
from jax.experimental.pallas import tpu as pltpu

F32 = jnp.float32
BF16 = jnp.bfloat16

FFT_GROUPS = 8
HEAD_DIM = 64
N_KV = 4
GROUP = 8
WINDOW_BLOCK = 128
ROPE_THETA = 500000.0
ROT_DIM = HEAD_DIM // 4
ROT_HALF = ROT_DIM // 2
EPS = 1e-6
NEG = -1e30
LANES = 128
MXU_DIM = 256
VMEM_LIMIT = 60 * 1024 * 1024


def _cparams(sem):
    return pltpu.CompilerParams(dimension_semantics=sem, vmem_limit_bytes=VMEM_LIMIT)


def _rms_rows(x, gain):
    ms = jnp.mean(x * x, axis=-1, keepdims=True)
    return x * lax.rsqrt(ms + EPS) * gain


def _seq_split(s):
    n1 = 1 << ((s.bit_length() - 1 + 1) // 2)
    n2 = s // n1
    assert n1 * n2 == s
    return n1, n2


SUBLANES = 8


def _pack_complex(re, im):
    rb = lax.bitcast_convert_type(re.astype(BF16).astype(F32), jnp.uint32)
    ib = lax.bitcast_convert_type(im.astype(BF16).astype(F32), jnp.uint32)
    return rb | (ib >> 16)


def _unpack_complex(u):
    re = lax.bitcast_convert_type(u & jnp.uint32(0xFFFF0000), F32)
    im = lax.bitcast_convert_type(u << 16, F32)
    return re, im


def _fourier_stage1_kernel(x_ref, g_ref, cs_ref, tab_ref, o_ref, zs_ref, *, gd):
    n1 = x_ref.shape[1]
    rows = n1 * SUBLANES
    d = x_ref.shape[3]
    nblk = 2 * gd // LANES
    h = _rms_rows(x_ref[0].reshape(rows, d), g_ref[...]).astype(BF16)
    cs = cs_ref[...]
    for g in range(FFT_GROUPS):
        slot = g % 2
        r = jnp.dot(h[:, g * gd:(g + 1) * gd], cs, preferred_element_type=F32)
        for c in range(nblk):
            zs_ref[slot, c] = r[:, c * LANES:(c + 1) * LANES]
        for j in range(SUBLANES):
            parts = [zs_ref[slot, c, pl.ds(j, n1, stride=SUBLANES), :] for c in range(nblk)]
            zr = jnp.concatenate(parts[:nblk // 2], axis=1)
            zi = jnp.concatenate(parts[nblk // 2:], axis=1)
            zc = jnp.concatenate([zr, zi], axis=0).astype(BF16)
            y = jnp.dot(tab_ref[j], zc, preferred_element_type=F32)
            o_ref[0, j, :, g * gd:(g + 1) * gd] = _pack_complex(y[:n1], y[n1:])


def _fourier_stage1(x4, gain, cs, tab):
    b, n1, n2, d = x4.shape
    gd = d // FFT_GROUPS
    rows = n1 * SUBLANES
    return pl.pallas_call(
        functools.partial(_fourier_stage1_kernel, gd=gd),
        grid=(b, n2 // SUBLANES),
        in_specs=[
            pl.BlockSpec((1, n1, SUBLANES, d), lambda bi, j: (bi, 0, j, 0)),
            pl.BlockSpec((1, d), lambda bi, j: (0, 0)),
            pl.BlockSpec((gd, 2 * gd), lambda bi, j: (0, 0)),
            pl.BlockSpec((SUBLANES, 2 * n1, 2 * n1), lambda bi, j: (j, 0, 0)),
        ],
        out_specs=pl.BlockSpec((1, SUBLANES, n1, d), lambda bi, j: (bi, j, 0, 0)),
        out_shape=jax.ShapeDtypeStruct((b, n2, n1, d), jnp.uint32),
        scratch_shapes=[pltpu.VMEM((2, 2 * gd // LANES, rows, LANES), F32)],
        compiler_params=_cparams(("parallel", "parallel")),
        name="fourier_stage1",
    )(x4, gain, cs, tab)


def _fourier_stage2_kernel(y_ref, tab_ref, o_ref, ys_ref, fs_ref, *, cw):
    n2 = y_ref.shape[1]
    rows = n2 * SUBLANES
    d = y_ref.shape[3]
    nblk = cw // LANES
    tab = tab_ref[...]
    for ch in range(d // cw):
        slot = ch % 2
        for c in range(nblk):
            lanes = slice(ch * cw + c * LANES, ch * cw + (c + 1) * LANES)
            ys_ref[slot, c] = y_ref[0, :, :, lanes].reshape(rows, LANES)
        for k in range(SUBLANES):
            u = jnp.concatenate(
                [ys_ref[slot, c, pl.ds(k, n2, stride=SUBLANES), :] for c in range(nblk)], axis=1)
            re, im = _unpack_complex(u)
            zc = jnp.concatenate([re, im], axis=0).astype(BF16)
            f = jnp.dot(tab, zc, preferred_element_type=F32)
            for c in range(nblk):
                fs_ref[slot, c, pl.ds(k, n2, stride=SUBLANES), :] = f[:, c * LANES:(c + 1) * LANES]
        for c in range(nblk):
            lanes = slice(ch * cw + c * LANES, ch * cw + (c + 1) * LANES)
            o_ref[0, :, :, lanes] = fs_ref[slot, c].reshape(n2, SUBLANES, LANES)


def _fourier_stage2(y, tab):
    b, n2, n1, d = y.shape
    cw = MXU_DIM
    rows = n2 * SUBLANES
    return pl.pallas_call(
        functools.partial(_fourier_stage2_kernel, cw=cw),
        grid=(b, n1 // SUBLANES),
        in_specs=[
            pl.BlockSpec((1, n2, SUBLANES, d), lambda bi, k: (bi, 0, k, 0)),
            pl.BlockSpec((n2, 2 * n2), lambda bi, k: (0, 0)),
        ],
        out_specs=pl.BlockSpec((1, n2, SUBLANES, d), lambda bi, k: (bi, 0, k, 0)),
        out_shape=jax.ShapeDtypeStruct((b, n2, n1, d), F32),
        scratch_shapes=[pltpu.VMEM((2, cw // LANES, rows, LANES), jnp.uint32),
                        pltpu.VMEM((2, cw // LANES, rows, LANES), F32)],
        compiler_params=_cparams(("parallel", "parallel")),
        name="fourier_stage2",
    )(y, tab)


def _proj_norm_kernel(a_ref, x_ref, w_ref, g_ref, x1_ref, h_ref):
    x1 = x_ref[...] + jnp.dot(a_ref[...].astype(BF16), w_ref[...], preferred_element_type=F32)
    x1_ref[...] = x1
    h_ref[...] = _rms_rows(x1, g_ref[...]).astype(BF16)


def _proj_norm(a, x2d, w, gain, tm):
    t, d = x2d.shape
    return pl.pallas_call(
        _proj_norm_kernel,
        grid=(t // tm,),
        in_specs=[
            pl.BlockSpec((tm, d), lambda i: (i, 0)),
            pl.BlockSpec((tm, d), lambda i: (i, 0)),
            pl.BlockSpec((None, d, d), lambda i: (0, 0, 0)),
            pl.BlockSpec((1, d), lambda i: (0, 0)),
        ],
        out_specs=[
            pl.BlockSpec((tm, d), lambda i: (i, 0)),
            pl.BlockSpec((tm, d), lambda i: (i, 0)),
        ],
        out_shape=[jax.ShapeDtypeStruct((t, d), F32), jax.ShapeDtypeStruct((t, d), BF16)],
        compiler_params=_cparams(("parallel",)),
        name="proj_norm",
    )(a, x2d, w, gain)


def _mlp_kernel(h_ref, x_ref, wu_ref, wd_ref, o_ref, *, nsplit):
    tf = wu_ref.shape[1]
    ts = tf // nsplit
    first = pl.program_id(1) == 0

    def piece(s):
        a = jnp.maximum(
            jnp.dot(h_ref[...], wu_ref[:, s * ts:(s + 1) * ts], preferred_element_type=F32), 0.0)
        return jnp.dot((a * a).astype(BF16), wd_ref[s * ts:(s + 1) * ts, :], preferred_element_type=F32)

    @pl.when(first)
    def _():
        o_ref[...] = x_ref[...] + piece(0)

    @pl.when(jnp.logical_not(first))
    def _():
        o_ref[...] += piece(0)

    for s in range(1, nsplit):
        o_ref[...] += piece(s)


def _mlp(h, x1, w_up, w_down, layer, tm, tf):
    t, d = x1.shape
    dff = w_up.shape[-1]
    return pl.pallas_call(
        functools.partial(_mlp_kernel, nsplit=max(1, tf // (2 * MXU_DIM))),
        grid=(t // tm, dff // tf),
        in_specs=[
            pl.BlockSpec((tm, d), lambda i, j: (i, 0)),
            pl.BlockSpec((tm, d), lambda i, j: (i, 0)),
            pl.BlockSpec((None, d, tf), lambda i, j: (layer, 0, j)),
            pl.BlockSpec((None, tf, d), lambda i, j: (layer, j, 0)),
        ],
        out_specs=pl.BlockSpec((tm, d), lambda i, j: (i, 0)),
        out_shape=jax.ShapeDtypeStruct((t, d), F32),
        compiler_params=_cparams(("parallel", "arbitrary")),
        name="sqrelu_mlp",
    )(h, x1, w_up, w_down)


def _qkv_kernel(x_ref, g_ref, w_ref, e_ref, qg_ref, kg_ref, cf_ref, s1_ref, s2_ref,
                q_ref, k_ref, v_ref, *, d, scale):
    h = _rms_rows(x_ref[...], g_ref[...]).astype(BF16)
    e = e_ref[...]
    cf, s1, s2 = cf_ref[...], s1_ref[...], s2_ref[...]
    kvd = N_KV * HEAD_DIM
    lo = lax.broadcasted_iota(jnp.int32, (h.shape[0], LANES), 1) < HEAD_DIM

    def proj(c0, width):
        return jnp.dot(h, w_ref[:, c0:c0 + width], preferred_element_type=F32)

    def head_norm_rope(blk, gain, out_scale):
        ss = jnp.dot((blk * blk).astype(BF16), e, preferred_element_type=F32)
        n = blk * lax.rsqrt(ss * (1.0 / HEAD_DIM) + EPS) * gain
        outs = []
        for u in range(MXU_DIM // LANES):
            xb = n[:, u * LANES:(u + 1) * LANES]
            r = (xb * cf + pltpu.roll(xb, LANES - ROT_HALF, axis=1) * s1
                 + pltpu.roll(xb, ROT_HALF, axis=1) * s2)
            outs.append(r * out_scale if out_scale != 1.0 else r)
        return outs

    def store_duplicated(ref, halves):
        for u, xb in enumerate(halves):
            xr = pltpu.roll(xb, HEAD_DIM, axis=1)
            ref[:, (2 * u) * LANES:(2 * u + 1) * LANES] = jnp.where(lo, xb, xr).astype(BF16)
            ref[:, (2 * u + 1) * LANES:(2 * u + 2) * LANES] = jnp.where(lo, xr, xb).astype(BF16)

    pw = 2 * MXU_DIM
    for c in range(d // pw):
        blk = proj(c * pw, pw)
        outs = (head_norm_rope(blk[:, :MXU_DIM], qg_ref[...], scale)
                + head_norm_rope(blk[:, MXU_DIM:], qg_ref[...], scale))
        q_ref[:, c * pw:(c + 1) * pw] = jnp.concatenate(outs, axis=1).astype(BF16)
    assert kvd == MXU_DIM
    kv = proj(d, 2 * kvd)
    store_duplicated(k_ref, head_norm_rope(kv[:, :kvd], kg_ref[...], 1.0))
    store_duplicated(v_ref, [kv[:, kvd + u * LANES:kvd + (u + 1) * LANES] for u in range(kvd // LANES)])


def _qkv(x2d, gain, w, e, qg, kg, cf, s1, s2, seq, tm):
    t, d = x2d.shape
    kd = N_KV * LANES
    nt = seq // tm
    tab_spec = pl.BlockSpec((tm, LANES), lambda i: (i % nt, 0))
    const = lambda shape: pl.BlockSpec(shape, lambda i: (0, 0))
    return pl.pallas_call(
        functools.partial(_qkv_kernel, d=d, scale=1.0 / math.sqrt(HEAD_DIM)),
        grid=(t // tm,),
        in_specs=[
            pl.BlockSpec((tm, d), lambda i: (i, 0)),
            const((1, d)),
            pl.BlockSpec((None,) + w.shape[1:], lambda i: (0, 0, 0)),
            const((MXU_DIM, MXU_DIM)),
            const((1, MXU_DIM)),
            const((1, MXU_DIM)),
            tab_spec, tab_spec, tab_spec,
        ],
        out_specs=[
            pl.BlockSpec((tm, d), lambda i: (i, 0)),
            pl.BlockSpec((tm, kd), lambda i: (i, 0)),
            pl.BlockSpec((tm, kd), lambda i: (i, 0)),
        ],
        out_shape=[
            jax.ShapeDtypeStruct((t, d), BF16),
            jax.ShapeDtypeStruct((t, kd), BF16),
            jax.ShapeDtypeStruct((t, kd), BF16),
        ],
        compiler_params=_cparams(("parallel",)),
        name="qkv_norm_rope",
    )(x2d, gain, w, e, qg, kg, cf, s1, s2)


def _attn_kernel(sink_ref, q_ref, kp_ref, kc_ref, kn_ref, vp_ref, vc_ref, vn_ref, o_ref, *, nb):
    j = pl.program_id(1)
    bq = WINDOW_BLOCK
    npair = GROUP // 2
    lane = lax.broadcasted_iota(jnp.int32, (bq, LANES), 1)
    lo = lane < HEAD_DIM
    row = lax.broadcasted_iota(jnp.int32, (bq, 3 * bq), 0)
    col = lax.broadcasted_iota(jnp.int32, (bq, 3 * bq), 1)
    seg = col // bq
    kk = col - seg * bq
    valid = ((seg == 1)
             | ((seg == 0) & (kk >= row) & (j > 0))
             | ((seg == 2) & (kk <= row) & (j < nb - 1)))
    zero = jnp.zeros((bq, LANES), BF16)
    lo3 = lax.broadcasted_iota(jnp.int32, (3 * bq, LANES), 1) < HEAD_DIM
    zero3 = jnp.zeros((3 * bq, LANES), BF16)

    for h in range(N_KV):
        ks = slice(h * LANES, (h + 1) * LANES)
        k2 = jnp.concatenate([kp_ref[0, :, ks], kc_ref[0, :, ks], kn_ref[0, :, ks]], axis=0)
        v2 = jnp.concatenate([vp_ref[0, :, ks], vc_ref[0, :, ks], vn_ref[0, :, ks]], axis=0)
        qparts = []
        for parity in range(2):
            for p in range(npair):
                c0 = (h * npair + p) * LANES
                blk = q_ref[0, :, c0:c0 + LANES]
                qparts.append(jnp.where(lo if parity == 0 else ~lo, blk, zero))
        qs = jnp.concatenate(qparts, axis=0)
        s = lax.dot_general(qs, k2, (((1,), (1,)), ((), ())), preferred_element_type=F32)
        ps, inv = [], []
        for r in range(GROUP):
            parity, p = divmod(r, npair)
            sink = sink_ref[h * GROUP + 2 * p + parity]
            sr = jnp.where(valid, s[r * bq:(r + 1) * bq], NEG)
            m = jnp.maximum(jnp.max(sr, axis=-1, keepdims=True), sink)
            e = jnp.exp(sr - m)
            denom = jnp.sum(e, axis=-1, keepdims=True) + jnp.exp(sink - m)
            ps.append(e.astype(BF16))
            inv.append(1.0 / denom)
        pe = jnp.concatenate(ps[:npair], axis=0)
        po = jnp.concatenate(ps[npair:], axis=0)
        o = (jnp.dot(pe, jnp.where(lo3, v2, zero3), preferred_element_type=F32)
             + jnp.dot(po, jnp.where(lo3, zero3, v2), preferred_element_type=F32))
        for p in range(npair):
            scale = jnp.where(lo, inv[p], inv[npair + p])
            c0 = (h * npair + p) * LANES
            o_ref[0, :, c0:c0 + LANES] = (o[p * bq:(p + 1) * bq] * scale).astype(BF16)


def _attention(sink, q, k, v):
    b, s, d = q.shape
    kd = k.shape[-1]
    bq = WINDOW_BLOCK
    nb = s // bq
    prev = lambda bi, j: (bi, jnp.maximum(j - 1, 0), 0)
    cur = lambda bi, j: (bi, j, 0)
    nxt = lambda bi, j: (bi, jnp.minimum(j + 1, nb - 1), 0)
    kv = lambda im: pl.BlockSpec((1, bq, kd), im)
    return pl.pallas_call(
        functools.partial(_attn_kernel, nb=nb),
        grid=(b, nb),
        in_specs=[
            pl.BlockSpec(memory_space=pltpu.SMEM),
            pl.BlockSpec((1, bq, d), cur),
            kv(prev), kv(cur), kv(nxt), kv(prev), kv(cur), kv(nxt),
        ],
        out_specs=pl.BlockSpec((1, bq, d), cur),
        out_shape=jax.ShapeDtypeStruct((b, s, d), BF16),
        compiler_params=_cparams(("parallel", "parallel")),
        name="window_attention",
    )(sink, q, k, k, k, v, v, v)


def _chan_table(gd):
    k = np.arange(gd)
    ang = 2.0 * np.pi * ((k[:, None] * k[None, :]) % gd) / gd
    sc = 1.0 / math.sqrt(gd)
    return jnp.asarray(np.concatenate([np.cos(ang) * sc, -np.sin(ang) * sc], axis=1), BF16)


def _stage1_table(n1, n2):
    n = n1 * n2
    k1 = jnp.arange(n1, dtype=jnp.int32)
    pos = n2 * jnp.arange(n1, dtype=jnp.int32)[None, None, :] + jnp.arange(n2, dtype=jnp.int32)[:, None, None]
    ang = ((k1[None, :, None] * pos) % n).astype(F32) * (2.0 * math.pi / n)
    c, s = jnp.cos(ang), jnp.sin(ang)
    top = jnp.concatenate([c, s], axis=-1)
    bot = jnp.concatenate([-s, c], axis=-1)
    return jnp.concatenate([top, bot], axis=-2).astype(BF16)


def _stage2_table(n1, n2):
    k = np.arange(n2)
    ang = 2.0 * np.pi * ((k[:, None] * k[None, :]) % n2) / n2
    sc = 1.0 / math.sqrt(n1 * n2)
    return jnp.asarray(np.concatenate([np.cos(ang) * sc, np.sin(ang) * sc], axis=1), BF16)


def _rope_lane_tables(s):
    inv_freq = ROPE_THETA ** (-jnp.arange(0, ROT_DIM, 2, dtype=F32) / ROT_DIM)
    ang = jnp.arange(s, dtype=F32)[:, None] * inv_freq[None, :]
    cos, sin = jnp.cos(ang), jnp.sin(ang)
    pad = HEAD_DIM - ROT_DIM
    ones = jnp.ones((s, pad), F32)
    zeros = jnp.zeros((s, pad), F32)
    zh = jnp.zeros((s, ROT_HALF), F32)
    cf = jnp.concatenate([cos, cos, ones], axis=1)
    s1 = jnp.concatenate([-sin, zh, zeros], axis=1)
    s2 = jnp.concatenate([zh, sin, zeros], axis=1)
    rep = LANES // HEAD_DIM
    return tuple(jnp.tile(a, (1, rep)) for a in (cf, s1, s2))


def _segment_ones():
    i = np.arange(MXU_DIM) // HEAD_DIM
    return jnp.asarray((i[:, None] == i[None, :]).astype(np.float32), BF16)


def _fourier_mixer(x, gain, w_out, mlp_gain, tm):
    b, s, d = x.shape
    gd = d // FFT_GROUPS
    n1, n2 = _seq_split(s)
    y = _fourier_stage1(x.reshape(b, n1, n2, d), gain, _chan_table(gd), _stage1_table(n1, n2))
    f = _fourier_stage2(y, _stage2_table(n1, n2))
    return _proj_norm(f.reshape(b * s, d), x.reshape(b * s, d), w_out, mlp_gain, tm)


def _attn_mixer(x2d, b, s, gain, w_qkv, q_gain, k_gain, sink, w_o, mlp_gain, tm):
    t, d = x2d.shape
    rep = MXU_DIM // HEAD_DIM
    cf, s1, s2 = _rope_lane_tables(s)
    q, k, v = _qkv(x2d, gain, w_qkv, _segment_ones(),
                   jnp.tile(q_gain, rep)[None, :], jnp.tile(k_gain, rep)[None, :],
                   cf, s1, s2, s, tm)
    kd = k.shape[-1]
    o = _attention(sink, q.reshape(b, s, d), k.reshape(b, s, kd), v.reshape(b, s, kd))
    return _proj_norm(o.reshape(t, d), x2d, w_o, mlp_gain, tm)


def _trunk(x, p, tm, tm_mlp, tf):
    b, s, d = x.shape
    x1, h = _fourier_mixer(x, p["fourier_norm"][0][None, :], p["fourier_w_out"],
                           p["mlp_norm"][0][None, :], tm)
    x2 = _mlp(h, x1, p["mlp_w_up"], p["mlp_w_down"], 0, tm_mlp, tf)
    x3, h = _attn_mixer(x2, b, s, p["attn_norm"][0][None, :], p["attn_w_qkv"], p["attn_q_norm"][0],
                        p["attn_k_norm"][0], p["attn_sink"][0], p["attn_w_o"],
                        p["mlp_norm"][1][None, :], tm)
    x4 = _mlp(h, x3, p["mlp_w_up"], p["mlp_w_down"], 1, tm_mlp, tf)
    return x4.reshape(b, s, d)


def kernel(x_prompt, x_sample, fourier_norm, fourier_w_out, attn_norm, attn_w_qkv, attn_q_norm,
           attn_k_norm, attn_sink, attn_w_o, mlp_norm, mlp_w_up, mlp_w_down):
    p = dict(
        fourier_norm=fourier_norm, fourier_w_out=fourier_w_out.astype(BF16),
        attn_norm=attn_norm, attn_w_qkv=attn_w_qkv.astype(BF16), attn_q_norm=attn_q_norm,
        attn_k_norm=attn_k_norm, attn_sink=attn_sink, attn_w_o=attn_w_o.astype(BF16),
        mlp_norm=mlp_norm, mlp_w_up=mlp_w_up.astype(BF16), mlp_w_down=mlp_w_down.astype(BF16),
    )
    outs = []
    for x in (x_prompt, x_sample):
        t = x.shape[0] * x.shape[1]
        tm = min(512, x.shape[1])
        tm_mlp = min(1024, t)
        tf = min(1024, mlp_w_up.shape[-1])
        outs.append(_trunk(x, p, tm, tm_mlp, tf))
    return tuple(outs)
```

```python
import functools
import math

import numpy as np
import jax
import jax.numpy as jnp
from jax import lax
from jax.experimental import pallas as pl
from jax.experimental.pallas import tpu as pltpu

F32 = jnp.float32
BF16 = jnp.bfloat16

FFT_GROUPS = 8
HEAD_DIM = 64
N_KV = 4
GROUP = 8
WINDOW_BLOCK = 128
ROPE_THETA = 500000.0
ROT_DIM = HEAD_DIM // 4
ROT_HALF = ROT_DIM // 2
EPS = 1e-6
NEG = -1e30
LOG2E = math.log2(math.e)
LANES = 128
MXU_DIM = 256
VMEM_LIMIT = 60 * 1024 * 1024


def _cparams(sem):
    return pltpu.CompilerParams(dimension_semantics=sem, vmem_limit_bytes=VMEM_LIMIT)


def _rms_rows(x, gain):
    ms = jnp.mean(x * x, axis=-1, keepdims=True)
    return x * lax.rsqrt(ms + EPS) * gain


def _seq_split(s):
    n1 = 1 << ((s.bit_length() - 1 + 1) // 2)
    n2 = s // n1
    assert n1 * n2 == s
    return n1, n2


SUBLANES = 8
BF16_ROWS = 2 * SUBLANES


def _pack_complex(re, im):
    rb = lax.bitcast_convert_type(re.astype(BF16).astype(F32), jnp.uint32)
    ib = lax.bitcast_convert_type(im.astype(BF16).astype(F32), jnp.uint32)
    return rb | (ib >> 16)


def _unpack_complex(u):
    re = lax.bitcast_convert_type(u & jnp.uint32(0xFFFF0000), F32)
    im = lax.bitcast_convert_type(u << 16, F32)
    return re, im


def _fourier_stage1_kernel(x_ref, g_ref, cs_ref, tab_ref, o_ref, zs_ref, *, gd):
    n1 = x_ref.shape[1]
    rows = n1 * SUBLANES
    d = x_ref.shape[3]
    nblk = 2 * gd // LANES
    h = _rms_rows(x_ref[0].reshape(rows, d), g_ref[...]).astype(BF16)
    cs = cs_ref[...]
    for g in range(FFT_GROUPS):
        slot = g % 2
        r = jnp.dot(h[:, g * gd:(g + 1) * gd], cs, preferred_element_type=F32)
        for c in range(nblk):
            zs_ref[slot, c] = r[:, c * LANES:(c + 1) * LANES]
        for j in range(SUBLANES):
            parts = [zs_ref[slot, c, pl.ds(j, n1, stride=SUBLANES), :] for c in range(nblk)]
            zr = jnp.concatenate(parts[:nblk // 2], axis=1)
            zi = jnp.concatenate(parts[nblk // 2:], axis=1)
            zc = jnp.concatenate([zr, zi], axis=0).astype(BF16)
            y = jnp.dot(tab_ref[j], zc, preferred_element_type=F32)
            o_ref[0, j, :, g * gd:(g + 1) * gd] = _pack_complex(y[:n1], y[n1:])


def _fourier_stage1(x4, gain, cs, tab):
    b, n1, n2, d = x4.shape
    gd = d // FFT_GROUPS
    rows = n1 * SUBLANES
    return pl.pallas_call(
        functools.partial(_fourier_stage1_kernel, gd=gd),
        grid=(b, n2 // SUBLANES),
        in_specs=[
            pl.BlockSpec((1, n1, SUBLANES, d), lambda bi, j: (bi, 0, j, 0)),
            pl.BlockSpec((1, d), lambda bi, j: (0, 0)),
            pl.BlockSpec((gd, 2 * gd), lambda bi, j: (0, 0)),
            pl.BlockSpec((SUBLANES, 2 * n1, 2 * n1), lambda bi, j: (j, 0, 0)),
        ],
        out_specs=pl.BlockSpec((1, SUBLANES, n1, d), lambda bi, j: (bi, j, 0, 0)),
        out_shape=jax.ShapeDtypeStruct((b, n2, n1, d), jnp.uint32),
        scratch_shapes=[pltpu.VMEM((2, 2 * gd // LANES, rows, LANES), F32)],
        compiler_params=_cparams(("parallel", "parallel")),
        name="fourier_stage1",
    )(x4, gain, cs, tab)


def _fourier_stage2_kernel(y_ref, tab_ref, o_ref, ys_ref, fs_ref, *, cw):
    n2 = y_ref.shape[1]
    rows = n2 * SUBLANES
    d = y_ref.shape[3]
    nblk = cw // LANES
    tab = tab_ref[...]
    for ch in range(d // cw):
        slot = ch % 2
        for c in range(nblk):
            lanes = slice(ch * cw + c * LANES, ch * cw + (c + 1) * LANES)
            ys_ref[slot, c] = y_ref[0, :, :, lanes].reshape(rows, LANES)
        for k in range(SUBLANES):
            u = jnp.concatenate(
                [ys_ref[slot, c, pl.ds(k, n2, stride=SUBLANES), :] for c in range(nblk)], axis=1)
            re, im = _unpack_complex(u)
            zc = jnp.concatenate([re, im], axis=0).astype(BF16)
            f = jnp.dot(tab, zc, preferred_element_type=F32)
            for c in range(nblk):
                fs_ref[slot, c, pl.ds(k, n2, stride=SUBLANES), :] = f[:, c * LANES:(c + 1) * LANES]
        for c in range(nblk):
            lanes = slice(ch * cw + c * LANES, ch * cw + (c + 1) * LANES)
            o_ref[0, :, :, lanes] = fs_ref[slot, c].reshape(n2, SUBLANES, LANES)


def _fourier_stage2(y, tab):
    b, n2, n1, d = y.shape
    cw = MXU_DIM
    rows = n2 * SUBLANES
    return pl.pallas_call(
        functools.partial(_fourier_stage2_kernel, cw=cw),
        grid=(b, n1 // SUBLANES),
        in_specs=[
            pl.BlockSpec((1, n2, SUBLANES, d), lambda bi, k: (bi, 0, k, 0)),
            pl.BlockSpec((n2, 2 * n2), lambda bi, k: (0, 0)),
        ],
        out_specs=pl.BlockSpec((1, n2, SUBLANES, d), lambda bi, k: (bi, 0, k, 0)),
        out_shape=jax.ShapeDtypeStruct((b, n2, n1, d), F32),
        scratch_shapes=[pltpu.VMEM((2, cw // LANES, rows, LANES), jnp.uint32),
                        pltpu.VMEM((2, cw // LANES, rows, LANES), F32)],
        compiler_params=_cparams(("parallel", "parallel")),
        name="fourier_stage2",
    )(y, tab)


def _proj_norm_kernel(a_ref, x_ref, w_ref, g_ref, x1_ref, h_ref):
    x1 = x_ref[...] + jnp.dot(a_ref[...].astype(BF16), w_ref[...], preferred_element_type=F32)
    x1_ref[...] = x1
    h_ref[...] = _rms_rows(x1, g_ref[...]).astype(BF16)


def _proj_norm(a, x2d, w, gain, tm):
    t, d = x2d.shape
    return pl.pallas_call(
        _proj_norm_kernel,
        grid=(t // tm,),
        in_specs=[
            pl.BlockSpec((tm, d), lambda i: (i, 0)),
            pl.BlockSpec((tm, d), lambda i: (i, 0)),
            pl.BlockSpec((None, d, d), lambda i: (0, 0, 0)),
            pl.BlockSpec((1, d), lambda i: (0, 0)),
        ],
        out_specs=[
            pl.BlockSpec((tm, d), lambda i: (i, 0)),
            pl.BlockSpec((tm, d), lambda i: (i, 0)),
        ],
        out_shape=[jax.ShapeDtypeStruct((t, d), F32), jax.ShapeDtypeStruct((t, d), BF16)],
        compiler_params=_cparams(("parallel",)),
        name="proj_norm",
    )(a, x2d, w, gain)


def _mlp_kernel(h_ref, x_ref, wu_ref, wd_ref, o_ref, *, nsplit):
    tf = wu_ref.shape[1]
    ts = tf // nsplit
    first = pl.program_id(1) == 0

    def piece(s):
        a = jnp.maximum(
            jnp.dot(h_ref[...], wu_ref[:, s * ts:(s + 1) * ts], preferred_element_type=F32), 0.0)
        return jnp.dot((a * a).astype(BF16), wd_ref[s * ts:(s + 1) * ts, :], preferred_element_type=F32)

    @pl.when(first)
    def _():
        o_ref[...] = x_ref[...] + piece(0)

    @pl.when(jnp.logical_not(first))
    def _():
        o_ref[...] += piece(0)

    for s in range(1, nsplit):
        o_ref[...] += piece(s)


def _mlp(h, x1, w_up, w_down, layer, tm, tf):
    t, d = x1.shape
    dff = w_up.shape[-1]
    return pl.pallas_call(
        functools.partial(_mlp_kernel, nsplit=max(1, tf // (2 * MXU_DIM))),
        grid=(t // tm, dff // tf),
        in_specs=[
            pl.BlockSpec((tm, d), lambda i, j: (i, 0)),
            pl.BlockSpec((tm, d), lambda i, j: (i, 0)),
            pl.BlockSpec((None, d, tf), lambda i, j: (layer, 0, j)),
            pl.BlockSpec((None, tf, d), lambda i, j: (layer, j, 0)),
        ],
        out_specs=pl.BlockSpec((tm, d), lambda i, j: (i, 0)),
        out_shape=jax.ShapeDtypeStruct((t, d), F32),
        compiler_params=_cparams(("parallel", "arbitrary")),
        name="sqrelu_mlp",
    )(h, x1, w_up, w_down)


def _qkv_kernel(x_ref, g_ref, w_ref, e_ref, qg_ref, kg_ref, cf_ref, s1_ref, s2_ref,
                q_ref, k_ref, v_ref, *, d, scale):
    h = _rms_rows(x_ref[...], g_ref[...]).astype(BF16)
    e = e_ref[...]
    cf, s1, s2 = cf_ref[...], s1_ref[...], s2_ref[...]
    kvd = N_KV * HEAD_DIM
    lo = lax.broadcasted_iota(jnp.int32, (h.shape[0], LANES), 1) < HEAD_DIM

    def proj(c0, width):
        return jnp.dot(h, w_ref[:, c0:c0 + width], preferred_element_type=F32)

    def head_norm_rope(blk, gain, out_scale):
        ss = jnp.dot((blk * blk).astype(BF16), e, preferred_element_type=F32)
        n = blk * lax.rsqrt(ss * (1.0 / HEAD_DIM) + EPS) * gain
        outs = []
        for u in range(MXU_DIM // LANES):
            xb = n[:, u * LANES:(u + 1) * LANES]
            r = (xb * cf + pltpu.roll(xb, LANES - ROT_HALF, axis=1) * s1
                 + pltpu.roll(xb, ROT_HALF, axis=1) * s2)
            outs.append(r * out_scale if out_scale != 1.0 else r)
        return outs

    def store_duplicated(ref, halves):
        for u, xb in enumerate(halves):
            xr = pltpu.roll(xb, HEAD_DIM, axis=1)
            ref[:, (2 * u) * LANES:(2 * u + 1) * LANES] = jnp.where(lo, xb, xr).astype(BF16)
            ref[:, (2 * u + 1) * LANES:(2 * u + 2) * LANES] = jnp.where(lo, xr, xb).astype(BF16)

    pw = 2 * MXU_DIM
    for c in range(d // pw):
        blk = proj(c * pw, pw)
        outs = (head_norm_rope(blk[:, :MXU_DIM], qg_ref[...], scale)
                + head_norm_rope(blk[:, MXU_DIM:], qg_ref[...], scale))
        q_ref[:, c * pw:(c + 1) * pw] = jnp.concatenate(outs, axis=1).astype(BF16)
    assert kvd == MXU_DIM
    kv = proj(d, 2 * kvd)
    store_duplicated(k_ref, head_norm_rope(kv[:, :kvd], kg_ref[...], 1.0))
    store_duplicated(v_ref, [kv[:, kvd + u * LANES:kvd + (u + 1) * LANES] for u in range(kvd // LANES)])


def _qkv(x2d, gain, w, e, qg, kg, cf, s1, s2, seq, tm):
    t, d = x2d.shape
    kd = N_KV * LANES
    nt = seq // tm
    tab_spec = pl.BlockSpec((tm, LANES), lambda i: (i % nt, 0))
    const = lambda shape: pl.BlockSpec(shape, lambda i: (0, 0))
    return pl.pallas_call(
        functools.partial(_qkv_kernel, d=d, scale=LOG2E / math.sqrt(HEAD_DIM)),
        grid=(t // tm,),
        in_specs=[
            pl.BlockSpec((tm, d), lambda i: (i, 0)),
            const((1, d)),
            pl.BlockSpec((None,) + w.shape[1:], lambda i: (0, 0, 0)),
            const((MXU_DIM, MXU_DIM)),
            const((1, MXU_DIM)),
            const((1, MXU_DIM)),
            tab_spec, tab_spec, tab_spec,
        ],
        out_specs=[
            pl.BlockSpec((tm, d), lambda i: (i, 0)),
            pl.BlockSpec((tm, kd), lambda i: (i, 0)),
            pl.BlockSpec((tm, kd), lambda i: (i, 0)),
        ],
        out_shape=[
            jax.ShapeDtypeStruct((t, d), BF16),
            jax.ShapeDtypeStruct((t, kd), BF16),
            jax.ShapeDtypeStruct((t, kd), BF16),
        ],
        compiler_params=_cparams(("parallel",)),
        name="qkv_norm_rope",
    )(x2d, gain, w, e, qg, kg, cf, s1, s2)


def _attn_kernel(sink_ref, q_ref, kp_ref, kc_ref, kn_ref, vp_ref, vc_ref, vn_ref, o_ref, *, nb):
    j = pl.program_id(1)
    bq = WINDOW_BLOCK
    npair = GROUP // 2
    nq = GROUP * bq
    lo = lax.broadcasted_iota(jnp.int32, (bq, LANES), 1) < HEAD_DIM
    key = lax.broadcasted_iota(jnp.int32, (bq, nq), 0)
    qry = lax.broadcasted_iota(jnp.int32, (bq, nq), 1) % bq
    vprev = (key >= qry) & (j > 0)
    vnext = (key <= qry) & (j < nb - 1)
    zero = jnp.zeros((bq, LANES), BF16)
    xr = BF16_ROWS
    lo_x = lax.broadcasted_iota(jnp.int32, (3 * bq + xr, LANES), 1) < HEAD_DIM
    sink_row = lax.broadcasted_iota(jnp.int32, (xr, nq), 0) == 0
    vzero = jnp.zeros((xr, LANES), BF16)
    nt = (((1,), (1,)), ((), ()))
    tn = (((0,), (0,)), ((), ()))

    for h in range(N_KV):
        ks = slice(h * LANES, (h + 1) * LANES)
        k2 = jnp.concatenate([kp_ref[0, :, ks], kc_ref[0, :, ks], kn_ref[0, :, ks]], axis=0)
        v2x = jnp.concatenate([vp_ref[0, :, ks], vc_ref[0, :, ks], vn_ref[0, :, ks], vzero], axis=0)
        qparts, sinks = [], []
        for parity in range(2):
            for p in range(npair):
                c0 = (h * npair + p) * LANES
                blk = q_ref[0, :, c0:c0 + LANES]
                qparts.append(jnp.where(lo if parity == 0 else ~lo, blk, zero))
                sinks.append(jnp.full((1, bq), sink_ref[h * GROUP + 2 * p + parity] * LOG2E, F32))
        qs = jnp.concatenate(qparts, axis=0)
        sink = jnp.concatenate(sinks, axis=1)
        st = lax.dot_general(k2, qs, nt, preferred_element_type=F32)
        sp = jnp.where(vprev, st[:bq], NEG)
        sc = st[bq:2 * bq]
        sn = jnp.where(vnext, st[2 * bq:], NEG)
        m = jnp.max(jnp.maximum(jnp.maximum(sp, sc), sn), axis=0, keepdims=True)
        m = jnp.maximum(m, sink)
        es = jnp.where(sink_row, jnp.exp2(sink - m), 0.0)
        pt = jnp.concatenate(
            [jnp.exp2(sp - m), jnp.exp2(sc - m), jnp.exp2(sn - m), es], axis=0).astype(BF16)
        ot = lax.dot_general(jnp.where(lo_x, v2x, 1.0), pt, tn, preferred_element_type=F32)
        res = ot[:HEAD_DIM] * (1.0 / ot[HEAD_DIM:HEAD_DIM + 1])
        half = nq // 2
        for p in range(npair):
            pair = jnp.concatenate(
                [res[:, p * bq:(p + 1) * bq], res[:, half + p * bq:half + (p + 1) * bq]], axis=0)
            c0 = (h * npair + p) * LANES
            o_ref[0, :, c0:c0 + LANES] = pair.T.astype(BF16)


def _attention(sink, q, k, v):
    b, s, d = q.shape
    kd = k.shape[-1]
    bq = WINDOW_BLOCK
    nb = s // bq
    prev = lambda bi, j: (bi, jnp.maximum(j - 1, 0), 0)
    cur = lambda bi, j: (bi, j, 0)
    nxt = lambda bi, j: (bi, jnp.minimum(j + 1, nb - 1), 0)
    kv = lambda im: pl.BlockSpec((1, bq, kd), im)
    return pl.pallas_call(
        functools.partial(_attn_kernel, nb=nb),
        grid=(b, nb),
        in_specs=[
            pl.BlockSpec(memory_space=pltpu.SMEM),
            pl.BlockSpec((1, bq, d), cur),
            kv(prev), kv(cur), kv(nxt), kv(prev), kv(cur), kv(nxt),
        ],
        out_specs=pl.BlockSpec((1, bq, d), cur),
        out_shape=jax.ShapeDtypeStruct((b, s, d), BF16),
        compiler_params=_cparams(("parallel", "parallel")),
        name="window_attention",
    )(sink, q, k, k, k, v, v, v)


def _chan_table(gd):
    k = np.arange(gd)
    ang = 2.0 * np.pi * ((k[:, None] * k[None, :]) % gd) / gd
    sc = 1.0 / math.sqrt(gd)
    return jnp.asarray(np.concatenate([np.cos(ang) * sc, -np.sin(ang) * sc], axis=1), BF16)


def _stage1_table(n1, n2):
    n = n1 * n2
    k1 = jnp.arange(n1, dtype=jnp.int32)
    pos = n2 * jnp.arange(n1, dtype=jnp.int32)[None, None, :] + jnp.arange(n2, dtype=jnp.int32)[:, None, None]
    ang = ((k1[None, :, None] * pos) % n).astype(F32) * (2.0 * math.pi / n)
    c, s = jnp.cos(ang), jnp.sin(ang)
    top = jnp.concatenate([c, s], axis=-1)
    bot = jnp.concatenate([-s, c], axis=-1)
    return jnp.concatenate([top, bot], axis=-2).astype(BF16)


def _stage2_table(n1, n2):
    k = np.arange(n2)
    ang = 2.0 * np.pi * ((k[:, None] * k[None, :]) % n2) / n2
    sc = 1.0 / math.sqrt(n1 * n2)
    return jnp.asarray(np.concatenate([np.cos(ang) * sc, np.sin(ang) * sc], axis=1), BF16)


def _rope_lane_tables(s):
    inv_freq = ROPE_THETA ** (-jnp.arange(0, ROT_DIM, 2, dtype=F32) / ROT_DIM)
    ang = jnp.arange(s, dtype=F32)[:, None] * inv_freq[None, :]
    cos, sin = jnp.cos(ang), jnp.sin(ang)
    pad = HEAD_DIM - ROT_DIM
    ones = jnp.ones((s, pad), F32)
    zeros = jnp.zeros((s, pad), F32)
    zh = jnp.zeros((s, ROT_HALF), F32)
    cf = jnp.concatenate([cos, cos, ones], axis=1)
    s1 = jnp.concatenate([-sin, zh, zeros], axis=1)
    s2 = jnp.concatenate([zh, sin, zeros], axis=1)
    rep = LANES // HEAD_DIM
    return tuple(jnp.tile(a, (1, rep)) for a in (cf, s1, s2))


def _segment_ones():
    i = np.arange(MXU_DIM) // HEAD_DIM
    return jnp.asarray((i[:, None] == i[None, :]).astype(np.float32), BF16)


def _fourier_mixer(x, gain, w_out, mlp_gain, tm):
    b, s, d = x.shape
    gd = d // FFT_GROUPS
    n1, n2 = _seq_split(s)
    y = _fourier_stage1(x.reshape(b, n1, n2, d), gain, _chan_table(gd), _stage1_table(n1, n2))
    f = _fourier_stage2(y, _stage2_table(n1, n2))
    return _proj_norm(f.reshape(b * s, d), x.reshape(b * s, d), w_out, mlp_gain, tm)


def _attn_mixer(x2d, b, s, gain, w_qkv, q_gain, k_gain, sink, w_o, mlp_gain, tm):
    t, d = x2d.shape
    rep = MXU_DIM // HEAD_DIM
    cf, s1, s2 = _rope_lane_tables(s)
    q, k, v = _qkv(x2d, gain, w_qkv, _segment_ones(),
                   jnp.tile(q_gain, rep)[None, :], jnp.tile(k_gain, rep)[None, :],
                   cf, s1, s2, s, tm)
    kd = k.shape[-1]
    o = _attention(sink, q.reshape(b, s, d), k.reshape(b, s, kd), v.reshape(b, s, kd))
    return _proj_norm(o.reshape(t, d), x2d, w_o, mlp_gain, tm)


def _trunk(x, p, tm, tm_mlp, tf):
    b, s, d = x.shape
    x1, h = _fourier_mixer(x, p["fourier_norm"][0][None, :], p["fourier_w_out"],
                           p["mlp_norm"][0][None, :], tm)
    x2 = _mlp(h, x1, p["mlp_w_up"], p["mlp_w_down"], 0, tm_mlp, tf)
    x3, h = _attn_mixer(x2, b, s, p["attn_norm"][0][None, :], p["attn_w_qkv"], p["attn_q_norm"][0],
                        p["attn_k_norm"][0], p["attn_sink"][0], p["attn_w_o"],
                        p["mlp_norm"][1][None, :], tm)
    x4 = _mlp(h, x3, p["mlp_w_up"], p["mlp_w_down"], 1, tm_mlp, tf)
    return x4.reshape(b, s, d)


def kernel(x_prompt, x_sample, fourier_norm, fourier_w_out, attn_norm, attn_w_qkv, attn_q_norm,
           attn_k_norm, attn_sink, attn_w_o, mlp_norm, mlp_w_up, mlp_w_down):
    p = dict(
        fourier_norm=fourier_norm, fourier_w_out=fourier_w_out.astype(BF16),
        attn_norm=attn_norm, attn_w_qkv=attn_w_qkv.astype(BF16), attn_q_norm=attn_q_norm,
        attn_k_norm=attn_k_norm, attn_sink=attn_sink, attn_w_o=attn_w_o.astype(BF16),
        mlp_norm=mlp_norm, mlp_w_up=mlp_w_up.astype(BF16), mlp_w_down=mlp_w_down.astype(BF16),
    )
    outs = []
    for x in (x_prompt, x_sample):
        t = x.shape[0] * x.shape[1]
        tm = min(512, x.shape[1])
        tm_mlp = min(1024, t)
        tf = min(1024, mlp_w_up.shape[-1])
        outs.append(_trunk(x, p, tm, tm_mlp, tf))
    return tuple(outs)
```

```python
import functools
import math

import numpy as np
import jax
import jax.numpy as jnp
from jax import lax
from jax.experimental import pallas as pl
from jax.experimental.pallas import tpu as pltpu

F32 = jnp.float32
BF16 = jnp.bfloat16

FFT_GROUPS = 8
HEAD_DIM = 64
N_KV = 4
GROUP = 8
WINDOW_BLOCK = 128
ROPE_THETA = 500000.0
ROT_DIM = HEAD_DIM // 4
ROT_HALF = ROT_DIM // 2
EPS = 1e-6
NEG = -1e30
LOG2E = math.log2(math.e)
LANES = 128
MXU_DIM = 256
VMEM_LIMIT = 60 * 1024 * 1024


def _cparams(sem):
    return pltpu.CompilerParams(dimension_semantics=sem, vmem_limit_bytes=VMEM_LIMIT)


def _rms_rows(x, gain):
    ms = jnp.mean(x * x, axis=-1, keepdims=True)
    return x * lax.rsqrt(ms + EPS) * gain


def _seq_split(s):
    n1 = 1 << ((s.bit_length() - 1 + 1) // 2)
    n2 = s // n1
    assert n1 * n2 == s
    return n1, n2


SUBLANES = 8
BF16_ROWS = 2 * SUBLANES


def _pack_complex(re, im):
    rb = lax.bitcast_convert_type(re.astype(BF16).astype(F32), jnp.uint32)
    ib = lax.bitcast_convert_type(im.astype(BF16).astype(F32), jnp.uint32)
    return rb | (ib >> 16)


def _unpack_complex(u):
    re = lax.bitcast_convert_type(u & jnp.uint32(0xFFFF0000), F32)
    im = lax.bitcast_convert_type(u << 16, F32)
    return re, im


def _fourier_stage1_kernel(x_ref, g_ref, cs_ref, tab_ref, o_ref, zs_ref, *, gd):
    n1 = x_ref.shape[1]
    rows = n1 * SUBLANES
    d = x_ref.shape[3]
    nblk = 2 * gd // LANES
    h = _rms_rows(x_ref[0].reshape(rows, d), g_ref[...]).astype(BF16)
    cs = cs_ref[...]
    for g in range(FFT_GROUPS):
        slot = g % 2
        r = jnp.dot(h[:, g * gd:(g + 1) * gd], cs, preferred_element_type=F32)
        for c in range(nblk):
            zs_ref[slot, c] = r[:, c * LANES:(c + 1) * LANES]
        for j in range(SUBLANES):
            parts = [zs_ref[slot, c, pl.ds(j, n1, stride=SUBLANES), :] for c in range(nblk)]
            zr = jnp.concatenate(parts[:nblk // 2], axis=1)
            zi = jnp.concatenate(parts[nblk // 2:], axis=1)
            zc = jnp.concatenate([zr, zi], axis=0).astype(BF16)
            y = jnp.dot(tab_ref[j], zc, preferred_element_type=F32)
            o_ref[0, j, :, g * gd:(g + 1) * gd] = _pack_complex(y[:n1], y[n1:])


def _fourier_stage1(x4, gain, cs, tab):
    b, n1, n2, d = x4.shape
    gd = d // FFT_GROUPS
    rows = n1 * SUBLANES
    return pl.pallas_call(
        functools.partial(_fourier_stage1_kernel, gd=gd),
        grid=(b, n2 // SUBLANES),
        in_specs=[
            pl.BlockSpec((1, n1, SUBLANES, d), lambda bi, j: (bi, 0, j, 0)),
            pl.BlockSpec((1, d), lambda bi, j: (0, 0)),
            pl.BlockSpec((gd, 2 * gd), lambda bi, j: (0, 0)),
            pl.BlockSpec((SUBLANES, 2 * n1, 2 * n1), lambda bi, j: (j, 0, 0)),
        ],
        out_specs=pl.BlockSpec((1, SUBLANES, n1, d), lambda bi, j: (bi, j, 0, 0)),
        out_shape=jax.ShapeDtypeStruct((b, n2, n1, d), jnp.uint32),
        scratch_shapes=[pltpu.VMEM((2, 2 * gd // LANES, rows, LANES), F32)],
        compiler_params=_cparams(("parallel", "parallel")),
        name="fourier_stage1",
    )(x4, gain, cs, tab)


def _fourier_stage2_kernel(y_ref, tab_ref, o_ref, ys_ref, fs_ref, *, cw):
    n2 = y_ref.shape[1]
    rows = n2 * SUBLANES
    d = y_ref.shape[3]
    nblk = cw // LANES
    tab = tab_ref[...]
    for ch in range(d // cw):
        slot = ch % 2
        for c in range(nblk):
            lanes = slice(ch * cw + c * LANES, ch * cw + (c + 1) * LANES)
            ys_ref[slot, c] = y_ref[0, :, :, lanes].reshape(rows, LANES)
        for k in range(SUBLANES):
            u = jnp.concatenate(
                [ys_ref[slot, c, pl.ds(k, n2, stride=SUBLANES), :] for c in range(nblk)], axis=1)
            re, im = _unpack_complex(u)
            zc = jnp.concatenate([re, im], axis=0).astype(BF16)
            f = jnp.dot(tab, zc, preferred_element_type=F32)
            for c in range(nblk):
                fs_ref[slot, c, pl.ds(k, n2, stride=SUBLANES), :] = f[:, c * LANES:(c + 1) * LANES]
        for c in range(nblk):
            lanes = slice(ch * cw + c * LANES, ch * cw + (c + 1) * LANES)
            o_ref[0, :, :, lanes] = fs_ref[slot, c].reshape(n2, SUBLANES, LANES)


def _fourier_stage2(y, tab):
    b, n2, n1, d = y.shape
    cw = MXU_DIM
    rows = n2 * SUBLANES
    return pl.pallas_call(
        functools.partial(_fourier_stage2_kernel, cw=cw),
        grid=(b, n1 // SUBLANES),
        in_specs=[
            pl.BlockSpec((1, n2, SUBLANES, d), lambda bi, k: (bi, 0, k, 0)),
            pl.BlockSpec((n2, 2 * n2), lambda bi, k: (0, 0)),
        ],
        out_specs=pl.BlockSpec((1, n2, SUBLANES, d), lambda bi, k: (bi, 0, k, 0)),
        out_shape=jax.ShapeDtypeStruct((b, n2, n1, d), F32),
        scratch_shapes=[pltpu.VMEM((2, cw // LANES, rows, LANES), jnp.uint32),
                        pltpu.VMEM((2, cw // LANES, rows, LANES), F32)],
        compiler_params=_cparams(("parallel", "parallel")),
        name="fourier_stage2",
    )(y, tab)


def _proj_norm_kernel(a_ref, x_ref, w_ref, g_ref, x1_ref, h_ref):
    x1 = x_ref[...] + jnp.dot(a_ref[...].astype(BF16), w_ref[...], preferred_element_type=F32)
    x1_ref[...] = x1
    h_ref[...] = _rms_rows(x1, g_ref[...]).astype(BF16)


def _proj_norm(a, x2d, w, gain, tm):
    t, d = x2d.shape
    return pl.pallas_call(
        _proj_norm_kernel,
        grid=(t // tm,),
        in_specs=[
            pl.BlockSpec((tm, d), lambda i: (i, 0)),
            pl.BlockSpec((tm, d), lambda i: (i, 0)),
            pl.BlockSpec((None, d, d), lambda i: (0, 0, 0)),
            pl.BlockSpec((1, d), lambda i: (0, 0)),
        ],
        out_specs=[
            pl.BlockSpec((tm, d), lambda i: (i, 0)),
            pl.BlockSpec((tm, d), lambda i: (i, 0)),
        ],
        out_shape=[jax.ShapeDtypeStruct((t, d), F32), jax.ShapeDtypeStruct((t, d), BF16)],
        compiler_params=_cparams(("parallel",)),
        name="proj_norm",
    )(a, x2d, w, gain)


def _mlp_kernel(h_ref, x_ref, wu_hbm, wd_hbm, o_ref, wu_buf, wd_buf, sem, *, layer, tf, ts):
    i = pl.program_id(0)
    nchunk = wu_hbm.shape[-1] // tf
    assert nchunk % 2 == 0

    def chunk_copies(j, slot):
        return (
            pltpu.make_async_copy(wu_hbm.at[layer, :, pl.ds(j * tf, tf)], wu_buf.at[slot], sem.at[0, slot]),
            pltpu.make_async_copy(wd_hbm.at[layer, pl.ds(j * tf, tf), :], wd_buf.at[slot], sem.at[1, slot]),
        )

    def start(j, slot):
        for c in chunk_copies(j, slot):
            c.start()

    @pl.when(i == 0)
    def _():
        start(0, 0)

    for j in range(nchunk):
        slot = j % 2
        for c in chunk_copies(j, slot):
            c.wait()
        if j + 1 < nchunk:
            start(j + 1, 1 - slot)
        else:
            @pl.when(i + 1 < pl.num_programs(0))
            def _():
                start(0, 1 - slot)
        for s in range(tf // ts):
            a = jnp.maximum(
                jnp.dot(h_ref[...], wu_buf[slot, :, s * ts:(s + 1) * ts], preferred_element_type=F32), 0.0)
            part = jnp.dot((a * a).astype(BF16), wd_buf[slot, s * ts:(s + 1) * ts, :],
                           preferred_element_type=F32)
            if j == 0 and s == 0:
                o_ref[...] = x_ref[...] + part
            else:
                o_ref[...] += part


def _mlp(h, x1, w_up, w_down, layer, tm, tf):
    t, d = x1.shape
    ts = min(tf, 2 * MXU_DIM)
    return pl.pallas_call(
        functools.partial(_mlp_kernel, layer=layer, tf=tf, ts=ts),
        grid=(t // tm,),
        in_specs=[
            pl.BlockSpec((tm, d), lambda i: (i, 0)),
            pl.BlockSpec((tm, d), lambda i: (i, 0)),
            pl.BlockSpec(memory_space=pl.ANY),
            pl.BlockSpec(memory_space=pl.ANY),
        ],
        out_specs=pl.BlockSpec((tm, d), lambda i: (i, 0)),
        out_shape=jax.ShapeDtypeStruct((t, d), F32),
        scratch_shapes=[
            pltpu.VMEM((2, d, tf), BF16),
            pltpu.VMEM((2, tf, d), BF16),
            pltpu.SemaphoreType.DMA((2, 2)),
        ],
        compiler_params=_cparams(("arbitrary",)),
        name="sqrelu_mlp",
    )(h, x1, w_up, w_down)


def _qkv_kernel(x_ref, g_ref, w_ref, e_ref, qg_ref, kg_ref, cf_ref, s1_ref, s2_ref,
                q_ref, k_ref, v_ref, *, d, scale):
    h = _rms_rows(x_ref[...], g_ref[...]).astype(BF16)
    e = e_ref[...]
    cf, s1, s2 = cf_ref[...], s1_ref[...], s2_ref[...]
    kvd = N_KV * HEAD_DIM
    lo = lax.broadcasted_iota(jnp.int32, (h.shape[0], LANES), 1) < HEAD_DIM

    def proj(c0, width):
        return jnp.dot(h, w_ref[:, c0:c0 + width], preferred_element_type=F32)

    def head_norm_rope(blk, gain, out_scale):
        ss = jnp.dot((blk * blk).astype(BF16), e, preferred_element_type=F32)
        n = blk * lax.rsqrt(ss * (1.0 / HEAD_DIM) + EPS) * gain
        outs = []
        for u in range(MXU_DIM // LANES):
            xb = n[:, u * LANES:(u + 1) * LANES]
            r = (xb * cf + pltpu.roll(xb, LANES - ROT_HALF, axis=1) * s1
                 + pltpu.roll(xb, ROT_HALF, axis=1) * s2)
            outs.append(r * out_scale if out_scale != 1.0 else r)
        return outs

    def store_duplicated(ref, halves):
        for u, xb in enumerate(halves):
            xr = pltpu.roll(xb, HEAD_DIM, axis=1)
            ref[:, (2 * u) * LANES:(2 * u + 1) * LANES] = jnp.where(lo, xb, xr).astype(BF16)
            ref[:, (2 * u + 1) * LANES:(2 * u + 2) * LANES] = jnp.where(lo, xr, xb).astype(BF16)

    pw = 2 * MXU_DIM
    for c in range(d // pw):
        blk = proj(c * pw, pw)
        outs = (head_norm_rope(blk[:, :MXU_DIM], qg_ref[...], scale)
                + head_norm_rope(blk[:, MXU_DIM:], qg_ref[...], scale))
        q_ref[:, c * pw:(c + 1) * pw] = jnp.concatenate(outs, axis=1).astype(BF16)
    assert kvd == MXU_DIM
    kv = proj(d, 2 * kvd)
    store_duplicated(k_ref, head_norm_rope(kv[:, :kvd], kg_ref[...], 1.0))
    store_duplicated(v_ref, [kv[:, kvd + u * LANES:kvd + (u + 1) * LANES] for u in range(kvd // LANES)])


def _qkv(x2d, gain, w, e, qg, kg, cf, s1, s2, seq, tm):
    t, d = x2d.shape
    kd = N_KV * LANES
    nt = seq // tm
    tab_spec = pl.BlockSpec((tm, LANES), lambda i: (i % nt, 0))
    const = lambda shape: pl.BlockSpec(shape, lambda i: (0, 0))
    return pl.pallas_call(
        functools.partial(_qkv_kernel, d=d, scale=LOG2E / math.sqrt(HEAD_DIM)),
        grid=(t // tm,),
        in_specs=[
            pl.BlockSpec((tm, d), lambda i: (i, 0)),
            const((1, d)),
            pl.BlockSpec((None,) + w.shape[1:], lambda i: (0, 0, 0)),
            const((MXU_DIM, MXU_DIM)),
            const((1, MXU_DIM)),
            const((1, MXU_DIM)),
            tab_spec, tab_spec, tab_spec,
        ],
        out_specs=[
            pl.BlockSpec((tm, d), lambda i: (i, 0)),
            pl.BlockSpec((tm, kd), lambda i: (i, 0)),
            pl.BlockSpec((tm, kd), lambda i: (i, 0)),
        ],
        out_shape=[
            jax.ShapeDtypeStruct((t, d), BF16),
            jax.ShapeDtypeStruct((t, kd), BF16),
            jax.ShapeDtypeStruct((t, kd), BF16),
        ],
        compiler_params=_cparams(("parallel",)),
        name="qkv_norm_rope",
    )(x2d, gain, w, e, qg, kg, cf, s1, s2)


def _attn_kernel(sink_ref, q_ref, kp_ref, kc_ref, kn_ref, vp_ref, vc_ref, vn_ref, o_ref, *, nb):
    j = pl.program_id(1)
    bq = WINDOW_BLOCK
    npair = GROUP // 2
    nq = GROUP * bq
    lo = lax.broadcasted_iota(jnp.int32, (bq, LANES), 1) < HEAD_DIM
    key = lax.broadcasted_iota(jnp.int32, (bq, nq), 0)
    qry = lax.broadcasted_iota(jnp.int32, (bq, nq), 1) % bq
    vprev = (key >= qry) & (j > 0)
    vnext = (key <= qry) & (j < nb - 1)
    zero = jnp.zeros((bq, LANES), BF16)
    xr = BF16_ROWS
    lo_x = lax.broadcasted_iota(jnp.int32, (3 * bq + xr, LANES), 1) < HEAD_DIM
    sink_row = lax.broadcasted_iota(jnp.int32, (xr, nq), 0) == 0
    vzero = jnp.zeros((xr, LANES), BF16)
    nt = (((1,), (1,)), ((), ()))
    tn = (((0,), (0,)), ((), ()))

    for h in range(N_KV):
        ks = slice(h * LANES, (h + 1) * LANES)
        k2 = jnp.concatenate([kp_ref[0, :, ks], kc_ref[0, :, ks], kn_ref[0, :, ks]], axis=0)
        v2x = jnp.concatenate([vp_ref[0, :, ks], vc_ref[0, :, ks], vn_ref[0, :, ks], vzero], axis=0)
        qparts, sinks = [], []
        for parity in range(2):
            for p in range(npair):
                c0 = (h * npair + p) * LANES
                blk = q_ref[0, :, c0:c0 + LANES]
                qparts.append(jnp.where(lo if parity == 0 else ~lo, blk, zero))
                sinks.append(jnp.full((1, bq), sink_ref[h * GROUP + 2 * p + parity] * LOG2E, F32))
        qs = jnp.concatenate(qparts, axis=0)
        sink = jnp.concatenate(sinks, axis=1)
        st = lax.dot_general(k2, qs, nt, preferred_element_type=F32)
        sp = jnp.where(vprev, st[:bq], NEG)
        sc = st[bq:2 * bq]
        sn = jnp.where(vnext, st[2 * bq:], NEG)
        m = jnp.max(jnp.maximum(jnp.maximum(sp, sc), sn), axis=0, keepdims=True)
        m = jnp.maximum(m, sink)
        es = jnp.where(sink_row, jnp.exp2(sink - m), 0.0)
        pt = jnp.concatenate(
            [jnp.exp2(sp - m), jnp.exp2(sc - m), jnp.exp2(sn - m), es], axis=0).astype(BF16)
        ot = lax.dot_general(jnp.where(lo_x, v2x, 1.0), pt, tn, preferred_element_type=F32)
        res = ot[:HEAD_DIM] * (1.0 / ot[HEAD_DIM:HEAD_DIM + 1])
        half = nq // 2
        for p in range(npair):
            pair = jnp.concatenate(
                [res[:, p * bq:(p + 1) * bq], res[:, half + p * bq:half + (p + 1) * bq]], axis=0)
            c0 = (h * npair + p) * LANES
            o_ref[0, :, c0:c0 + LANES] = pair.T.astype(BF16)


def _attention(sink, q, k, v):
    b, s, d = q.shape
    kd = k.shape[-1]
    bq = WINDOW_BLOCK
    nb = s // bq
    prev = lambda bi, j: (bi, jnp.maximum(j - 1, 0), 0)
    cur = lambda bi, j: (bi, j, 0)
    nxt = lambda bi, j: (bi, jnp.minimum(j + 1, nb - 1), 0)
    kv = lambda im: pl.BlockSpec((1, bq, kd), im)
    return pl.pallas_call(
        functools.partial(_attn_kernel, nb=nb),
        grid=(b, nb),
        in_specs=[
            pl.BlockSpec(memory_space=pltpu.SMEM),
            pl.BlockSpec((1, bq, d), cur),
            kv(prev), kv(cur), kv(nxt), kv(prev), kv(cur), kv(nxt),
        ],
        out_specs=pl.BlockSpec((1, bq, d), cur),
        out_shape=jax.ShapeDtypeStruct((b, s, d), BF16),
        compiler_params=_cparams(("parallel", "parallel")),
        name="window_attention",
    )(sink, q, k, k, k, v, v, v)


def _chan_table(gd):
    k = np.arange(gd)
    ang = 2.0 * np.pi * ((k[:, None] * k[None, :]) % gd) / gd
    sc = 1.0 / math.sqrt(gd)
    return jnp.asarray(np.concatenate([np.cos(ang) * sc, -np.sin(ang) * sc], axis=1), BF16)


def _stage1_table(n1, n2):
    n = n1 * n2
    k1 = jnp.arange(n1, dtype=jnp.int32)
    pos = n2 * jnp.arange(n1, dtype=jnp.int32)[None, None, :] + jnp.arange(n2, dtype=jnp.int32)[:, None, None]
    ang = ((k1[None, :, None] * pos) % n).astype(F32) * (2.0 * math.pi / n)
    c, s = jnp.cos(ang), jnp.sin(ang)
    top = jnp.concatenate([c, s], axis=-1)
    bot = jnp.concatenate([-s, c], axis=-1)
    return jnp.concatenate([top, bot], axis=-2).astype(BF16)


def _stage2_table(n1, n2):
    k = np.arange(n2)
    ang = 2.0 * np.pi * ((k[:, None] * k[None, :]) % n2) / n2
    sc = 1.0 / math.sqrt(n1 * n2)
    return jnp.asarray(np.concatenate([np.cos(ang) * sc, np.sin(ang) * sc], axis=1), BF16)


def _rope_lane_tables(s):
    inv_freq = ROPE_THETA ** (-jnp.arange(0, ROT_DIM, 2, dtype=F32) / ROT_DIM)
    ang = jnp.arange(s, dtype=F32)[:, None] * inv_freq[None, :]
    cos, sin = jnp.cos(ang), jnp.sin(ang)
    pad = HEAD_DIM - ROT_DIM
    ones = jnp.ones((s, pad), F32)
    zeros = jnp.zeros((s, pad), F32)
    zh = jnp.zeros((s, ROT_HALF), F32)
    cf = jnp.concatenate([cos, cos, ones], axis=1)
    s1 = jnp.concatenate([-sin, zh, zeros], axis=1)
    s2 = jnp.concatenate([zh, sin, zeros], axis=1)
    rep = LANES // HEAD_DIM
    return tuple(jnp.tile(a, (1, rep)) for a in (cf, s1, s2))


def _segment_ones():
    i = np.arange(MXU_DIM) // HEAD_DIM
    return jnp.asarray((i[:, None] == i[None, :]).astype(np.float32), BF16)


def _fourier_mixer(x, gain, w_out, mlp_gain, tm):
    b, s, d = x.shape
    gd = d // FFT_GROUPS
    n1, n2 = _seq_split(s)
    y = _fourier_stage1(x.reshape(b, n1, n2, d), gain, _chan_table(gd), _stage1_table(n1, n2))
    f = _fourier_stage2(y, _stage2_table(n1, n2))
    return _proj_norm(f.reshape(b * s, d), x.reshape(b * s, d), w_out, mlp_gain, tm)


def _attn_mixer(x2d, b, s, gain, w_qkv, q_gain, k_gain, sink, w_o, mlp_gain, tm):
    t, d = x2d.shape
    rep = MXU_DIM // HEAD_DIM
    cf, s1, s2 = _rope_lane_tables(s)
    q, k, v = _qkv(x2d, gain, w_qkv, _segment_ones(),
                   jnp.tile(q_gain, rep)[None, :], jnp.tile(k_gain, rep)[None, :],
                   cf, s1, s2, s, tm)
    kd = k.shape[-1]
    o = _attention(sink, q.reshape(b, s, d), k.reshape(b, s, kd), v.reshape(b, s, kd))
    return _proj_norm(o.reshape(t, d), x2d, w_o, mlp_gain, tm)


def _trunk(x, p, tm, tm_mlp, tf):
    b, s, d = x.shape
    x1, h = _fourier_mixer(x, p["fourier_norm"][0][None, :], p["fourier_w_out"],
                           p["mlp_norm"][0][None, :], tm)
    x2 = _mlp(h, x1, p["mlp_w_up"], p["mlp_w_down"], 0, tm_mlp, tf)
    x3, h = _attn_mixer(x2, b, s, p["attn_norm"][0][None, :], p["attn_w_qkv"], p["attn_q_norm"][0],
                        p["attn_k_norm"][0], p["attn_sink"][0], p["attn_w_o"],
                        p["mlp_norm"][1][None, :], tm)
    x4 = _mlp(h, x3, p["mlp_w_up"], p["mlp_w_down"], 1, tm_mlp, tf)
    return x4.reshape(b, s, d)


def kernel(x_prompt, x_sample, fourier_norm, fourier_w_out, attn_norm, attn_w_qkv, attn_q_norm,
           attn_k_norm, attn_sink, attn_w_o, mlp_norm, mlp_w_up, mlp_w_down):
    p = dict(
        fourier_norm=fourier_norm, fourier_w_out=fourier_w_out.astype(BF16),
        attn_norm=attn_norm, attn_w_qkv=attn_w_qkv.astype(BF16), attn_q_norm=attn_q_norm,
        attn_k_norm=attn_k_norm, attn_sink=attn_sink, attn_w_o=attn_w_o.astype(BF16),
        mlp_norm=mlp_norm, mlp_w_up=mlp_w_up.astype(BF16), mlp_w_down=mlp_w_down.astype(BF16),
    )
    outs = []
    for x in (x_prompt, x_sample):
        t = x.shape[0] * x.shape[1]
        tm = min(512, x.shape[1])
        tm_mlp = min(1024, t)
        tf = min(1024, mlp_w_up.shape[-1] // 2)
        outs.append(_trunk(x, p, tm, tm_mlp, tf))
    return tuple(outs)
```

```python
import functools
import math

import numpy as np
import jax
import jax.numpy as jnp
from jax import lax
from jax.experimental import pallas as pl
from jax.experimental.pallas import tpu as pltpu

F32 = jnp.float32
BF16 = jnp.bfloat16

FFT_GROUPS = 8
HEAD_DIM = 64
N_KV = 4
GROUP = 8
WINDOW_BLOCK = 128
ROPE_THETA = 500000.0
ROT_DIM = HEAD_DIM // 4
ROT_HALF = ROT_DIM // 2
EPS = 1e-6
NEG = -1e30
LOG2E = math.log2(math.e)
LANES = 128
MXU_DIM = 256
VMEM_LIMIT = 60 * 1024 * 1024


def _cparams(sem):
    return pltpu.CompilerParams(dimension_semantics=sem, vmem_limit_bytes=VMEM_LIMIT)


def _rms_rows(x, gain):
    ms = jnp.mean(x * x, axis=-1, keepdims=True)
    return x * lax.rsqrt(ms + EPS) * gain


def _seq_split(s):
    n1 = 1 << ((s.bit_length() - 1 + 1) // 2)
    n2 = s // n1
    assert n1 * n2 == s
    return n1, n2


SUBLANES = 8
BF16_ROWS = 2 * SUBLANES


def _pack_complex(re, im):
    rb = lax.bitcast_convert_type(re.astype(BF16).astype(F32), jnp.uint32)
    ib = lax.bitcast_convert_type(im.astype(BF16).astype(F32), jnp.uint32)
    return rb | (ib >> 16)


def _unpack_complex(u):
    re = lax.bitcast_convert_type(u & jnp.uint32(0xFFFF0000), F32)
    im = lax.bitcast_convert_type(u << 16, F32)
    return re, im


def _fourier_stage1_kernel(x_ref, g_ref, cs_ref, tab_ref, o_ref, zs_ref, *, gd):
    n1 = x_ref.shape[1]
    rows = n1 * SUBLANES
    d = x_ref.shape[3]
    nblk = 2 * gd // LANES
    h = _rms_rows(x_ref[0].reshape(rows, d), g_ref[...]).astype(BF16)
    cs = cs_ref[...]
    for g in range(FFT_GROUPS):
        slot = g % 2
        r = jnp.dot(h[:, g * gd:(g + 1) * gd], cs, preferred_element_type=F32)
        for c in range(nblk):
            zs_ref[slot, c] = r[:, c * LANES:(c + 1) * LANES]
        for j in range(SUBLANES):
            parts = [zs_ref[slot, c, pl.ds(j, n1, stride=SUBLANES), :] for c in range(nblk)]
            zr = jnp.concatenate(parts[:nblk // 2], axis=1)
            zi = jnp.concatenate(parts[nblk // 2:], axis=1)
            zc = jnp.concatenate([zr, zi], axis=0).astype(BF16)
            y = jnp.dot(tab_ref[j], zc, preferred_element_type=F32)
            o_ref[0, j, :, g * gd:(g + 1) * gd] = _pack_complex(y[:n1], y[n1:])


def _fourier_stage1(x4, gain, cs, tab):
    b, n1, n2, d = x4.shape
    gd = d // FFT_GROUPS
    rows = n1 * SUBLANES
    return pl.pallas_call(
        functools.partial(_fourier_stage1_kernel, gd=gd),
        grid=(b, n2 // SUBLANES),
        in_specs=[
            pl.BlockSpec((1, n1, SUBLANES, d), lambda bi, j: (bi, 0, j, 0)),
            pl.BlockSpec((1, d), lambda bi, j: (0, 0)),
            pl.BlockSpec((gd, 2 * gd), lambda bi, j: (0, 0)),
            pl.BlockSpec((SUBLANES, 2 * n1, 2 * n1), lambda bi, j: (j, 0, 0)),
        ],
        out_specs=pl.BlockSpec((1, SUBLANES, n1, d), lambda bi, j: (bi, j, 0, 0)),
        out_shape=jax.ShapeDtypeStruct((b, n2, n1, d), jnp.uint32),
        scratch_shapes=[pltpu.VMEM((2, 2 * gd // LANES, rows, LANES), F32)],
        compiler_params=_cparams(("parallel", "parallel")),
        name="fourier_stage1",
    )(x4, gain, cs, tab)


def _fourier_stage2_kernel(y_ref, tab_ref, o_ref, ys_ref, fs_ref, *, cw):
    n2 = y_ref.shape[1]
    rows = n2 * SUBLANES
    d = y_ref.shape[3]
    nblk = cw // LANES
    tab = tab_ref[...]
    for ch in range(d // cw):
        slot = ch % 2
        for c in range(nblk):
            lanes = slice(ch * cw + c * LANES, ch * cw + (c + 1) * LANES)
            ys_ref[slot, c] = y_ref[0, :, :, lanes].reshape(rows, LANES)
        for k in range(SUBLANES):
            u = jnp.concatenate(
                [ys_ref[slot, c, pl.ds(k, n2, stride=SUBLANES), :] for c in range(nblk)], axis=1)
            re, im = _unpack_complex(u)
            zc = jnp.concatenate([re, im], axis=0).astype(BF16)
            f = jnp.dot(tab, zc, preferred_element_type=F32)
            for c in range(nblk):
                fs_ref[slot, c, pl.ds(k, n2, stride=SUBLANES), :] = f[:, c * LANES:(c + 1) * LANES]
        for c in range(nblk):
            lanes = slice(ch * cw + c * LANES, ch * cw + (c + 1) * LANES)
            o_ref[0, :, :, lanes] = fs_ref[slot, c].reshape(n2, SUBLANES, LANES)


def _fourier_stage2(y, tab):
    b, n2, n1, d = y.shape
    cw = MXU_DIM
    rows = n2 * SUBLANES
    return pl.pallas_call(
        functools.partial(_fourier_stage2_kernel, cw=cw),
        grid=(b, n1 // SUBLANES),
        in_specs=[
            pl.BlockSpec((1, n2, SUBLANES, d), lambda bi, k: (bi, 0, k, 0)),
            pl.BlockSpec((n2, 2 * n2), lambda bi, k: (0, 0)),
        ],
        out_specs=pl.BlockSpec((1, n2, SUBLANES, d), lambda bi, k: (bi, 0, k, 0)),
        out_shape=jax.ShapeDtypeStruct((b, n2, n1, d), F32),
        scratch_shapes=[pltpu.VMEM((2, cw // LANES, rows, LANES), jnp.uint32),
                        pltpu.VMEM((2, cw // LANES, rows, LANES), F32)],
        compiler_params=_cparams(("parallel", "parallel")),
        name="fourier_stage2",
    )(y, tab)


def _proj_norm_kernel(a_ref, x_ref, w_ref, g_ref, x1_ref, h_ref):
    x1 = x_ref[...] + jnp.dot(a_ref[...].astype(BF16), w_ref[...], preferred_element_type=F32)
    x1_ref[...] = x1
    h_ref[...] = _rms_rows(x1, g_ref[...]).astype(BF16)


def _proj_norm(a, x2d, w, gain, tm):
    t, d = x2d.shape
    return pl.pallas_call(
        _proj_norm_kernel,
        grid=(t // tm,),
        in_specs=[
            pl.BlockSpec((tm, d), lambda i: (i, 0)),
            pl.BlockSpec((tm, d), lambda i: (i, 0)),
            pl.BlockSpec((None, d, d), lambda i: (0, 0, 0)),
            pl.BlockSpec((1, d), lambda i: (0, 0)),
        ],
        out_specs=[
            pl.BlockSpec((tm, d), lambda i: (i, 0)),
            pl.BlockSpec((tm, d), lambda i: (i, 0)),
        ],
        out_shape=[jax.ShapeDtypeStruct((t, d), F32), jax.ShapeDtypeStruct((t, d), BF16)],
        compiler_params=_cparams(("parallel",)),
        name="proj_norm",
    )(a, x2d, w, gain)


def _mlp_kernel(h_ref, x_ref, wu_hbm, wd_hbm, o_ref, wu_buf, wd_buf, sem, *, layer, tf, ts):
    i = pl.program_id(0)
    nchunk = wu_hbm.shape[-1] // tf
    assert nchunk % 2 == 0

    def chunk_copies(j, slot):
        return (
            pltpu.make_async_copy(wu_hbm.at[layer, :, pl.ds(j * tf, tf)], wu_buf.at[slot], sem.at[0, slot]),
            pltpu.make_async_copy(wd_hbm.at[layer, pl.ds(j * tf, tf), :], wd_buf.at[slot], sem.at[1, slot]),
        )

    def start(j, slot):
        for c in chunk_copies(j, slot):
            c.start()

    @pl.when(i == 0)
    def _():
        start(0, 0)

    o_ref[...] = x_ref[...]
    npairs = nchunk // 2

    def pair_body(jj, carry):
        for slot in range(2):
            j = 2 * jj + slot
            for c in chunk_copies(j, slot):
                c.wait()
            if slot == 0:
                start(j + 1, 1)
            else:
                @pl.when(jj + 1 < npairs)
                def _():
                    start(j + 1, 0)

                @pl.when(jnp.logical_and(jj + 1 == npairs, i + 1 < pl.num_programs(0)))
                def _():
                    start(0, 0)
            for s in range(tf // ts):
                a = jnp.maximum(
                    jnp.dot(h_ref[...], wu_buf[slot, :, s * ts:(s + 1) * ts], preferred_element_type=F32),
                    0.0)
                o_ref[...] += jnp.dot((a * a).astype(BF16), wd_buf[slot, s * ts:(s + 1) * ts, :],
                                      preferred_element_type=F32)
        return carry

    lax.fori_loop(0, npairs, pair_body, 0)


def _mlp(h, x1, w_up, w_down, layer, tm, tf):
    t, d = x1.shape
    ts = min(tf, 2 * MXU_DIM)
    return pl.pallas_call(
        functools.partial(_mlp_kernel, layer=layer, tf=tf, ts=ts),
        grid=(t // tm,),
        in_specs=[
            pl.BlockSpec((tm, d), lambda i: (i, 0)),
            pl.BlockSpec((tm, d), lambda i: (i, 0)),
            pl.BlockSpec(memory_space=pl.ANY),
            pl.BlockSpec(memory_space=pl.ANY),
        ],
        out_specs=pl.BlockSpec((tm, d), lambda i: (i, 0)),
        out_shape=jax.ShapeDtypeStruct((t, d), F32),
        scratch_shapes=[
            pltpu.VMEM((2, d, tf), BF16),
            pltpu.VMEM((2, tf, d), BF16),
            pltpu.SemaphoreType.DMA((2, 2)),
        ],
        compiler_params=_cparams(("arbitrary",)),
        name="sqrelu_mlp",
    )(h, x1, w_up, w_down)


def _qkv_kernel(x_ref, g_ref, w_ref, e_ref, qg_ref, kg_ref, cf_ref, s1_ref, s2_ref,
                q_ref, k_ref, v_ref, *, d, scale):
    h = _rms_rows(x_ref[...], g_ref[...]).astype(BF16)
    e = e_ref[...]
    cf, s1, s2 = cf_ref[...], s1_ref[...], s2_ref[...]
    kvd = N_KV * HEAD_DIM
    lo = lax.broadcasted_iota(jnp.int32, (h.shape[0], LANES), 1) < HEAD_DIM

    def proj(c0, width):
        return jnp.dot(h, w_ref[:, c0:c0 + width], preferred_element_type=F32)

    def head_norm_rope(blk, gain, out_scale):
        ss = jnp.dot((blk * blk).astype(BF16), e, preferred_element_type=F32)
        n = blk * lax.rsqrt(ss * (1.0 / HEAD_DIM) + EPS) * gain
        outs = []
        for u in range(MXU_DIM // LANES):
            xb = n[:, u * LANES:(u + 1) * LANES]
            r = (xb * cf + pltpu.roll(xb, LANES - ROT_HALF, axis=1) * s1
                 + pltpu.roll(xb, ROT_HALF, axis=1) * s2)
            outs.append(r * out_scale if out_scale != 1.0 else r)
        return outs

    def store_duplicated(ref, halves):
        for u, xb in enumerate(halves):
            xr = pltpu.roll(xb, HEAD_DIM, axis=1)
            ref[:, (2 * u) * LANES:(2 * u + 1) * LANES] = jnp.where(lo, xb, xr).astype(BF16)
            ref[:, (2 * u + 1) * LANES:(2 * u + 2) * LANES] = jnp.where(lo, xr, xb).astype(BF16)

    pw = 2 * MXU_DIM
    for c in range(d // pw):
        blk = proj(c * pw, pw)
        outs = (head_norm_rope(blk[:, :MXU_DIM], qg_ref[...], scale)
                + head_norm_rope(blk[:, MXU_DIM:], qg_ref[...], scale))
        q_ref[:, c * pw:(c + 1) * pw] = jnp.concatenate(outs, axis=1).astype(BF16)
    assert kvd == MXU_DIM
    kv = proj(d, 2 * kvd)
    store_duplicated(k_ref, head_norm_rope(kv[:, :kvd], kg_ref[...], 1.0))
    store_duplicated(v_ref, [kv[:, kvd + u * LANES:kvd + (u + 1) * LANES] for u in range(kvd // LANES)])


def _qkv(x2d, gain, w, e, qg, kg, cf, s1, s2, seq, tm):
    t, d = x2d.shape
    kd = N_KV * LANES
    nt = seq // tm
    tab_spec = pl.BlockSpec((tm, LANES), lambda i: (i % nt, 0))
    const = lambda shape: pl.BlockSpec(shape, lambda i: (0, 0))
    return pl.pallas_call(
        functools.partial(_qkv_kernel, d=d, scale=LOG2E / math.sqrt(HEAD_DIM)),
        grid=(t // tm,),
        in_specs=[
            pl.BlockSpec((tm, d), lambda i: (i, 0)),
            const((1, d)),
            pl.BlockSpec((None,) + w.shape[1:], lambda i: (0, 0, 0)),
            const((MXU_DIM, MXU_DIM)),
            const((1, MXU_DIM)),
            const((1, MXU_DIM)),
            tab_spec, tab_spec, tab_spec,
        ],
        out_specs=[
            pl.BlockSpec((tm, d), lambda i: (i, 0)),
            pl.BlockSpec((tm, kd), lambda i: (i, 0)),
            pl.BlockSpec((tm, kd), lambda i: (i, 0)),
        ],
        out_shape=[
            jax.ShapeDtypeStruct((t, d), BF16),
            jax.ShapeDtypeStruct((t, kd), BF16),
            jax.ShapeDtypeStruct((t, kd), BF16),
        ],
        compiler_params=_cparams(("parallel",)),
        name="qkv_norm_rope",
    )(x2d, gain, w, e, qg, kg, cf, s1, s2)


def _attn_kernel(sink_ref, q_ref, kp_ref, kc_ref, kn_ref, vp_ref, vc_ref, vn_ref, o_ref, *, nb):
    j = pl.program_id(1)
    bq = WINDOW_BLOCK
    npair = GROUP // 2
    nq = GROUP * bq
    lo = lax.broadcasted_iota(jnp.int32, (bq, LANES), 1) < HEAD_DIM
    key = lax.broadcasted_iota(jnp.int32, (bq, nq), 0)
    qry = lax.broadcasted_iota(jnp.int32, (bq, nq), 1) % bq
    vprev = (key >= qry) & (j > 0)
    vnext = (key <= qry) & (j < nb - 1)
    zero = jnp.zeros((bq, LANES), BF16)
    xr = BF16_ROWS
    lo_x = lax.broadcasted_iota(jnp.int32, (3 * bq + xr, LANES), 1) < HEAD_DIM
    sink_row = lax.broadcasted_iota(jnp.int32, (xr, nq), 0) == 0
    vzero = jnp.zeros((xr, LANES), BF16)
    nt = (((1,), (1,)), ((), ()))
    tn = (((0,), (0,)), ((), ()))

    for h in range(N_KV):
        ks = slice(h * LANES, (h + 1) * LANES)
        k2 = jnp.concatenate([kp_ref[0, :, ks], kc_ref[0, :, ks], kn_ref[0, :, ks]], axis=0)
        v2x = jnp.concatenate([vp_ref[0, :, ks], vc_ref[0, :, ks], vn_ref[0, :, ks], vzero], axis=0)
        qparts, sinks = [], []
        for parity in range(2):
            for p in range(npair):
                c0 = (h * npair + p) * LANES
                blk = q_ref[0, :, c0:c0 + LANES]
                qparts.append(jnp.where(lo if parity == 0 else ~lo, blk, zero))
                sinks.append(jnp.full((1, bq), sink_ref[h * GROUP + 2 * p + parity] * LOG2E, F32))
        qs = jnp.concatenate(qparts, axis=0)
        sink = jnp.concatenate(sinks, axis=1)
        st = lax.dot_general(k2, qs, nt, preferred_element_type=F32)
        sp = jnp.where(vprev, st[:bq], NEG)
        sc = st[bq:2 * bq]
        sn = jnp.where(vnext, st[2 * bq:], NEG)
        m = jnp.max(jnp.maximum(jnp.maximum(sp, sc), sn), axis=0, keepdims=True)
        m = jnp.maximum(m, sink)
        es = jnp.where(sink_row, jnp.exp2(sink - m), 0.0)
        pt = jnp.concatenate(
            [jnp.exp2(sp - m), jnp.exp2(sc - m), jnp.exp2(sn - m), es], axis=0).astype(BF16)
        ot = lax.dot_general(jnp.where(lo_x, v2x, 1.0), pt, tn, preferred_element_type=F32)
        res = ot[:HEAD_DIM] * (1.0 / ot[HEAD_DIM:HEAD_DIM + 1])
        half = nq // 2
        for p in range(npair):
            pair = jnp.concatenate(
                [res[:, p * bq:(p + 1) * bq], res[:, half + p * bq:half + (p + 1) * bq]], axis=0)
            c0 = (h * npair + p) * LANES
            o_ref[0, :, c0:c0 + LANES] = pair.T.astype(BF16)


def _attention(sink, q, k, v):
    b, s, d = q.shape
    kd = k.shape[-1]
    bq = WINDOW_BLOCK
    nb = s // bq
    prev = lambda bi, j: (bi, jnp.maximum(j - 1, 0), 0)
    cur = lambda bi, j: (bi, j, 0)
    nxt = lambda bi, j: (bi, jnp.minimum(j + 1, nb - 1), 0)
    kv = lambda im: pl.BlockSpec((1, bq, kd), im)
    return pl.pallas_call(
        functools.partial(_attn_kernel, nb=nb),
        grid=(b, nb),
        in_specs=[
            pl.BlockSpec(memory_space=pltpu.SMEM),
            pl.BlockSpec((1, bq, d), cur),
            kv(prev), kv(cur), kv(nxt), kv(prev), kv(cur), kv(nxt),
        ],
        out_specs=pl.BlockSpec((1, bq, d), cur),
        out_shape=jax.ShapeDtypeStruct((b, s, d), BF16),
        compiler_params=_cparams(("parallel", "parallel")),
        name="window_attention",
    )(sink, q, k, k, k, v, v, v)


def _chan_table(gd):
    k = np.arange(gd)
    ang = 2.0 * np.pi * ((k[:, None] * k[None, :]) % gd) / gd
    sc = 1.0 / math.sqrt(gd)
    return jnp.asarray(np.concatenate([np.cos(ang) * sc, -np.sin(ang) * sc], axis=1), BF16)


def _stage1_table(n1, n2):
    n = n1 * n2
    k1 = jnp.arange(n1, dtype=jnp.int32)
    pos = n2 * jnp.arange(n1, dtype=jnp.int32)[None, None, :] + jnp.arange(n2, dtype=jnp.int32)[:, None, None]
    ang = ((k1[None, :, None] * pos) % n).astype(F32) * (2.0 * math.pi / n)
    c, s = jnp.cos(ang), jnp.sin(ang)
    top = jnp.concatenate([c, s], axis=-1)
    bot = jnp.concatenate([-s, c], axis=-1)
    return jnp.concatenate([top, bot], axis=-2).astype(BF16)


def _stage2_table(n1, n2):
    k = np.arange(n2)
    ang = 2.0 * np.pi * ((k[:, None] * k[None, :]) % n2) / n2
    sc = 1.0 / math.sqrt(n1 * n2)
    return jnp.asarray(np.concatenate([np.cos(ang) * sc, np.sin(ang) * sc], axis=1), BF16)


def _rope_lane_tables(s):
    inv_freq = ROPE_THETA ** (-jnp.arange(0, ROT_DIM, 2, dtype=F32) / ROT_DIM)
    ang = jnp.arange(s, dtype=F32)[:, None] * inv_freq[None, :]
    cos, sin = jnp.cos(ang), jnp.sin(ang)
    pad = HEAD_DIM - ROT_DIM
    ones = jnp.ones((s, pad), F32)
    zeros = jnp.zeros((s, pad), F32)
    zh = jnp.zeros((s, ROT_HALF), F32)
    cf = jnp.concatenate([cos, cos, ones], axis=1)
    s1 = jnp.concatenate([-sin, zh, zeros], axis=1)
    s2 = jnp.concatenate([zh, sin, zeros], axis=1)
    rep = LANES // HEAD_DIM
    return tuple(jnp.tile(a, (1, rep)) for a in (cf, s1, s2))


def _segment_ones():
    i = np.arange(MXU_DIM) // HEAD_DIM
    return jnp.asarray((i[:, None] == i[None, :]).astype(np.float32), BF16)


def _fourier_mixer(x, gain, w_out, mlp_gain, tm):
    b, s, d = x.shape
    gd = d // FFT_GROUPS
    n1, n2 = _seq_split(s)
    y = _fourier_stage1(x.reshape(b, n1, n2, d), gain, _chan_table(gd), _stage1_table(n1, n2))
    f = _fourier_stage2(y, _stage2_table(n1, n2))
    return _proj_norm(f.reshape(b * s, d), x.reshape(b * s, d), w_out, mlp_gain, tm)


def _attn_mixer(x2d, b, s, gain, w_qkv, q_gain, k_gain, sink, w_o, mlp_gain, tm):
    t, d = x2d.shape
    rep = MXU_DIM // HEAD_DIM
    cf, s1, s2 = _rope_lane_tables(s)
    q, k, v = _qkv(x2d, gain, w_qkv, _segment_ones(),
                   jnp.tile(q_gain, rep)[None, :], jnp.tile(k_gain, rep)[None, :],
                   cf, s1, s2, s, tm)
    kd = k.shape[-1]
    o = _attention(sink, q.reshape(b, s, d), k.reshape(b, s, kd), v.reshape(b, s, kd))
    return _proj_norm(o.reshape(t, d), x2d, w_o, mlp_gain, tm)


def _trunk(x, p, tm, tm_mlp, tf):
    b, s, d = x.shape
    x1, h = _fourier_mixer(x, p["fourier_norm"][0][None, :], p["fourier_w_out"],
                           p["mlp_norm"][0][None, :], tm)
    x2 = _mlp(h, x1, p["mlp_w_up"], p["mlp_w_down"], 0, tm_mlp, tf)
    x3, h = _attn_mixer(x2, b, s, p["attn_norm"][0][None, :], p["attn_w_qkv"], p["attn_q_norm"][0],
                        p["attn_k_norm"][0], p["attn_sink"][0], p["attn_w_o"],
                        p["mlp_norm"][1][None, :], tm)
    x4 = _mlp(h, x3, p["mlp_w_up"], p["mlp_w_down"], 1, tm_mlp, tf)
    return x4.reshape(b, s, d)


def kernel(x_prompt, x_sample, fourier_norm, fourier_w_out, attn_norm, attn_w_qkv, attn_q_norm,
           attn_k_norm, attn_sink, attn_w_o, mlp_norm, mlp_w_up, mlp_w_down):
    p = dict(
        fourier_norm=fourier_norm, fourier_w_out=fourier_w_out.astype(BF16),
        attn_norm=attn_norm, attn_w_qkv=attn_w_qkv.astype(BF16), attn_q_norm=attn_q_norm,
        attn_k_norm=attn_k_norm, attn_sink=attn_sink, attn_w_o=attn_w_o.astype(BF16),
        mlp_norm=mlp_norm, mlp_w_up=mlp_w_up.astype(BF16), mlp_w_down=mlp_w_down.astype(BF16),
    )
    outs = []
    for x in (x_prompt, x_sample):
        t = x.shape[0] * x.shape[1]
        tm = min(512, x.shape[1])
        tm_mlp = min(1024, t)
        tf = min(1024, mlp_w_up.shape[-1] // 2)
        outs.append(_trunk(x, p, tm, tm_mlp, tf))
    return tuple(outs)
```

```python
import functools
import math

import numpy as np
import jax
import jax.numpy as jnp
from jax import lax
from jax.experimental import pallas as pl
from jax.experimental.pallas import tpu as pltpu

F32 = jnp.float32
BF16 = jnp.bfloat16

FFT_GROUPS = 8
HEAD_DIM = 64
N_KV = 4
GROUP = 8
WINDOW_BLOCK = 128
ROPE_THETA = 500000.0
ROT_DIM = HEAD_DIM // 4
ROT_HALF = ROT_DIM // 2
EPS = 1e-6
NEG = -1e30
LOG2E = math.log2(math.e)
ATTN_SUB = 2
LANES = 128
MXU_DIM = 256
VMEM_LIMIT = 60 * 1024 * 1024


def _cparams(sem):
    return pltpu.CompilerParams(dimension_semantics=sem, vmem_limit_bytes=VMEM_LIMIT)


def _rms_rows(x, gain):
    ms = jnp.mean(x * x, axis=-1, keepdims=True)
    return x * lax.rsqrt(ms + EPS) * gain


def _seq_split(s):
    n1 = 1 << ((s.bit_length() - 1 + 1) // 2)
    n2 = s // n1
    assert n1 * n2 == s
    return n1, n2


SUBLANES = 8
BF16_ROWS = 2 * SUBLANES
STAGE1_GROUPS = 2


def _pack_complex(re, im):
    rb = lax.bitcast_convert_type(re.astype(BF16).astype(F32), jnp.uint32)
    ib = lax.bitcast_convert_type(im.astype(BF16).astype(F32), jnp.uint32)
    return rb | (ib >> 16)


def _unpack_complex(u):
    re = lax.bitcast_convert_type(u & jnp.uint32(0xFFFF0000), F32)
    im = lax.bitcast_convert_type(u << 16, F32)
    return re, im


def _fourier_stage1_kernel(x_ref, g_ref, cs_ref, tab_ref, o_ref, zs_ref, *, gd):
    n1 = x_ref.shape[1]
    rows = n1 * SUBLANES
    d = x_ref.shape[3]
    nblk = 2 * gd // LANES
    h = _rms_rows(x_ref[0].reshape(rows, d), g_ref[...]).astype(BF16)
    cs = cs_ref[...]
    gp = STAGE1_GROUPS
    for g0 in range(0, FFT_GROUPS, gp):
        slot = (g0 // gp) % 2
        for gi in range(gp):
            g = g0 + gi
            r = jnp.dot(h[:, g * gd:(g + 1) * gd], cs, preferred_element_type=F32)
            for c in range(nblk):
                zs_ref[slot, gi * nblk + c] = r[:, c * LANES:(c + 1) * LANES]
        for j in range(SUBLANES):
            parts = [zs_ref[slot, c, pl.ds(j, n1, stride=SUBLANES), :] for c in range(gp * nblk)]
            zr = jnp.concatenate(
                [parts[gi * nblk + c] for gi in range(gp) for c in range(nblk // 2)], axis=1)
            zi = jnp.concatenate(
                [parts[gi * nblk + nblk // 2 + c] for gi in range(gp) for c in range(nblk // 2)], axis=1)
            zc = jnp.concatenate([zr, zi], axis=0).astype(BF16)
            y = jnp.dot(tab_ref[j], zc, preferred_element_type=F32)
            o_ref[0, j, :, g0 * gd:(g0 + gp) * gd] = _pack_complex(y[:n1], y[n1:])


def _fourier_stage1(x4, gain, cs, tab):
    b, n1, n2, d = x4.shape
    gd = d // FFT_GROUPS
    rows = n1 * SUBLANES
    return pl.pallas_call(
        functools.partial(_fourier_stage1_kernel, gd=gd),
        grid=(b, n2 // SUBLANES),
        in_specs=[
            pl.BlockSpec((1, n1, SUBLANES, d), lambda bi, j: (bi, 0, j, 0)),
            pl.BlockSpec((1, d), lambda bi, j: (0, 0)),
            pl.BlockSpec((gd, 2 * gd), lambda bi, j: (0, 0)),
            pl.BlockSpec((SUBLANES, 2 * n1, 2 * n1), lambda bi, j: (j, 0, 0)),
        ],
        out_specs=pl.BlockSpec((1, SUBLANES, n1, d), lambda bi, j: (bi, j, 0, 0)),
        out_shape=jax.ShapeDtypeStruct((b, n2, n1, d), jnp.uint32),
        scratch_shapes=[pltpu.VMEM((2, STAGE1_GROUPS * 2 * gd // LANES, rows, LANES), F32)],
        compiler_params=_cparams(("parallel", "parallel")),
        name="fourier_stage1",
    )(x4, gain, cs, tab)


def _fourier_stage2_kernel(y_ref, tab_ref, o_ref, ys_ref, fs_ref, *, cw):
    n2 = y_ref.shape[1]
    rows = n2 * SUBLANES
    d = y_ref.shape[3]
    nblk = cw // LANES
    tab = tab_ref[...]
    for ch in range(d // cw):
        slot = ch % 2
        for c in range(nblk):
            lanes = slice(ch * cw + c * LANES, ch * cw + (c + 1) * LANES)
            ys_ref[slot, c] = y_ref[0, :, :, lanes].reshape(rows, LANES)
        for k in range(SUBLANES):
            u = jnp.concatenate(
                [ys_ref[slot, c, pl.ds(k, n2, stride=SUBLANES), :] for c in range(nblk)], axis=1)
            re, im = _unpack_complex(u)
            zc = jnp.concatenate([re, im], axis=0).astype(BF16)
            f = jnp.dot(tab, zc, preferred_element_type=F32)
            for c in range(nblk):
                fs_ref[slot, c, pl.ds(k, n2, stride=SUBLANES), :] = f[:, c * LANES:(c + 1) * LANES]
        for c in range(nblk):
            lanes = slice(ch * cw + c * LANES, ch * cw + (c + 1) * LANES)
            o_ref[0, :, :, lanes] = fs_ref[slot, c].reshape(n2, SUBLANES, LANES)


def _fourier_stage2(y, tab):
    b, n2, n1, d = y.shape
    cw = 2 * MXU_DIM
    rows = n2 * SUBLANES
    return pl.pallas_call(
        functools.partial(_fourier_stage2_kernel, cw=cw),
        grid=(b, n1 // SUBLANES),
        in_specs=[
            pl.BlockSpec((1, n2, SUBLANES, d), lambda bi, k: (bi, 0, k, 0)),
            pl.BlockSpec((n2, 2 * n2), lambda bi, k: (0, 0)),
        ],
        out_specs=pl.BlockSpec((1, n2, SUBLANES, d), lambda bi, k: (bi, 0, k, 0)),
        out_shape=jax.ShapeDtypeStruct((b, n2, n1, d), F32),
        scratch_shapes=[pltpu.VMEM((2, cw // LANES, rows, LANES), jnp.uint32),
                        pltpu.VMEM((2, cw // LANES, rows, LANES), F32)],
        compiler_params=_cparams(("parallel", "parallel")),
        name="fourier_stage2",
    )(y, tab)


def _proj_norm_kernel(a_ref, x_ref, w_ref, g_ref, x1_ref, h_ref):
    x1 = x_ref[...] + jnp.dot(a_ref[...].astype(BF16), w_ref[...], preferred_element_type=F32)
    x1_ref[...] = x1
    h_ref[...] = _rms_rows(x1, g_ref[...]).astype(BF16)


def _proj_norm(a, x2d, w, gain, tm):
    t, d = x2d.shape
    return pl.pallas_call(
        _proj_norm_kernel,
        grid=(t // tm,),
        in_specs=[
            pl.BlockSpec((tm, d), lambda i: (i, 0)),
            pl.BlockSpec((tm, d), lambda i: (i, 0)),
            pl.BlockSpec((None, d, d), lambda i: (0, 0, 0)),
            pl.BlockSpec((1, d), lambda i: (0, 0)),
        ],
        out_specs=[
            pl.BlockSpec((tm, d), lambda i: (i, 0)),
            pl.BlockSpec((tm, d), lambda i: (i, 0)),
        ],
        out_shape=[jax.ShapeDtypeStruct((t, d), F32), jax.ShapeDtypeStruct((t, d), BF16)],
        compiler_params=_cparams(("parallel",)),
        name="proj_norm",
    )(a, x2d, w, gain)


def _mlp_kernel(h_ref, x_ref, wu_hbm, wd_hbm, o_ref, wu_buf, wd_buf, sem, *, layer, tf, ts):
    i = pl.program_id(0)
    nchunk = wu_hbm.shape[-1] // tf
    assert nchunk % 2 == 0

    def chunk_copies(j, slot):
        return (
            pltpu.make_async_copy(wu_hbm.at[layer, :, pl.ds(j * tf, tf)], wu_buf.at[slot], sem.at[0, slot]),
            pltpu.make_async_copy(wd_hbm.at[layer, pl.ds(j * tf, tf), :], wd_buf.at[slot], sem.at[1, slot]),
        )

    def start(j, slot):
        for c in chunk_copies(j, slot):
            c.start()

    @pl.when(i == 0)
    def _():
        start(0, 0)

    o_ref[...] = x_ref[...]
    npairs = nchunk // 2

    def pair_body(jj, carry):
        for slot in range(2):
            j = 2 * jj + slot
            for c in chunk_copies(j, slot):
                c.wait()
            if slot == 0:
                start(j + 1, 1)
            else:
                @pl.when(jj + 1 < npairs)
                def _():
                    start(j + 1, 0)

                @pl.when(jnp.logical_and(jj + 1 == npairs, i + 1 < pl.num_programs(0)))
                def _():
                    start(0, 0)
            for s in range(tf // ts):
                a = jnp.maximum(
                    jnp.dot(h_ref[...], wu_buf[slot, :, s * ts:(s + 1) * ts], preferred_element_type=F32),
                    0.0)
                o_ref[...] += jnp.dot((a * a).astype(BF16), wd_buf[slot, s * ts:(s + 1) * ts, :],
                                      preferred_element_type=F32)
        return carry

    lax.fori_loop(0, npairs, pair_body, 0)


def _mlp(h, x1, w_up, w_down, layer, tm, tf):
    t, d = x1.shape
    ts = min(tf, 2 * MXU_DIM)
    return pl.pallas_call(
        functools.partial(_mlp_kernel, layer=layer, tf=tf, ts=ts),
        grid=(t // tm,),
        in_specs=[
            pl.BlockSpec((tm, d), lambda i: (i, 0)),
            pl.BlockSpec((tm, d), lambda i: (i, 0)),
            pl.BlockSpec(memory_space=pl.ANY),
            pl.BlockSpec(memory_space=pl.ANY),
        ],
        out_specs=pl.BlockSpec((tm, d), lambda i: (i, 0)),
        out_shape=jax.ShapeDtypeStruct((t, d), F32),
        scratch_shapes=[
            pltpu.VMEM((2, d, tf), BF16),
            pltpu.VMEM((2, tf, d), BF16),
            pltpu.SemaphoreType.DMA((2, 2)),
        ],
        compiler_params=_cparams(("arbitrary",)),
        name="sqrelu_mlp",
    )(h, x1, w_up, w_down)


def _qkv_kernel(x_ref, g_ref, w_ref, e_ref, qg_ref, kg_ref, cf_ref, s1_ref, s2_ref,
                q_ref, k_ref, v_ref, *, d, scale):
    h = _rms_rows(x_ref[...], g_ref[...]).astype(BF16)
    e = e_ref[...]
    cf, s1, s2 = cf_ref[...], s1_ref[...], s2_ref[...]
    kvd = N_KV * HEAD_DIM
    lo = lax.broadcasted_iota(jnp.int32, (h.shape[0], LANES), 1) < HEAD_DIM

    def proj(c0, width):
        return jnp.dot(h, w_ref[:, c0:c0 + width], preferred_element_type=F32)

    def head_norm_rope(blk, gain, out_scale):
        ss = jnp.dot((blk * blk).astype(BF16), e, preferred_element_type=F32)
        n = blk * lax.rsqrt(ss * (1.0 / HEAD_DIM) + EPS) * gain
        outs = []
        for u in range(MXU_DIM // LANES):
            xb = n[:, u * LANES:(u + 1) * LANES]
            r = (xb * cf + pltpu.roll(xb, LANES - ROT_HALF, axis=1) * s1
                 + pltpu.roll(xb, ROT_HALF, axis=1) * s2)
            outs.append(r * out_scale if out_scale != 1.0 else r)
        return outs

    def store_duplicated(ref, halves):
        for u, xb in enumerate(halves):
            xr = pltpu.roll(xb, HEAD_DIM, axis=1)
            ref[:, (2 * u) * LANES:(2 * u + 1) * LANES] = jnp.where(lo, xb, xr).astype(BF16)
            ref[:, (2 * u + 1) * LANES:(2 * u + 2) * LANES] = jnp.where(lo, xr, xb).astype(BF16)

    pw = 2 * MXU_DIM
    for c in range(d // pw):
        blk = proj(c * pw, pw)
        outs = (head_norm_rope(blk[:, :MXU_DIM], qg_ref[...], scale)
                + head_norm_rope(blk[:, MXU_DIM:], qg_ref[...], scale))
        q_ref[:, c * pw:(c + 1) * pw] = jnp.concatenate(outs, axis=1).astype(BF16)
    assert kvd == MXU_DIM
    kv = proj(d, 2 * kvd)
    store_duplicated(k_ref, head_norm_rope(kv[:, :kvd], kg_ref[...], 1.0))
    store_duplicated(v_ref, [kv[:, kvd + u * LANES:kvd + (u + 1) * LANES] for u in range(kvd // LANES)])


def _qkv(x2d, gain, w, e, qg, kg, cf, s1, s2, seq, tm):
    t, d = x2d.shape
    kd = N_KV * LANES
    nt = seq // tm
    tab_spec = pl.BlockSpec((tm, LANES), lambda i: (i % nt, 0))
    const = lambda shape: pl.BlockSpec(shape, lambda i: (0, 0))
    return pl.pallas_call(
        functools.partial(_qkv_kernel, d=d, scale=LOG2E / math.sqrt(HEAD_DIM)),
        grid=(t // tm,),
        in_specs=[
            pl.BlockSpec((tm, d), lambda i: (i, 0)),
            const((1, d)),
            pl.BlockSpec((None,) + w.shape[1:], lambda i: (0, 0, 0)),
            const((MXU_DIM, MXU_DIM)),
            const((1, MXU_DIM)),
            const((1, MXU_DIM)),
            tab_spec, tab_spec, tab_spec,
        ],
        out_specs=[
            pl.BlockSpec((tm, d), lambda i: (i, 0)),
            pl.BlockSpec((tm, kd), lambda i: (i, 0)),
            pl.BlockSpec((tm, kd), lambda i: (i, 0)),
        ],
        out_shape=[
            jax.ShapeDtypeStruct((t, d), BF16),
            jax.ShapeDtypeStruct((t, kd), BF16),
            jax.ShapeDtypeStruct((t, kd), BF16),
        ],
        compiler_params=_cparams(("parallel",)),
        name="qkv_norm_rope",
    )(x2d, gain, w, e, qg, kg, cf, s1, s2)


def _attn_kernel(sink_ref, q_ref, kp_ref, kc_ref, kn_ref, vp_ref, vc_ref, vn_ref, o_ref, *, nb, nsub):
    j = pl.program_id(1)
    bq = WINDOW_BLOCK
    npair = GROUP // 2
    nq = GROUP * bq
    lo = lax.broadcasted_iota(jnp.int32, (bq, LANES), 1) < HEAD_DIM
    key = lax.broadcasted_iota(jnp.int32, (bq, nq), 0)
    qry = lax.broadcasted_iota(jnp.int32, (bq, nq), 1) % bq
    zero = jnp.zeros((bq, LANES), BF16)
    xr = BF16_ROWS
    lo_x = lax.broadcasted_iota(jnp.int32, (3 * bq + xr, LANES), 1) < HEAD_DIM
    sink_row = lax.broadcasted_iota(jnp.int32, (xr, nq), 0) == 0
    vzero = jnp.zeros((xr, LANES), BF16)
    nt = (((1,), (1,)), ((), ()))
    tn = (((0,), (0,)), ((), ()))

    def kv_blocks(prev_ref, cur_ref, next_ref, u, lanes):
        rows = lambda t: slice(t * bq, (t + 1) * bq)
        before = prev_ref[0, :, lanes] if u == 0 else cur_ref[0, rows(u - 1), lanes]
        after = next_ref[0, :, lanes] if u == nsub - 1 else cur_ref[0, rows(u + 1), lanes]
        return [before, cur_ref[0, rows(u), lanes], after]

    for u in range(nsub):
        blk_idx = j * nsub + u
        vprev = (key >= qry) & (blk_idx > 0)
        vnext = (key <= qry) & (blk_idx < nb - 1)
        qrows = slice(u * bq, (u + 1) * bq)
        for h in range(N_KV):
            ks = slice(h * LANES, (h + 1) * LANES)
            k2 = jnp.concatenate(kv_blocks(kp_ref, kc_ref, kn_ref, u, ks), axis=0)
            v2x = jnp.concatenate(kv_blocks(vp_ref, vc_ref, vn_ref, u, ks) + [vzero], axis=0)
            qparts, sinks = [], []
            for parity in range(2):
                for p in range(npair):
                    c0 = (h * npair + p) * LANES
                    blk = q_ref[0, qrows, c0:c0 + LANES]
                    qparts.append(jnp.where(lo if parity == 0 else ~lo, blk, zero))
                    sinks.append(jnp.full((1, bq), sink_ref[h * GROUP + 2 * p + parity] * LOG2E, F32))
            qs = jnp.concatenate(qparts, axis=0)
            sink = jnp.concatenate(sinks, axis=1)
            st = lax.dot_general(k2, qs, nt, preferred_element_type=F32)
            sp = jnp.where(vprev, st[:bq], NEG)
            sc = st[bq:2 * bq]
            sn = jnp.where(vnext, st[2 * bq:], NEG)
            m = jnp.max(jnp.maximum(jnp.maximum(sp, sc), sn), axis=0, keepdims=True)
            m = jnp.maximum(m, sink)
            es = jnp.where(sink_row, jnp.exp2(sink - m), 0.0)
            pt = jnp.concatenate(
                [jnp.exp2(sp - m), jnp.exp2(sc - m), jnp.exp2(sn - m), es], axis=0).astype(BF16)
            ot = lax.dot_general(jnp.where(lo_x, v2x, 1.0), pt, tn, preferred_element_type=F32)
            res = ot[:HEAD_DIM] * (1.0 / ot[HEAD_DIM:HEAD_DIM + 1])
            half = nq // 2
            for p in range(npair):
                pair = jnp.concatenate(
                    [res[:, p * bq:(p + 1) * bq], res[:, half + p * bq:half + (p + 1) * bq]], axis=0)
                c0 = (h * npair + p) * LANES
                o_ref[0, qrows, c0:c0 + LANES] = pair.T.astype(BF16)


def _attention(sink, q, k, v):
    b, s, d = q.shape
    kd = k.shape[-1]
    bq = WINDOW_BLOCK
    nb = s // bq
    nsub = ATTN_SUB if nb % ATTN_SUB == 0 else 1
    prev = lambda bi, j: (bi, jnp.maximum(j * nsub - 1, 0), 0)
    cur = lambda bi, j: (bi, j, 0)
    nxt = lambda bi, j: (bi, jnp.minimum((j + 1) * nsub, nb - 1), 0)
    edge = lambda im: pl.BlockSpec((1, bq, kd), im)
    own = pl.BlockSpec((1, nsub * bq, kd), cur)
    return pl.pallas_call(
        functools.partial(_attn_kernel, nb=nb, nsub=nsub),
        grid=(b, nb // nsub),
        in_specs=[
            pl.BlockSpec(memory_space=pltpu.SMEM),
            pl.BlockSpec((1, nsub * bq, d), cur),
            edge(prev), own, edge(nxt), edge(prev), own, edge(nxt),
        ],
        out_specs=pl.BlockSpec((1, nsub * bq, d), cur),
        out_shape=jax.ShapeDtypeStruct((b, s, d), BF16),
        compiler_params=_cparams(("parallel", "parallel")),
        name="window_attention",
    )(sink, q, k, k, k, v, v, v)


def _chan_table(gd):
    k = np.arange(gd)
    ang = 2.0 * np.pi * ((k[:, None] * k[None, :]) % gd) / gd
    sc = 1.0 / math.sqrt(gd)
    return jnp.asarray(np.concatenate([np.cos(ang) * sc, -np.sin(ang) * sc], axis=1), BF16)


def _stage1_table(n1, n2):
    n = n1 * n2
    k1 = jnp.arange(n1, dtype=jnp.int32)
    pos = n2 * jnp.arange(n1, dtype=jnp.int32)[None, None, :] + jnp.arange(n2, dtype=jnp.int32)[:, None, None]
    ang = ((k1[None, :, None] * pos) % n).astype(F32) * (2.0 * math.pi / n)
    c, s = jnp.cos(ang), jnp.sin(ang)
    top = jnp.concatenate([c, s], axis=-1)
    bot = jnp.concatenate([-s, c], axis=-1)
    return jnp.concatenate([top, bot], axis=-2).astype(BF16)


def _stage2_table(n1, n2):
    k = np.arange(n2)
    ang = 2.0 * np.pi * ((k[:, None] * k[None, :]) % n2) / n2
    sc = 1.0 / math.sqrt(n1 * n2)
    return jnp.asarray(np.concatenate([np.cos(ang) * sc, np.sin(ang) * sc], axis=1), BF16)


def _rope_lane_tables(s):
    inv_freq = ROPE_THETA ** (-jnp.arange(0, ROT_DIM, 2, dtype=F32) / ROT_DIM)
    ang = jnp.arange(s, dtype=F32)[:, None] * inv_freq[None, :]
    cos, sin = jnp.cos(ang), jnp.sin(ang)
    pad = HEAD_DIM - ROT_DIM
    ones = jnp.ones((s, pad), F32)
    zeros = jnp.zeros((s, pad), F32)
    zh = jnp.zeros((s, ROT_HALF), F32)
    cf = jnp.concatenate([cos, cos, ones], axis=1)
    s1 = jnp.concatenate([-sin, zh, zeros], axis=1)
    s2 = jnp.concatenate([zh, sin, zeros], axis=1)
    rep = LANES // HEAD_DIM
    return tuple(jnp.tile(a, (1, rep)) for a in (cf, s1, s2))


def _segment_ones():
    i = np.arange(MXU_DIM) // HEAD_DIM
    return jnp.asarray((i[:, None] == i[None, :]).astype(np.float32), BF16)


def _fourier_mixer(x, gain, w_out, mlp_gain, tm):
    b, s, d = x.shape
    gd = d // FFT_GROUPS
    n1, n2 = _seq_split(s)
    y = _fourier_stage1(x.reshape(b, n1, n2, d), gain, _chan_table(gd), _stage1_table(n1, n2))
    f = _fourier_stage2(y, _stage2_table(n1, n2))
    return _proj_norm(f.reshape(b * s, d), x.reshape(b * s, d), w_out, mlp_gain, tm)


def _attn_mixer(x2d, b, s, gain, w_qkv, q_gain, k_gain, sink, w_o, mlp_gain, tm):
    t, d = x2d.shape
    rep = MXU_DIM // HEAD_DIM
    cf, s1, s2 = _rope_lane_tables(s)
    q, k, v = _qkv(x2d, gain, w_qkv, _segment_ones(),
                   jnp.tile(q_gain, rep)[None, :], jnp.tile(k_gain, rep)[None, :],
                   cf, s1, s2, s, tm)
    kd = k.shape[-1]
    o = _attention(sink, q.reshape(b, s, d), k.reshape(b, s, kd), v.reshape(b, s, kd))
    return _proj_norm(o.reshape(t, d), x2d, w_o, mlp_gain, tm)


def _trunk(x, p, tm, tm_mlp, tf):
    b, s, d = x.shape
    x1, h = _fourier_mixer(x, p["fourier_norm"][0][None, :], p["fourier_w_out"],
                           p["mlp_norm"][0][None, :], tm)
    x2 = _mlp(h, x1, p["mlp_w_up"], p["mlp_w_down"], 0, tm_mlp, tf)
    x3, h = _attn_mixer(x2, b, s, p["attn_norm"][0][None, :], p["attn_w_qkv"], p["attn_q_norm"][0],
                        p["attn_k_norm"][0], p["attn_sink"][0], p["attn_w_o"],
                        p["mlp_norm"][1][None, :], tm)
    x4 = _mlp(h, x3, p["mlp_w_up"], p["mlp_w_down"], 1, tm_mlp, tf)
    return x4.reshape(b, s, d)


def kernel(x_prompt, x_sample, fourier_norm, fourier_w_out, attn_norm, attn_w_qkv, attn_q_norm,
           attn_k_norm, attn_sink, attn_w_o, mlp_norm, mlp_w_up, mlp_w_down):
    p = dict(
        fourier_norm=fourier_norm, fourier_w_out=fourier_w_out.astype(BF16),
        attn_norm=attn_norm, attn_w_qkv=attn_w_qkv.astype(BF16), attn_q_norm=attn_q_norm,
        attn_k_norm=attn_k_norm, attn_sink=attn_sink, attn_w_o=attn_w_o.astype(BF16),
        mlp_norm=mlp_norm, mlp_w_up=mlp_w_up.astype(BF16), mlp_w_down=mlp_w_down.astype(BF16),
    )
    outs = []
    for x in (x_prompt, x_sample):
        t = x.shape[0] * x.shape[1]
        tm = min(512, x.shape[1])
        tm_mlp = min(1024, t)
        tf = min(1024, mlp_w_up.shape[-1] // 2)
        outs.append(_trunk(x, p, tm, tm_mlp, tf))
    return tuple(outs)
```

```python
import functools
import math

import numpy as np
import jax
import jax.numpy as jnp
from jax import lax
from jax.experimental import pallas as pl
from jax.experimental.pallas import tpu as pltpu

F32 = jnp.float32
BF16 = jnp.bfloat16

FFT_GROUPS = 8
HEAD_DIM = 64
N_KV = 4
GROUP = 8
WINDOW_BLOCK = 128
ROPE_THETA = 500000.0
ROT_DIM = HEAD_DIM // 4
ROT_HALF = ROT_DIM // 2
EPS = 1e-6
NEG = -1e30
LOG2E = math.log2(math.e)
ATTN_SUB = 4
LANES = 128
MXU_DIM = 256
VMEM_LIMIT = 60 * 1024 * 1024


def _cparams(sem):
    return pltpu.CompilerParams(dimension_semantics=sem, vmem_limit_bytes=VMEM_LIMIT)


def _rms_rows(x, gain):
    ms = jnp.mean(x * x, axis=-1, keepdims=True)
    return x * lax.rsqrt(ms + EPS) * gain


def _seq_split(s):
    n1 = 1 << ((s.bit_length() - 1 + 1) // 2)
    n2 = s // n1
    assert n1 * n2 == s
    return n1, n2


SUBLANES = 8
BF16_ROWS = 2 * SUBLANES
STAGE1_GROUPS = 2


def _pack_complex(re, im):
    rb = lax.bitcast_convert_type(re.astype(BF16).astype(F32), jnp.uint32)
    ib = lax.bitcast_convert_type(im.astype(BF16).astype(F32), jnp.uint32)
    return rb | (ib >> 16)


def _unpack_complex(u):
    re = lax.bitcast_convert_type(u & jnp.uint32(0xFFFF0000), F32)
    im = lax.bitcast_convert_type(u << 16, F32)
    return re, im


def _fourier_stage1_kernel(x_ref, g_ref, cs_ref, tab_ref, o_ref, zs_ref, *, gd):
    n1 = x_ref.shape[1]
    rows = n1 * SUBLANES
    d = x_ref.shape[3]
    nblk = 2 * gd // LANES
    h = _rms_rows(x_ref[0].reshape(rows, d), g_ref[...]).astype(BF16)
    cs = cs_ref[...]
    gp = STAGE1_GROUPS
    for g0 in range(0, FFT_GROUPS, gp):
        slot = (g0 // gp) % 2
        for gi in range(gp):
            g = g0 + gi
            r = jnp.dot(h[:, g * gd:(g + 1) * gd], cs, preferred_element_type=F32)
            for c in range(nblk):
                zs_ref[slot, gi * nblk + c] = r[:, c * LANES:(c + 1) * LANES]
        for j in range(SUBLANES):
            parts = [zs_ref[slot, c, pl.ds(j, n1, stride=SUBLANES), :] for c in range(gp * nblk)]
            zr = jnp.concatenate(
                [parts[gi * nblk + c] for gi in range(gp) for c in range(nblk // 2)], axis=1)
            zi = jnp.concatenate(
                [parts[gi * nblk + nblk // 2 + c] for gi in range(gp) for c in range(nblk // 2)], axis=1)
            zc = jnp.concatenate([zr, zi], axis=0).astype(BF16)
            y = jnp.dot(tab_ref[j], zc, preferred_element_type=F32)
            o_ref[0, j, :, g0 * gd:(g0 + gp) * gd] = _pack_complex(y[:n1], y[n1:])


def _fourier_stage1(x4, gain, cs, tab):
    b, n1, n2, d = x4.shape
    gd = d // FFT_GROUPS
    rows = n1 * SUBLANES
    return pl.pallas_call(
        functools.partial(_fourier_stage1_kernel, gd=gd),
        grid=(b, n2 // SUBLANES),
        in_specs=[
            pl.BlockSpec((1, n1, SUBLANES, d), lambda bi, j: (bi, 0, j, 0)),
            pl.BlockSpec((1, d), lambda bi, j: (0, 0)),
            pl.BlockSpec((gd, 2 * gd), lambda bi, j: (0, 0)),
            pl.BlockSpec((SUBLANES, 2 * n1, 2 * n1), lambda bi, j: (j, 0, 0)),
        ],
        out_specs=pl.BlockSpec((1, SUBLANES, n1, d), lambda bi, j: (bi, j, 0, 0)),
        out_shape=jax.ShapeDtypeStruct((b, n2, n1, d), jnp.uint32),
        scratch_shapes=[pltpu.VMEM((2, STAGE1_GROUPS * 2 * gd // LANES, rows, LANES), F32)],
        compiler_params=_cparams(("parallel", "parallel")),
        name="fourier_stage1",
    )(x4, gain, cs, tab)


def _fourier_stage2_kernel(y_ref, tab_ref, o_ref, ys_ref, fs_ref, *, cw):
    n2 = y_ref.shape[1]
    rows = n2 * SUBLANES
    d = y_ref.shape[3]
    nblk = cw // LANES
    tab = tab_ref[...]
    for ch in range(d // cw):
        slot = ch % 2
        for c in range(nblk):
            lanes = slice(ch * cw + c * LANES, ch * cw + (c + 1) * LANES)
            ys_ref[slot, c] = y_ref[0, :, :, lanes].reshape(rows, LANES)
        for k in range(SUBLANES):
            u = jnp.concatenate(
                [ys_ref[slot, c, pl.ds(k, n2, stride=SUBLANES), :] for c in range(nblk)], axis=1)
            re, im = _unpack_complex(u)
            zc = jnp.concatenate([re, im], axis=0).astype(BF16)
            f = jnp.dot(tab, zc, preferred_element_type=F32)
            for c in range(nblk):
                fs_ref[slot, c, pl.ds(k, n2, stride=SUBLANES), :] = f[:, c * LANES:(c + 1) * LANES]
        for c in range(nblk):
            lanes = slice(ch * cw + c * LANES, ch * cw + (c + 1) * LANES)
            o_ref[0, :, :, lanes] = fs_ref[slot, c].reshape(n2, SUBLANES, LANES)


def _fourier_stage2(y, tab):
    b, n2, n1, d = y.shape
    cw = 2 * MXU_DIM
    rows = n2 * SUBLANES
    return pl.pallas_call(
        functools.partial(_fourier_stage2_kernel, cw=cw),
        grid=(b, n1 // SUBLANES),
        in_specs=[
            pl.BlockSpec((1, n2, SUBLANES, d), lambda bi, k: (bi, 0, k, 0)),
            pl.BlockSpec((n2, 2 * n2), lambda bi, k: (0, 0)),
        ],
        out_specs=pl.BlockSpec((1, n2, SUBLANES, d), lambda bi, k: (bi, 0, k, 0)),
        out_shape=jax.ShapeDtypeStruct((b, n2, n1, d), F32),
        scratch_shapes=[pltpu.VMEM((2, cw // LANES, rows, LANES), jnp.uint32),
                        pltpu.VMEM((2, cw // LANES, rows, LANES), F32)],
        compiler_params=_cparams(("parallel", "parallel")),
        name="fourier_stage2",
    )(y, tab)


def _proj_norm_kernel(a_ref, x_ref, w_ref, g_ref, x1_ref, h_ref):
    x1 = x_ref[...] + jnp.dot(a_ref[...].astype(BF16), w_ref[...], preferred_element_type=F32)
    x1_ref[...] = x1
    h_ref[...] = _rms_rows(x1, g_ref[...]).astype(BF16)


def _proj_norm(a, x2d, w, gain, tm):
    t, d = x2d.shape
    return pl.pallas_call(
        _proj_norm_kernel,
        grid=(t // tm,),
        in_specs=[
            pl.BlockSpec((tm, d), lambda i: (i, 0)),
            pl.BlockSpec((tm, d), lambda i: (i, 0)),
            pl.BlockSpec((None, d, d), lambda i: (0, 0, 0)),
            pl.BlockSpec((1, d), lambda i: (0, 0)),
        ],
        out_specs=[
            pl.BlockSpec((tm, d), lambda i: (i, 0)),
            pl.BlockSpec((tm, d), lambda i: (i, 0)),
        ],
        out_shape=[jax.ShapeDtypeStruct((t, d), F32), jax.ShapeDtypeStruct((t, d), BF16)],
        compiler_params=_cparams(("parallel",)),
        name="proj_norm",
    )(a, x2d, w, gain)


def _mlp_kernel(h_ref, x_ref, wu_hbm, wd_hbm, o_ref, wu_buf, wd_buf, sem, *, layer, tf, ts):
    i = pl.program_id(0)
    nchunk = wu_hbm.shape[-1] // tf
    assert nchunk % 2 == 0

    def chunk_copies(j, slot):
        return (
            pltpu.make_async_copy(wu_hbm.at[layer, :, pl.ds(j * tf, tf)], wu_buf.at[slot], sem.at[0, slot]),
            pltpu.make_async_copy(wd_hbm.at[layer, pl.ds(j * tf, tf), :], wd_buf.at[slot], sem.at[1, slot]),
        )

    def start(j, slot):
        for c in chunk_copies(j, slot):
            c.start()

    @pl.when(i == 0)
    def _():
        start(0, 0)

    o_ref[...] = x_ref[...]
    npairs = nchunk // 2

    def pair_body(jj, carry):
        for slot in range(2):
            j = 2 * jj + slot
            for c in chunk_copies(j, slot):
                c.wait()
            if slot == 0:
                start(j + 1, 1)
            else:
                @pl.when(jj + 1 < npairs)
                def _():
                    start(j + 1, 0)

                @pl.when(jnp.logical_and(jj + 1 == npairs, i + 1 < pl.num_programs(0)))
                def _():
                    start(0, 0)
            for s in range(tf // ts):
                a = jnp.maximum(
                    jnp.dot(h_ref[...], wu_buf[slot, :, s * ts:(s + 1) * ts], preferred_element_type=F32),
                    0.0)
                o_ref[...] += jnp.dot((a * a).astype(BF16), wd_buf[slot, s * ts:(s + 1) * ts, :],
                                      preferred_element_type=F32)
        return carry

    lax.fori_loop(0, npairs, pair_body, 0)


def _mlp(h, x1, w_up, w_down, layer, tm, tf):
    t, d = x1.shape
    ts = min(tf, 2 * MXU_DIM)
    return pl.pallas_call(
        functools.partial(_mlp_kernel, layer=layer, tf=tf, ts=ts),
        grid=(t // tm,),
        in_specs=[
            pl.BlockSpec((tm, d), lambda i: (i, 0)),
            pl.BlockSpec((tm, d), lambda i: (i, 0)),
            pl.BlockSpec(memory_space=pl.ANY),
            pl.BlockSpec(memory_space=pl.ANY),
        ],
        out_specs=pl.BlockSpec((tm, d), lambda i: (i, 0)),
        out_shape=jax.ShapeDtypeStruct((t, d), F32),
        scratch_shapes=[
            pltpu.VMEM((2, d, tf), BF16),
            pltpu.VMEM((2, tf, d), BF16),
            pltpu.SemaphoreType.DMA((2, 2)),
        ],
        compiler_params=_cparams(("arbitrary",)),
        name="sqrelu_mlp",
    )(h, x1, w_up, w_down)


def _qkv_kernel(x_ref, g_ref, w_ref, e_ref, qg_ref, kg_ref, cf_ref, s1_ref, s2_ref,
                q_ref, k_ref, v_ref, *, d, scale):
    h = _rms_rows(x_ref[...], g_ref[...]).astype(BF16)
    e = e_ref[...]
    cf, s1, s2 = cf_ref[...], s1_ref[...], s2_ref[...]
    kvd = N_KV * HEAD_DIM
    lo = lax.broadcasted_iota(jnp.int32, (h.shape[0], LANES), 1) < HEAD_DIM

    def proj(c0, width):
        return jnp.dot(h, w_ref[:, c0:c0 + width], preferred_element_type=F32)

    def head_norm_rope(blk, gain, out_scale):
        ss = jnp.dot((blk * blk).astype(BF16), e, preferred_element_type=F32)
        n = blk * lax.rsqrt(ss * (1.0 / HEAD_DIM) + EPS) * gain
        outs = []
        for u in range(MXU_DIM // LANES):
            xb = n[:, u * LANES:(u + 1) * LANES]
            r = (xb * cf + pltpu.roll(xb, LANES - ROT_HALF, axis=1) * s1
                 + pltpu.roll(xb, ROT_HALF, axis=1) * s2)
            outs.append(r * out_scale if out_scale != 1.0 else r)
        return outs

    def store_duplicated(ref, halves):
        for u, xb in enumerate(halves):
            xr = pltpu.roll(xb, HEAD_DIM, axis=1)
            ref[:, (2 * u) * LANES:(2 * u + 1) * LANES] = jnp.where(lo, xb, xr).astype(BF16)
            ref[:, (2 * u + 1) * LANES:(2 * u + 2) * LANES] = jnp.where(lo, xr, xb).astype(BF16)

    pw = 2 * MXU_DIM
    for c in range(d // pw):
        blk = proj(c * pw, pw)
        outs = (head_norm_rope(blk[:, :MXU_DIM], qg_ref[...], scale)
                + head_norm_rope(blk[:, MXU_DIM:], qg_ref[...], scale))
        q_ref[:, c * pw:(c + 1) * pw] = jnp.concatenate(outs, axis=1).astype(BF16)
    assert kvd == MXU_DIM
    kv = proj(d, 2 * kvd)
    store_duplicated(k_ref, head_norm_rope(kv[:, :kvd], kg_ref[...], 1.0))
    store_duplicated(v_ref, [kv[:, kvd + u * LANES:kvd + (u + 1) * LANES] for u in range(kvd // LANES)])


def _qkv(x2d, gain, w, e, qg, kg, cf, s1, s2, seq, tm):
    t, d = x2d.shape
    kd = N_KV * LANES
    nt = seq // tm
    tab_spec = pl.BlockSpec((tm, LANES), lambda i: (i % nt, 0))
    const = lambda shape: pl.BlockSpec(shape, lambda i: (0, 0))
    return pl.pallas_call(
        functools.partial(_qkv_kernel, d=d, scale=LOG2E / math.sqrt(HEAD_DIM)),
        grid=(t // tm,),
        in_specs=[
            pl.BlockSpec((tm, d), lambda i: (i, 0)),
            const((1, d)),
            pl.BlockSpec((None,) + w.shape[1:], lambda i: (0, 0, 0)),
            const((MXU_DIM, MXU_DIM)),
            const((1, MXU_DIM)),
            const((1, MXU_DIM)),
            tab_spec, tab_spec, tab_spec,
        ],
        out_specs=[
            pl.BlockSpec((tm, d), lambda i: (i, 0)),
            pl.BlockSpec((tm, kd), lambda i: (i, 0)),
            pl.BlockSpec((tm, kd), lambda i: (i, 0)),
        ],
        out_shape=[
            jax.ShapeDtypeStruct((t, d), BF16),
            jax.ShapeDtypeStruct((t, kd), BF16),
            jax.ShapeDtypeStruct((t, kd), BF16),
        ],
        compiler_params=_cparams(("parallel",)),
        name="qkv_norm_rope",
    )(x2d, gain, w, e, qg, kg, cf, s1, s2)


def _attn_kernel(sink_ref, q_ref, kp_ref, kc_ref, kn_ref, vp_ref, vc_ref, vn_ref, x_ref, w_ref, g_ref,
                 x1_ref, h_ref, o_scr, *, nb, nsub):
    j = pl.program_id(1)
    bq = WINDOW_BLOCK
    npair = GROUP // 2
    nq = GROUP * bq
    lo = lax.broadcasted_iota(jnp.int32, (bq, LANES), 1) < HEAD_DIM
    key = lax.broadcasted_iota(jnp.int32, (bq, nq), 0)
    qry = lax.broadcasted_iota(jnp.int32, (bq, nq), 1) % bq
    zero = jnp.zeros((bq, LANES), BF16)
    xr = BF16_ROWS
    lo_x = lax.broadcasted_iota(jnp.int32, (3 * bq + xr, LANES), 1) < HEAD_DIM
    sink_row = lax.broadcasted_iota(jnp.int32, (xr, nq), 0) == 0
    vzero = jnp.zeros((xr, LANES), BF16)
    nt = (((1,), (1,)), ((), ()))
    tn = (((0,), (0,)), ((), ()))

    def kv_blocks(prev_ref, cur_ref, next_ref, u, lanes):
        rows = lambda t: slice(t * bq, (t + 1) * bq)
        before = prev_ref[0, :, lanes] if u == 0 else cur_ref[0, rows(u - 1), lanes]
        after = next_ref[0, :, lanes] if u == nsub - 1 else cur_ref[0, rows(u + 1), lanes]
        return [before, cur_ref[0, rows(u), lanes], after]

    for u in range(nsub):
        blk_idx = j * nsub + u
        vprev = (key >= qry) & (blk_idx > 0)
        vnext = (key <= qry) & (blk_idx < nb - 1)
        qrows = slice(u * bq, (u + 1) * bq)
        for h in range(N_KV):
            ks = slice(h * LANES, (h + 1) * LANES)
            k2 = jnp.concatenate(kv_blocks(kp_ref, kc_ref, kn_ref, u, ks), axis=0)
            v2x = jnp.concatenate(kv_blocks(vp_ref, vc_ref, vn_ref, u, ks) + [vzero], axis=0)
            qparts, sinks = [], []
            for parity in range(2):
                for p in range(npair):
                    c0 = (h * npair + p) * LANES
                    blk = q_ref[0, qrows, c0:c0 + LANES]
                    qparts.append(jnp.where(lo if parity == 0 else ~lo, blk, zero))
                    sinks.append(jnp.full((1, bq), sink_ref[h * GROUP + 2 * p + parity] * LOG2E, F32))
            qs = jnp.concatenate(qparts, axis=0)
            sink = jnp.concatenate(sinks, axis=1)
            st = lax.dot_general(k2, qs, nt, preferred_element_type=F32)
            sp = jnp.where(vprev, st[:bq], NEG)
            sc = st[bq:2 * bq]
            sn = jnp.where(vnext, st[2 * bq:], NEG)
            m = jnp.max(jnp.maximum(jnp.maximum(sp, sc), sn), axis=0, keepdims=True)
            m = jnp.maximum(m, sink)
            es = jnp.where(sink_row, jnp.exp2(sink - m), 0.0)
            pt = jnp.concatenate(
                [jnp.exp2(sp - m), jnp.exp2(sc - m), jnp.exp2(sn - m), es], axis=0).astype(BF16)
            ot = lax.dot_general(jnp.where(lo_x, v2x, 1.0), pt, tn, preferred_element_type=F32)
            res = ot[:HEAD_DIM] * (1.0 / ot[HEAD_DIM:HEAD_DIM + 1])
            half = nq // 2
            for p in range(npair):
                pair = jnp.concatenate(
                    [res[:, p * bq:(p + 1) * bq], res[:, half + p * bq:half + (p + 1) * bq]], axis=0)
                c0 = (h * npair + p) * LANES
                o_scr[qrows, c0:c0 + LANES] = pair.T.astype(BF16)

    x1 = x_ref[0] + jnp.dot(o_scr[...], w_ref[...], preferred_element_type=F32)
    x1_ref[0] = x1
    h_ref[0] = _rms_rows(x1, g_ref[...]).astype(BF16)


def _attention(sink, q, k, v, x, w_o, gain):
    b, s, d = q.shape
    kd = k.shape[-1]
    bq = WINDOW_BLOCK
    nb = s // bq
    nsub = ATTN_SUB if nb % ATTN_SUB == 0 else 1
    prev = lambda bi, j: (bi, jnp.maximum(j * nsub - 1, 0), 0)
    cur = lambda bi, j: (bi, j, 0)
    nxt = lambda bi, j: (bi, jnp.minimum((j + 1) * nsub, nb - 1), 0)
    edge = lambda im: pl.BlockSpec((1, bq, kd), im)
    own = pl.BlockSpec((1, nsub * bq, kd), cur)
    tile = pl.BlockSpec((1, nsub * bq, d), cur)
    return pl.pallas_call(
        functools.partial(_attn_kernel, nb=nb, nsub=nsub),
        grid=(b, nb // nsub),
        in_specs=[
            pl.BlockSpec(memory_space=pltpu.SMEM),
            tile,
            edge(prev), own, edge(nxt), edge(prev), own, edge(nxt),
            tile,
            pl.BlockSpec((None, d, d), lambda bi, j: (0, 0, 0)),
            pl.BlockSpec((1, d), lambda bi, j: (0, 0)),
        ],
        out_specs=[tile, tile],
        out_shape=[jax.ShapeDtypeStruct((b, s, d), F32), jax.ShapeDtypeStruct((b, s, d), BF16)],
        scratch_shapes=[pltpu.VMEM((nsub * bq, d), BF16)],
        compiler_params=_cparams(("parallel", "parallel")),
        name="window_attention",
    )(sink, q, k, k, k, v, v, v, x, w_o, gain)


def _chan_table(gd):
    k = np.arange(gd)
    ang = 2.0 * np.pi * ((k[:, None] * k[None, :]) % gd) / gd
    sc = 1.0 / math.sqrt(gd)
    return jnp.asarray(np.concatenate([np.cos(ang) * sc, -np.sin(ang) * sc], axis=1), BF16)


def _stage1_table(n1, n2):
    n = n1 * n2
    k1 = np.arange(n1)
    tw = 2.0 * np.pi * ((np.arange(n2)[:, None] * k1[None, :]) % n) / n
    dft = 2.0 * np.pi * ((k1[:, None] * k1[None, :]) % n1) / n1
    ct, st = jnp.asarray(np.cos(tw), F32)[:, :, None], jnp.asarray(np.sin(tw), F32)[:, :, None]
    cd, sd = jnp.asarray(np.cos(dft), F32)[None], jnp.asarray(np.sin(dft), F32)[None]
    c, s = ct * cd - st * sd, st * cd + ct * sd
    top = jnp.concatenate([c, s], axis=-1)
    bot = jnp.concatenate([-s, c], axis=-1)
    return jnp.concatenate([top, bot], axis=-2).astype(BF16)


def _stage2_table(n1, n2):
    k = np.arange(n2)
    ang = 2.0 * np.pi * ((k[:, None] * k[None, :]) % n2) / n2
    sc = 1.0 / math.sqrt(n1 * n2)
    return jnp.asarray(np.concatenate([np.cos(ang) * sc, np.sin(ang) * sc], axis=1), BF16)


def _rope_lane_tables(s):
    inv_freq = ROPE_THETA ** (-jnp.arange(0, ROT_DIM, 2, dtype=F32) / ROT_DIM)
    ang = jnp.arange(s, dtype=F32)[:, None] * inv_freq[None, :]
    cos, sin = jnp.cos(ang), jnp.sin(ang)
    pad = HEAD_DIM - ROT_DIM
    ones = jnp.ones((s, pad), F32)
    zeros = jnp.zeros((s, pad), F32)
    zh = jnp.zeros((s, ROT_HALF), F32)
    cf = jnp.concatenate([cos, cos, ones], axis=1)
    s1 = jnp.concatenate([-sin, zh, zeros], axis=1)
    s2 = jnp.concatenate([zh, sin, zeros], axis=1)
    rep = LANES // HEAD_DIM
    return tuple(jnp.tile(a, (1, rep)) for a in (cf, s1, s2))


def _segment_ones():
    i = np.arange(MXU_DIM) // HEAD_DIM
    return jnp.asarray((i[:, None] == i[None, :]).astype(np.float32), BF16)


def _fourier_mixer(x, gain, w_out, mlp_gain, tm):
    b, s, d = x.shape
    gd = d // FFT_GROUPS
    n1, n2 = _seq_split(s)
    y = _fourier_stage1(x.reshape(b, n1, n2, d), gain, _chan_table(gd), _stage1_table(n1, n2))
    f = _fourier_stage2(y, _stage2_table(n1, n2))
    return _proj_norm(f.reshape(b * s, d), x.reshape(b * s, d), w_out, mlp_gain, tm)


def _attn_mixer(x2d, b, s, gain, w_qkv, q_gain, k_gain, sink, w_o, mlp_gain, tm):
    t, d = x2d.shape
    rep = MXU_DIM // HEAD_DIM
    cf, s1, s2 = _rope_lane_tables(s)
    q, k, v = _qkv(x2d, gain, w_qkv, _segment_ones(),
                   jnp.tile(q_gain, rep)[None, :], jnp.tile(k_gain, rep)[None, :],
                   cf, s1, s2, s, tm)
    kd = k.shape[-1]
    x1, h = _attention(sink, q.reshape(b, s, d), k.reshape(b, s, kd), v.reshape(b, s, kd),
                       x2d.reshape(b, s, d), w_o, mlp_gain)
    return x1.reshape(t, d), h.reshape(t, d)


def _trunk(x, p, tm, tm_mlp, tf):
    b, s, d = x.shape
    x1, h = _fourier_mixer(x, p["fourier_norm"][0][None, :], p["fourier_w_out"],
                           p["mlp_norm"][0][None, :], tm)
    x2 = _mlp(h, x1, p["mlp_w_up"], p["mlp_w_down"], 0, tm_mlp, tf)
    x3, h = _attn_mixer(x2, b, s, p["attn_norm"][0][None, :], p["attn_w_qkv"], p["attn_q_norm"][0],
                        p["attn_k_norm"][0], p["attn_sink"][0], p["attn_w_o"],
                        p["mlp_norm"][1][None, :], tm)
    x4 = _mlp(h, x3, p["mlp_w_up"], p["mlp_w_down"], 1, tm_mlp, tf)
    return x4.reshape(b, s, d)


def kernel(x_prompt, x_sample, fourier_norm, fourier_w_out, attn_norm, attn_w_qkv, attn_q_norm,
           attn_k_norm, attn_sink, attn_w_o, mlp_norm, mlp_w_up, mlp_w_down):
    p = dict(
        fourier_norm=fourier_norm, fourier_w_out=fourier_w_out.astype(BF16),
        attn_norm=attn_norm, attn_w_qkv=attn_w_qkv.astype(BF16), attn_q_norm=attn_q_norm,
        attn_k_norm=attn_k_norm, attn_sink=attn_sink, attn_w_o=attn_w_o.astype(BF16),
        mlp_norm=mlp_norm, mlp_w_up=mlp_w_up.astype(BF16), mlp_w_down=mlp_w_down.astype(BF16),
    )
    outs = []
    for x in (x_prompt, x_sample):
        t = x.shape[0] * x.shape[1]
        tm = min(512, x.shape[1])
        tm_mlp = min(1024, t)
        tf = min(1024, mlp_w_up.shape[-1] // 2)
        outs.append(_trunk(x, p, tm, tm_mlp, tf))
    return tuple(outs)
```

```python
import functools
import math

import numpy as np
import jax
import jax.numpy as jnp
from jax import lax
from jax.experimental import pallas as pl
from jax.experimental.pallas import tpu as pltpu

F32 = jnp.float32
BF16 = jnp.bfloat16

FFT_GROUPS = 8
HEAD_DIM = 64
N_KV = 4
GROUP = 8
WINDOW_BLOCK = 128
ROPE_THETA = 500000.0
ROT_DIM = HEAD_DIM // 4
ROT_HALF = ROT_DIM // 2
EPS = 1e-6
NEG = -1e30
LOG2E = math.log2(math.e)
ATTN_SUB = 4
MLP_CHUNKS_PER_ITER = 4
LANES = 128
MXU_DIM = 256
VMEM_LIMIT = 60 * 1024 * 1024


def _cparams(sem):
    return pltpu.CompilerParams(dimension_semantics=sem, vmem_limit_bytes=VMEM_LIMIT)


def _rms_rows(x, gain):
    ms = jnp.mean(x * x, axis=-1, keepdims=True)
    return x * lax.rsqrt(ms + EPS) * gain


def _seq_split(s):
    n1 = 1 << ((s.bit_length() - 1 + 1) // 2)
    n2 = s // n1
    assert n1 * n2 == s
    return n1, n2


SUBLANES = 8
BF16_ROWS = 2 * SUBLANES
STAGE1_GROUPS = 2


def _pack_complex(re, im):
    rb = lax.bitcast_convert_type(re.astype(BF16).astype(F32), jnp.uint32)
    ib = lax.bitcast_convert_type(im.astype(BF16).astype(F32), jnp.uint32)
    return rb | (ib >> 16)


def _unpack_complex(u):
    re = lax.bitcast_convert_type(u & jnp.uint32(0xFFFF0000), F32)
    im = lax.bitcast_convert_type(u << 16, F32)
    return re, im


def _fourier_stage1_kernel(x_ref, g_ref, cs_ref, tab_ref, o_ref, zs_ref, *, gd):
    n1 = x_ref.shape[1]
    rows = n1 * SUBLANES
    d = x_ref.shape[3]
    nblk = 2 * gd // LANES
    h = _rms_rows(x_ref[0].reshape(rows, d), g_ref[...]).astype(BF16)
    cs = cs_ref[...]
    gp = STAGE1_GROUPS
    for g0 in range(0, FFT_GROUPS, gp):
        slot = (g0 // gp) % 2
        for gi in range(gp):
            g = g0 + gi
            r = jnp.dot(h[:, g * gd:(g + 1) * gd], cs, preferred_element_type=F32)
            for c in range(nblk):
                zs_ref[slot, gi * nblk + c] = r[:, c * LANES:(c + 1) * LANES]
        for j in range(SUBLANES):
            parts = [zs_ref[slot, c, pl.ds(j, n1, stride=SUBLANES), :] for c in range(gp * nblk)]
            zr = jnp.concatenate(
                [parts[gi * nblk + c] for gi in range(gp) for c in range(nblk // 2)], axis=1)
            zi = jnp.concatenate(
                [parts[gi * nblk + nblk // 2 + c] for gi in range(gp) for c in range(nblk // 2)], axis=1)
            zc = jnp.concatenate([zr, zi], axis=0).astype(BF16)
            y = jnp.dot(tab_ref[j], zc, preferred_element_type=F32)
            o_ref[0, j, :, g0 * gd:(g0 + gp) * gd] = _pack_complex(y[:n1], y[n1:])


def _fourier_stage1(x4, gain, cs, tab):
    b, n1, n2, d = x4.shape
    gd = d // FFT_GROUPS
    rows = n1 * SUBLANES
    return pl.pallas_call(
        functools.partial(_fourier_stage1_kernel, gd=gd),
        grid=(b, n2 // SUBLANES),
        in_specs=[
            pl.BlockSpec((1, n1, SUBLANES, d), lambda bi, j: (bi, 0, j, 0)),
            pl.BlockSpec((1, d), lambda bi, j: (0, 0)),
            pl.BlockSpec((gd, 2 * gd), lambda bi, j: (0, 0)),
            pl.BlockSpec((SUBLANES, 2 * n1, 2 * n1), lambda bi, j: (j, 0, 0)),
        ],
        out_specs=pl.BlockSpec((1, SUBLANES, n1, d), lambda bi, j: (bi, j, 0, 0)),
        out_shape=jax.ShapeDtypeStruct((b, n2, n1, d), jnp.uint32),
        scratch_shapes=[pltpu.VMEM((2, STAGE1_GROUPS * 2 * gd // LANES, rows, LANES), F32)],
        compiler_params=_cparams(("parallel", "parallel")),
        name="fourier_stage1",
    )(x4, gain, cs, tab)


def _fourier_stage2_kernel(y_ref, tab_ref, o_ref, ys_ref, fs_ref, *, cw):
    n2 = y_ref.shape[1]
    rows = n2 * SUBLANES
    d = y_ref.shape[3]
    nblk = cw // LANES
    tab = tab_ref[...]
    for ch in range(d // cw):
        slot = ch % 2
        for c in range(nblk):
            lanes = slice(ch * cw + c * LANES, ch * cw + (c + 1) * LANES)
            ys_ref[slot, c] = y_ref[0, :, :, lanes].reshape(rows, LANES)
        for k in range(SUBLANES):
            u = jnp.concatenate(
                [ys_ref[slot, c, pl.ds(k, n2, stride=SUBLANES), :] for c in range(nblk)], axis=1)
            re, im = _unpack_complex(u)
            zc = jnp.concatenate([re, im], axis=0).astype(BF16)
            f = jnp.dot(tab, zc, preferred_element_type=F32)
            for c in range(nblk):
                fs_ref[slot, c, pl.ds(k, n2, stride=SUBLANES), :] = f[:, c * LANES:(c + 1) * LANES]
        for c in range(nblk):
            lanes = slice(ch * cw + c * LANES, ch * cw + (c + 1) * LANES)
            o_ref[0, :, :, lanes] = fs_ref[slot, c].reshape(n2, SUBLANES, LANES)


def _fourier_stage2(y, tab):
    b, n2, n1, d = y.shape
    cw = 2 * MXU_DIM
    rows = n2 * SUBLANES
    return pl.pallas_call(
        functools.partial(_fourier_stage2_kernel, cw=cw),
        grid=(b, n1 // SUBLANES),
        in_specs=[
            pl.BlockSpec((1, n2, SUBLANES, d), lambda bi, k: (bi, 0, k, 0)),
            pl.BlockSpec((n2, 2 * n2), lambda bi, k: (0, 0)),
        ],
        out_specs=pl.BlockSpec((1, n2, SUBLANES, d), lambda bi, k: (bi, 0, k, 0)),
        out_shape=jax.ShapeDtypeStruct((b, n2, n1, d), F32),
        scratch_shapes=[pltpu.VMEM((2, cw // LANES, rows, LANES), jnp.uint32),
                        pltpu.VMEM((2, cw // LANES, rows, LANES), F32)],
        compiler_params=_cparams(("parallel", "parallel")),
        name="fourier_stage2",
    )(y, tab)


def _proj_norm_kernel(a_ref, x_ref, w_ref, g_ref, x1_ref, h_ref):
    x1 = x_ref[...] + jnp.dot(a_ref[...].astype(BF16), w_ref[...], preferred_element_type=F32)
    x1_ref[...] = x1
    h_ref[...] = _rms_rows(x1, g_ref[...]).astype(BF16)


def _proj_norm(a, x2d, w, gain, tm):
    t, d = x2d.shape
    return pl.pallas_call(
        _proj_norm_kernel,
        grid=(t // tm,),
        in_specs=[
            pl.BlockSpec((tm, d), lambda i: (i, 0)),
            pl.BlockSpec((tm, d), lambda i: (i, 0)),
            pl.BlockSpec((None, d, d), lambda i: (0, 0, 0)),
            pl.BlockSpec((1, d), lambda i: (0, 0)),
        ],
        out_specs=[
            pl.BlockSpec((tm, d), lambda i: (i, 0)),
            pl.BlockSpec((tm, d), lambda i: (i, 0)),
        ],
        out_shape=[jax.ShapeDtypeStruct((t, d), F32), jax.ShapeDtypeStruct((t, d), BF16)],
        compiler_params=_cparams(("parallel",)),
        name="proj_norm",
    )(a, x2d, w, gain)


def _mlp_kernel(h_ref, x_ref, wu_hbm, wd_hbm, o_ref, wu_buf, wd_buf, sem, *, layer, tf, ts):
    i = pl.program_id(0)
    nchunk = wu_hbm.shape[-1] // tf
    assert nchunk % 2 == 0

    def chunk_copies(j, slot):
        return (
            pltpu.make_async_copy(wu_hbm.at[layer, :, pl.ds(j * tf, tf)], wu_buf.at[slot], sem.at[0, slot]),
            pltpu.make_async_copy(wd_hbm.at[layer, pl.ds(j * tf, tf), :], wd_buf.at[slot], sem.at[1, slot]),
        )

    def start(j, slot):
        for c in chunk_copies(j, slot):
            c.start()

    @pl.when(i == 0)
    def _():
        start(0, 0)

    o_ref[...] = x_ref[...]
    per_iter = min(MLP_CHUNKS_PER_ITER, nchunk)
    niter = nchunk // per_iter
    assert per_iter % 2 == 0 and niter * per_iter == nchunk

    def body(jj, carry):
        for u in range(per_iter):
            j = per_iter * jj + u
            slot = u % 2
            for c in chunk_copies(j, slot):
                c.wait()
            if u + 1 < per_iter:
                start(j + 1, 1 - slot)
            else:
                @pl.when(jj + 1 < niter)
                def _():
                    start(j + 1, 0)

                @pl.when(jnp.logical_and(jj + 1 == niter, i + 1 < pl.num_programs(0)))
                def _():
                    start(0, 0)
            for s in range(tf // ts):
                a = jnp.maximum(
                    jnp.dot(h_ref[...], wu_buf[slot, :, s * ts:(s + 1) * ts], preferred_element_type=F32),
                    0.0)
                o_ref[...] += jnp.dot((a * a).astype(BF16), wd_buf[slot, s * ts:(s + 1) * ts, :],
                                      preferred_element_type=F32)
        return carry

    lax.fori_loop(0, niter, body, 0)


def _mlp(h, x1, w_up, w_down, layer, tm, tf):
    t, d = x1.shape
    ts = min(tf, 2 * MXU_DIM)
    return pl.pallas_call(
        functools.partial(_mlp_kernel, layer=layer, tf=tf, ts=ts),
        grid=(t // tm,),
        in_specs=[
            pl.BlockSpec((tm, d), lambda i: (i, 0)),
            pl.BlockSpec((tm, d), lambda i: (i, 0)),
            pl.BlockSpec(memory_space=pl.ANY),
            pl.BlockSpec(memory_space=pl.ANY),
        ],
        out_specs=pl.BlockSpec((tm, d), lambda i: (i, 0)),
        out_shape=jax.ShapeDtypeStruct((t, d), F32),
        scratch_shapes=[
            pltpu.VMEM((2, d, tf), BF16),
            pltpu.VMEM((2, tf, d), BF16),
            pltpu.SemaphoreType.DMA((2, 2)),
        ],
        compiler_params=_cparams(("arbitrary",)),
        name="sqrelu_mlp",
    )(h, x1, w_up, w_down)


def _qkv_kernel(x_ref, g_ref, w_ref, e_ref, qg_ref, kg_ref, cf_ref, s1_ref, s2_ref,
                q_ref, k_ref, v_ref, *, d, scale):
    h = _rms_rows(x_ref[...], g_ref[...]).astype(BF16)
    e = e_ref[...]
    cf, s1, s2 = cf_ref[...], s1_ref[...], s2_ref[...]
    kvd = N_KV * HEAD_DIM
    lo = lax.broadcasted_iota(jnp.int32, (h.shape[0], LANES), 1) < HEAD_DIM

    def proj(c0, width):
        return jnp.dot(h, w_ref[:, c0:c0 + width], preferred_element_type=F32)

    def head_norm_rope(blk, gain, out_scale):
        ss = jnp.dot((blk * blk).astype(BF16), e, preferred_element_type=F32)
        n = blk * lax.rsqrt(ss * (1.0 / HEAD_DIM) + EPS) * gain
        outs = []
        for u in range(MXU_DIM // LANES):
            xb = n[:, u * LANES:(u + 1) * LANES]
            r = (xb * cf + pltpu.roll(xb, LANES - ROT_HALF, axis=1) * s1
                 + pltpu.roll(xb, ROT_HALF, axis=1) * s2)
            outs.append(r * out_scale if out_scale != 1.0 else r)
        return outs

    def store_duplicated(ref, halves):
        for u, xb in enumerate(halves):
            xr = pltpu.roll(xb, HEAD_DIM, axis=1)
            ref[:, (2 * u) * LANES:(2 * u + 1) * LANES] = jnp.where(lo, xb, xr).astype(BF16)
            ref[:, (2 * u + 1) * LANES:(2 * u + 2) * LANES] = jnp.where(lo, xr, xb).astype(BF16)

    pw = 2 * MXU_DIM
    for c in range(d // pw):
        blk = proj(c * pw, pw)
        outs = (head_norm_rope(blk[:, :MXU_DIM], qg_ref[...], scale)
                + head_norm_rope(blk[:, MXU_DIM:], qg_ref[...], scale))
        q_ref[:, c * pw:(c + 1) * pw] = jnp.concatenate(outs, axis=1).astype(BF16)
    assert kvd == MXU_DIM
    kv = proj(d, 2 * kvd)
    store_duplicated(k_ref, head_norm_rope(kv[:, :kvd], kg_ref[...], 1.0))
    store_duplicated(v_ref, [kv[:, kvd + u * LANES:kvd + (u + 1) * LANES] for u in range(kvd // LANES)])


def _qkv(x2d, gain, w, e, qg, kg, cf, s1, s2, seq, tm):
    t, d = x2d.shape
    kd = N_KV * LANES
    nt = seq // tm
    tab_spec = pl.BlockSpec((tm, LANES), lambda i: (i % nt, 0))
    const = lambda shape: pl.BlockSpec(shape, lambda i: (0, 0))
    return pl.pallas_call(
        functools.partial(_qkv_kernel, d=d, scale=LOG2E / math.sqrt(HEAD_DIM)),
        grid=(t // tm,),
        in_specs=[
            pl.BlockSpec((tm, d), lambda i: (i, 0)),
            const((1, d)),
            pl.BlockSpec((None,) + w.shape[1:], lambda i: (0, 0, 0)),
            const((MXU_DIM, MXU_DIM)),
            const((1, MXU_DIM)),
            const((1, MXU_DIM)),
            tab_spec, tab_spec, tab_spec,
        ],
        out_specs=[
            pl.BlockSpec((tm, d), lambda i: (i, 0)),
            pl.BlockSpec((tm, kd), lambda i: (i, 0)),
            pl.BlockSpec((tm, kd), lambda i: (i, 0)),
        ],
        out_shape=[
            jax.ShapeDtypeStruct((t, d), BF16),
            jax.ShapeDtypeStruct((t, kd), BF16),
            jax.ShapeDtypeStruct((t, kd), BF16),
        ],
        compiler_params=_cparams(("parallel",)),
        name="qkv_norm_rope",
    )(x2d, gain, w, e, qg, kg, cf, s1, s2)


def _attn_kernel(sink_ref, q_ref, kp_ref, kc_ref, kn_ref, vp_ref, vc_ref, vn_ref, x_ref, w_ref, g_ref,
                 x1_ref, h_ref, o_scr, *, nb, nsub):
    j = pl.program_id(1)
    bq = WINDOW_BLOCK
    npair = GROUP // 2
    nq = GROUP * bq
    lo = lax.broadcasted_iota(jnp.int32, (bq, LANES), 1) < HEAD_DIM
    key = lax.broadcasted_iota(jnp.int32, (bq, nq), 0)
    qry = lax.broadcasted_iota(jnp.int32, (bq, nq), 1) % bq
    zero = jnp.zeros((bq, LANES), BF16)
    xr = BF16_ROWS
    lo_x = lax.broadcasted_iota(jnp.int32, (3 * bq + xr, LANES), 1) < HEAD_DIM
    sink_row = lax.broadcasted_iota(jnp.int32, (xr, nq), 0) == 0
    vzero = jnp.zeros((xr, LANES), BF16)
    nt = (((1,), (1,)), ((), ()))
    tn = (((0,), (0,)), ((), ()))

    def kv_blocks(prev_ref, cur_ref, next_ref, u, lanes):
        rows = lambda t: slice(t * bq, (t + 1) * bq)
        before = prev_ref[0, :, lanes] if u == 0 else cur_ref[0, rows(u - 1), lanes]
        after = next_ref[0, :, lanes] if u == nsub - 1 else cur_ref[0, rows(u + 1), lanes]
        return [before, cur_ref[0, rows(u), lanes], after]

    for u in range(nsub):
        blk_idx = j * nsub + u
        vprev = (key >= qry) & (blk_idx > 0)
        vnext = (key <= qry) & (blk_idx < nb - 1)
        qrows = slice(u * bq, (u + 1) * bq)
        for h in range(N_KV):
            ks = slice(h * LANES, (h + 1) * LANES)
            k2 = jnp.concatenate(kv_blocks(kp_ref, kc_ref, kn_ref, u, ks), axis=0)
            v2x = jnp.concatenate(kv_blocks(vp_ref, vc_ref, vn_ref, u, ks) + [vzero], axis=0)
            qparts, sinks = [], []
            for parity in range(2):
                for p in range(npair):
                    c0 = (h * npair + p) * LANES
                    blk = q_ref[0, qrows, c0:c0 + LANES]
                    qparts.append(jnp.where(lo if parity == 0 else ~lo, blk, zero))
                    sinks.append(jnp.full((1, bq), sink_ref[h * GROUP + 2 * p + parity] * LOG2E, F32))
            qs = jnp.concatenate(qparts, axis=0)
            sink = jnp.concatenate(sinks, axis=1)
            st = lax.dot_general(k2, qs, nt, preferred_element_type=F32)
            sp = jnp.where(vprev, st[:bq], NEG)
            sc = st[bq:2 * bq]
            sn = jnp.where(vnext, st[2 * bq:], NEG)
            m = jnp.max(jnp.maximum(jnp.maximum(sp, sc), sn), axis=0, keepdims=True)
            m = jnp.maximum(m, sink)
            es = jnp.where(sink_row, jnp.exp2(sink - m), 0.0)
            pt = jnp.concatenate(
                [jnp.exp2(sp - m), jnp.exp2(sc - m), jnp.exp2(sn - m), es], axis=0).astype(BF16)
            ot = lax.dot_general(jnp.where(lo_x, v2x, 1.0), pt, tn, preferred_element_type=F32)
            res = ot[:HEAD_DIM] * (1.0 / ot[HEAD_DIM:HEAD_DIM + 1])
            half = nq // 2
            for p in range(npair):
                pair = jnp.concatenate(
                    [res[:, p * bq:(p + 1) * bq], res[:, half + p * bq:half + (p + 1) * bq]], axis=0)
                c0 = (h * npair + p) * LANES
                o_scr[qrows, c0:c0 + LANES] = pair.T.astype(BF16)

    x1 = x_ref[0] + jnp.dot(o_scr[...], w_ref[...], preferred_element_type=F32)
    x1_ref[0] = x1
    h_ref[0] = _rms_rows(x1, g_ref[...]).astype(BF16)


def _attention(sink, q, k, v, x, w_o, gain):
    b, s, d = q.shape
    kd = k.shape[-1]
    bq = WINDOW_BLOCK
    nb = s // bq
    nsub = ATTN_SUB if nb % ATTN_SUB == 0 else 1
    prev = lambda bi, j: (bi, jnp.maximum(j * nsub - 1, 0), 0)
    cur = lambda bi, j: (bi, j, 0)
    nxt = lambda bi, j: (bi, jnp.minimum((j + 1) * nsub, nb - 1), 0)
    edge = lambda im: pl.BlockSpec((1, bq, kd), im)
    own = pl.BlockSpec((1, nsub * bq, kd), cur)
    tile = pl.BlockSpec((1, nsub * bq, d), cur)
    return pl.pallas_call(
        functools.partial(_attn_kernel, nb=nb, nsub=nsub),
        grid=(b, nb // nsub),
        in_specs=[
            pl.BlockSpec(memory_space=pltpu.SMEM),
            tile,
            edge(prev), own, edge(nxt), edge(prev), own, edge(nxt),
            tile,
            pl.BlockSpec((None, d, d), lambda bi, j: (0, 0, 0)),
            pl.BlockSpec((1, d), lambda bi, j: (0, 0)),
        ],
        out_specs=[tile, tile],
        out_shape=[jax.ShapeDtypeStruct((b, s, d), F32), jax.ShapeDtypeStruct((b, s, d), BF16)],
        scratch_shapes=[pltpu.VMEM((nsub * bq, d), BF16)],
        compiler_params=_cparams(("parallel", "parallel")),
        name="window_attention",
    )(sink, q, k, k, k, v, v, v, x, w_o, gain)


def _chan_table(gd):
    k = np.arange(gd)
    ang = 2.0 * np.pi * ((k[:, None] * k[None, :]) % gd) / gd
    sc = 1.0 / math.sqrt(gd)
    return jnp.asarray(np.concatenate([np.cos(ang) * sc, -np.sin(ang) * sc], axis=1), BF16)


def _stage1_table(n1, n2):
    n = n1 * n2
    k1 = np.arange(n1)
    tw = 2.0 * np.pi * ((np.arange(n2)[:, None] * k1[None, :]) % n) / n
    dft = 2.0 * np.pi * ((k1[:, None] * k1[None, :]) % n1) / n1
    ct, st = jnp.asarray(np.cos(tw), F32)[:, :, None], jnp.asarray(np.sin(tw), F32)[:, :, None]
    cd, sd = jnp.asarray(np.cos(dft), F32)[None], jnp.asarray(np.sin(dft), F32)[None]
    c, s = ct * cd - st * sd, st * cd + ct * sd
    top = jnp.concatenate([c, s], axis=-1)
    bot = jnp.concatenate([-s, c], axis=-1)
    return jnp.concatenate([top, bot], axis=-2).astype(BF16)


def _stage2_table(n1, n2):
    k = np.arange(n2)
    ang = 2.0 * np.pi * ((k[:, None] * k[None, :]) % n2) / n2
    sc = 1.0 / math.sqrt(n1 * n2)
    return jnp.asarray(np.concatenate([np.cos(ang) * sc, np.sin(ang) * sc], axis=1), BF16)


def _rope_lane_tables(s):
    inv_freq = ROPE_THETA ** (-jnp.arange(0, ROT_DIM, 2, dtype=F32) / ROT_DIM)
    ang = jnp.arange(s, dtype=F32)[:, None] * inv_freq[None, :]
    cos, sin = jnp.cos(ang), jnp.sin(ang)
    pad = HEAD_DIM - ROT_DIM
    ones = jnp.ones((s, pad), F32)
    zeros = jnp.zeros((s, pad), F32)
    zh = jnp.zeros((s, ROT_HALF), F32)
    cf = jnp.concatenate([cos, cos, ones], axis=1)
    s1 = jnp.concatenate([-sin, zh, zeros], axis=1)
    s2 = jnp.concatenate([zh, sin, zeros], axis=1)
    rep = LANES // HEAD_DIM
    return tuple(jnp.tile(a, (1, rep)) for a in (cf, s1, s2))


def _segment_ones():
    i = np.arange(MXU_DIM) // HEAD_DIM
    return jnp.asarray((i[:, None] == i[None, :]).astype(np.float32), BF16)


def _fourier_mixer(x, gain, w_out, mlp_gain, tm):
    b, s, d = x.shape
    gd = d // FFT_GROUPS
    n1, n2 = _seq_split(s)
    y = _fourier_stage1(x.reshape(b, n1, n2, d), gain, _chan_table(gd), _stage1_table(n1, n2))
    f = _fourier_stage2(y, _stage2_table(n1, n2))
    return _proj_norm(f.reshape(b * s, d), x.reshape(b * s, d), w_out, mlp_gain, tm)


def _attn_mixer(x2d, b, s, gain, w_qkv, q_gain, k_gain, sink, w_o, mlp_gain, tm):
    t, d = x2d.shape
    rep = MXU_DIM // HEAD_DIM
    cf, s1, s2 = _rope_lane_tables(s)
    q, k, v = _qkv(x2d, gain, w_qkv, _segment_ones(),
                   jnp.tile(q_gain, rep)[None, :], jnp.tile(k_gain, rep)[None, :],
                   cf, s1, s2, s, tm)
    kd = k.shape[-1]
    x1, h = _attention(sink, q.reshape(b, s, d), k.reshape(b, s, kd), v.reshape(b, s, kd),
                       x2d.reshape(b, s, d), w_o, mlp_gain)
    return x1.reshape(t, d), h.reshape(t, d)


def _trunk(x, p, tm, tm_mlp, tf):
    b, s, d = x.shape
    x1, h = _fourier_mixer(x, p["fourier_norm"][0][None, :], p["fourier_w_out"],
                           p["mlp_norm"][0][None, :], tm)
    x2 = _mlp(h, x1, p["mlp_w_up"], p["mlp_w_down"], 0, tm_mlp, tf)
    x3, h = _attn_mixer(x2, b, s, p["attn_norm"][0][None, :], p["attn_w_qkv"], p["attn_q_norm"][0],
                        p["attn_k_norm"][0], p["attn_sink"][0], p["attn_w_o"],
                        p["mlp_norm"][1][None, :], tm)
    x4 = _mlp(h, x3, p["mlp_w_up"], p["mlp_w_down"], 1, tm_mlp, tf)
    return x4.reshape(b, s, d)


def kernel(x_prompt, x_sample, fourier_norm, fourier_w_out, attn_norm, attn_w_qkv, attn_q_norm,
           attn_k_norm, attn_sink, attn_w_o, mlp_norm, mlp_w_up, mlp_w_down):
    p = dict(
        fourier_norm=fourier_norm, fourier_w_out=fourier_w_out.astype(BF16),
        attn_norm=attn_norm, attn_w_qkv=attn_w_qkv.astype(BF16), attn_q_norm=attn_q_norm,
        attn_k_norm=attn_k_norm, attn_sink=attn_sink, attn_w_o=attn_w_o.astype(BF16),
        mlp_norm=mlp_norm, mlp_w_up=mlp_w_up.astype(BF16), mlp_w_down=mlp_w_down.astype(BF16),
    )
    outs = []
    for x in (x_prompt, x_sample):
        t = x.shape[0] * x.shape[1]
        tm = min(512, x.shape[1])
        tm_mlp = min(1024, t)
        tf = min(1024, mlp_w_up.shape[-1] // 2)
        outs.append(_trunk(x, p, tm, tm_mlp, tf))
    return tuple(outs)
```

```python
import functools
import math

import numpy as np
import jax
import jax.numpy as jnp
from jax import lax
from jax.experimental import pallas as pl
from jax.experimental.pallas import tpu as pltpu

F32 = jnp.float32
BF16 = jnp.bfloat16

FFT_GROUPS = 8
HEAD_DIM = 64
N_KV = 4
GROUP = 8
WINDOW_BLOCK = 128
ROPE_THETA = 500000.0
ROT_DIM = HEAD_DIM // 4
ROT_HALF = ROT_DIM // 2
EPS = 1e-6
NEG = -1e30
LOG2E = math.log2(math.e)
ATTN_SUB = 4
MLP_CHUNKS_PER_ITER = 2
LANES = 128
MXU_DIM = 256
VMEM_LIMIT = 60 * 1024 * 1024


def _cparams(sem):
    return pltpu.CompilerParams(dimension_semantics=sem, vmem_limit_bytes=VMEM_LIMIT)


def _rms_rows(x, gain):
    ms = jnp.mean(x * x, axis=-1, keepdims=True)
    return x * lax.rsqrt(ms + EPS) * gain


def _seq_split(s):
    n1 = 1 << ((s.bit_length() - 1 + 1) // 2)
    n2 = s // n1
    assert n1 * n2 == s
    return n1, n2


SUBLANES = 8
BF16_ROWS = 2 * SUBLANES
STAGE1_GROUPS = 2


def _pack_complex(re, im):
    rb = lax.bitcast_convert_type(re.astype(BF16).astype(F32), jnp.uint32)
    ib = lax.bitcast_convert_type(im.astype(BF16).astype(F32), jnp.uint32)
    return rb | (ib >> 16)


def _fourier_stage1_kernel(x_ref, g_ref, cs_ref, tab_ref, o_ref, zs_ref, *, gd):
    n1 = x_ref.shape[1]
    rows = n1 * SUBLANES
    d = x_ref.shape[3]
    nblk = 2 * gd // LANES
    h = _rms_rows(x_ref[0].reshape(rows, d), g_ref[...]).astype(BF16)
    cs = cs_ref[...]
    gp = STAGE1_GROUPS
    for g0 in range(0, FFT_GROUPS, gp):
        slot = (g0 // gp) % 2
        for gi in range(gp):
            g = g0 + gi
            r = jnp.dot(h[:, g * gd:(g + 1) * gd], cs, preferred_element_type=F32)
            for c in range(nblk):
                zs_ref[slot, gi * nblk + c] = r[:, c * LANES:(c + 1) * LANES]
        for j in range(SUBLANES):
            parts = [zs_ref[slot, c, pl.ds(j, n1, stride=SUBLANES), :] for c in range(gp * nblk)]
            zr = jnp.concatenate(
                [parts[gi * nblk + c] for gi in range(gp) for c in range(nblk // 2)], axis=1)
            zi = jnp.concatenate(
                [parts[gi * nblk + nblk // 2 + c] for gi in range(gp) for c in range(nblk // 2)], axis=1)
            zc = jnp.concatenate([zr, zi], axis=0).astype(BF16)
            y = jnp.dot(tab_ref[j], zc, preferred_element_type=F32)
            o_ref[0, j, :, g0 * gd:(g0 + gp) * gd] = _pack_complex(y[:n1], y[n1:])


def _fourier_stage1(x4, gain, cs, tab):
    b, n1, n2, d = x4.shape
    gd = d // FFT_GROUPS
    rows = n1 * SUBLANES
    return pl.pallas_call(
        functools.partial(_fourier_stage1_kernel, gd=gd),
        grid=(b, n2 // SUBLANES),
        in_specs=[
            pl.BlockSpec((1, n1, SUBLANES, d), lambda bi, j: (bi, 0, j, 0)),
            pl.BlockSpec((1, d), lambda bi, j: (0, 0)),
            pl.BlockSpec((gd, 2 * gd), lambda bi, j: (0, 0)),
            pl.BlockSpec((SUBLANES, 2 * n1, 2 * n1), lambda bi, j: (j, 0, 0)),
        ],
        out_specs=pl.BlockSpec((1, SUBLANES, n1, d), lambda bi, j: (bi, j, 0, 0)),
        out_shape=jax.ShapeDtypeStruct((b, n2, n1, d), jnp.uint32),
        scratch_shapes=[pltpu.VMEM((2, STAGE1_GROUPS * 2 * gd // LANES, rows, LANES), F32)],
        compiler_params=_cparams(("parallel", "parallel")),
        name="fourier_stage1",
    )(x4, gain, cs, tab)


def _fourier_stage2_kernel(y_ref, tab_ref, o_ref, ys_ref, fs_ref, *, cw):
    n2 = y_ref.shape[1]
    rows = n2 * SUBLANES
    d = y_ref.shape[3]
    nblk = cw // LANES
    tab = tab_ref[...]
    for ch in range(d // cw):
        slot = ch % 2
        for c in range(nblk):
            lanes = slice(ch * cw + c * LANES, ch * cw + (c + 1) * LANES)
            ys_ref[slot, c] = y_ref[0, :, :, lanes].reshape(rows, LANES)
        for k in range(SUBLANES):
            u = jnp.concatenate(
                [ys_ref[slot, c, pl.ds(k, n2, stride=SUBLANES), :] for c in range(nblk)], axis=1)
            zc = pltpu.bitcast(u, BF16)
            f = jnp.dot(tab, zc, preferred_element_type=F32)
            for c in range(nblk):
                fs_ref[slot, c, pl.ds(k, n2, stride=SUBLANES), :] = f[:, c * LANES:(c + 1) * LANES]
        for c in range(nblk):
            lanes = slice(ch * cw + c * LANES, ch * cw + (c + 1) * LANES)
            o_ref[0, :, :, lanes] = fs_ref[slot, c].reshape(n2, SUBLANES, LANES)


def _fourier_stage2(y, tab):
    b, n2, n1, d = y.shape
    cw = 2 * MXU_DIM
    rows = n2 * SUBLANES
    return pl.pallas_call(
        functools.partial(_fourier_stage2_kernel, cw=cw),
        grid=(b, n1 // SUBLANES),
        in_specs=[
            pl.BlockSpec((1, n2, SUBLANES, d), lambda bi, k: (bi, 0, k, 0)),
            pl.BlockSpec((n2, 2 * n2), lambda bi, k: (0, 0)),
        ],
        out_specs=pl.BlockSpec((1, n2, SUBLANES, d), lambda bi, k: (bi, 0, k, 0)),
        out_shape=jax.ShapeDtypeStruct((b, n2, n1, d), F32),
        scratch_shapes=[pltpu.VMEM((2, cw // LANES, rows, LANES), jnp.uint32),
                        pltpu.VMEM((2, cw // LANES, rows, LANES), F32)],
        compiler_params=_cparams(("parallel", "parallel")),
        name="fourier_stage2",
    )(y, tab)


def _proj_norm_kernel(a_ref, x_ref, w_ref, g_ref, x1_ref, h_ref):
    x1 = x_ref[...] + jnp.dot(a_ref[...].astype(BF16), w_ref[...], preferred_element_type=F32)
    x1_ref[...] = x1
    h_ref[...] = _rms_rows(x1, g_ref[...]).astype(BF16)


def _proj_norm(a, x2d, w, gain, tm):
    t, d = x2d.shape
    return pl.pallas_call(
        _proj_norm_kernel,
        grid=(t // tm,),
        in_specs=[
            pl.BlockSpec((tm, d), lambda i: (i, 0)),
            pl.BlockSpec((tm, d), lambda i: (i, 0)),
            pl.BlockSpec((None, d, d), lambda i: (0, 0, 0)),
            pl.BlockSpec((1, d), lambda i: (0, 0)),
        ],
        out_specs=[
            pl.BlockSpec((tm, d), lambda i: (i, 0)),
            pl.BlockSpec((tm, d), lambda i: (i, 0)),
        ],
        out_shape=[jax.ShapeDtypeStruct((t, d), F32), jax.ShapeDtypeStruct((t, d), BF16)],
        compiler_params=_cparams(("parallel",)),
        name="proj_norm",
    )(a, x2d, w, gain)


def _mlp_kernel(h_ref, x_ref, wu_hbm, wd_hbm, o_ref, wu_buf, wd_buf, sem, *, layer, tf, ts):
    i = pl.program_id(0)
    nchunk = wu_hbm.shape[-1] // tf
    assert nchunk % 2 == 0

    def chunk_copies(j, slot):
        return (
            pltpu.make_async_copy(wu_hbm.at[layer, :, pl.ds(j * tf, tf)], wu_buf.at[slot], sem.at[0, slot]),
            pltpu.make_async_copy(wd_hbm.at[layer, pl.ds(j * tf, tf), :], wd_buf.at[slot], sem.at[1, slot]),
        )

    def start(j, slot):
        for c in chunk_copies(j, slot):
            c.start()

    @pl.when(i == 0)
    def _():
        start(0, 0)

    o_ref[...] = x_ref[...]
    per_iter = min(MLP_CHUNKS_PER_ITER, nchunk)
    niter = nchunk // per_iter
    assert per_iter % 2 == 0 and niter * per_iter == nchunk

    def body(jj, carry):
        for u in range(per_iter):
            j = per_iter * jj + u
            slot = u % 2
            for c in chunk_copies(j, slot):
                c.wait()
            if u + 1 < per_iter:
                start(j + 1, 1 - slot)
            else:
                @pl.when(jj + 1 < niter)
                def _():
                    start(j + 1, 0)

                @pl.when(jnp.logical_and(jj + 1 == niter, i + 1 < pl.num_programs(0)))
                def _():
                    start(0, 0)
            for s in range(tf // ts):
                a = jnp.maximum(
                    jnp.dot(h_ref[...], wu_buf[slot, :, s * ts:(s + 1) * ts], preferred_element_type=F32),
                    0.0)
                o_ref[...] += jnp.dot((a * a).astype(BF16), wd_buf[slot, s * ts:(s + 1) * ts, :],
                                      preferred_element_type=F32)
        return carry

    lax.fori_loop(0, niter, body, 0)


def _mlp(h, x1, w_up, w_down, layer, tm, tf):
    t, d = x1.shape
    ts = min(tf, 2 * MXU_DIM)
    return pl.pallas_call(
        functools.partial(_mlp_kernel, layer=layer, tf=tf, ts=ts),
        grid=(t // tm,),
        in_specs=[
            pl.BlockSpec((tm, d), lambda i: (i, 0)),
            pl.BlockSpec((tm, d), lambda i: (i, 0)),
            pl.BlockSpec(memory_space=pl.ANY),
            pl.BlockSpec(memory_space=pl.ANY),
        ],
        out_specs=pl.BlockSpec((tm, d), lambda i: (i, 0)),
        out_shape=jax.ShapeDtypeStruct((t, d), F32),
        scratch_shapes=[
            pltpu.VMEM((2, d, tf), BF16),
            pltpu.VMEM((2, tf, d), BF16),
            pltpu.SemaphoreType.DMA((2, 2)),
        ],
        compiler_params=_cparams(("arbitrary",)),
        name="sqrelu_mlp",
    )(h, x1, w_up, w_down)


def _qkv_kernel(x_ref, g_ref, w_ref, e_ref, qg_ref, kg_ref, cf_ref, s1_ref, s2_ref,
                q_ref, k_ref, v_ref, *, d, scale):
    h = _rms_rows(x_ref[...], g_ref[...]).astype(BF16)
    e = e_ref[...]
    cf, s1, s2 = cf_ref[...], s1_ref[...], s2_ref[...]
    kvd = N_KV * HEAD_DIM
    lo = lax.broadcasted_iota(jnp.int32, (h.shape[0], LANES), 1) < HEAD_DIM

    def proj(c0, width):
        return jnp.dot(h, w_ref[:, c0:c0 + width], preferred_element_type=F32)

    def head_norm_rope(blk, gain, out_scale):
        ss = jnp.dot((blk * blk).astype(BF16), e, preferred_element_type=F32)
        n = blk * lax.rsqrt(ss * (1.0 / HEAD_DIM) + EPS) * gain
        outs = []
        for u in range(MXU_DIM // LANES):
            xb = n[:, u * LANES:(u + 1) * LANES]
            r = (xb * cf + pltpu.roll(xb, LANES - ROT_HALF, axis=1) * s1
                 + pltpu.roll(xb, ROT_HALF, axis=1) * s2)
            outs.append(r * out_scale if out_scale != 1.0 else r)
        return outs

    def store_duplicated(ref, halves):
        for u, xb in enumerate(halves):
            xr = pltpu.roll(xb, HEAD_DIM, axis=1)
            ref[:, (2 * u) * LANES:(2 * u + 1) * LANES] = jnp.where(lo, xb, xr).astype(BF16)
            ref[:, (2 * u + 1) * LANES:(2 * u + 2) * LANES] = jnp.where(lo, xr, xb).astype(BF16)

    pw = 2 * MXU_DIM
    for c in range(d // pw):
        blk = proj(c * pw, pw)
        outs = (head_norm_rope(blk[:, :MXU_DIM], qg_ref[...], scale)
                + head_norm_rope(blk[:, MXU_DIM:], qg_ref[...], scale))
        q_ref[:, c * pw:(c + 1) * pw] = jnp.concatenate(outs, axis=1).astype(BF16)
    assert kvd == MXU_DIM
    kv = proj(d, 2 * kvd)
    store_duplicated(k_ref, head_norm_rope(kv[:, :kvd], kg_ref[...], 1.0))
    store_duplicated(v_ref, [kv[:, kvd + u * LANES:kvd + (u + 1) * LANES] for u in range(kvd // LANES)])


def _qkv(x2d, gain, w, e, qg, kg, cf, s1, s2, seq, tm):
    t, d = x2d.shape
    kd = N_KV * LANES
    nt = seq // tm
    tab_spec = pl.BlockSpec((tm, LANES), lambda i: (i % nt, 0))
    const = lambda shape: pl.BlockSpec(shape, lambda i: (0, 0))
    return pl.pallas_call(
        functools.partial(_qkv_kernel, d=d, scale=LOG2E / math.sqrt(HEAD_DIM)),
        grid=(t // tm,),
        in_specs=[
            pl.BlockSpec((tm, d), lambda i: (i, 0)),
            const((1, d)),
            pl.BlockSpec((None,) + w.shape[1:], lambda i: (0, 0, 0)),
            const((MXU_DIM, MXU_DIM)),
            const((1, MXU_DIM)),
            const((1, MXU_DIM)),
            tab_spec, tab_spec, tab_spec,
        ],
        out_specs=[
            pl.BlockSpec((tm, d), lambda i: (i, 0)),
            pl.BlockSpec((tm, kd), lambda i: (i, 0)),
            pl.BlockSpec((tm, kd), lambda i: (i, 0)),
        ],
        out_shape=[
            jax.ShapeDtypeStruct((t, d), BF16),
            jax.ShapeDtypeStruct((t, kd), BF16),
            jax.ShapeDtypeStruct((t, kd), BF16),
        ],
        compiler_params=_cparams(("parallel",)),
        name="qkv_norm_rope",
    )(x2d, gain, w, e, qg, kg, cf, s1, s2)


def _attn_kernel(sink_ref, q_ref, kp_ref, kc_ref, kn_ref, vp_ref, vc_ref, vn_ref, x_ref, w_ref, g_ref,
                 x1_ref, h_ref, o_scr, *, nb, nsub):
    j = pl.program_id(1)
    bq = WINDOW_BLOCK
    npair = GROUP // 2
    nq = GROUP * bq
    lo = lax.broadcasted_iota(jnp.int32, (bq, LANES), 1) < HEAD_DIM
    key = lax.broadcasted_iota(jnp.int32, (bq, nq), 0)
    qry = lax.broadcasted_iota(jnp.int32, (bq, nq), 1) % bq
    zero = jnp.zeros((bq, LANES), BF16)
    xr = BF16_ROWS
    lo_x = lax.broadcasted_iota(jnp.int32, (3 * bq + xr, LANES), 1) < HEAD_DIM
    sink_row = lax.broadcasted_iota(jnp.int32, (xr, nq), 0) == 0
    vzero = jnp.zeros((xr, LANES), BF16)
    nt = (((1,), (1,)), ((), ()))
    tn = (((0,), (0,)), ((), ()))

    def kv_blocks(prev_ref, cur_ref, next_ref, u, lanes):
        rows = lambda t: slice(t * bq, (t + 1) * bq)
        before = prev_ref[0, :, lanes] if u == 0 else cur_ref[0, rows(u - 1), lanes]
        after = next_ref[0, :, lanes] if u == nsub - 1 else cur_ref[0, rows(u + 1), lanes]
        return [before, cur_ref[0, rows(u), lanes], after]

    for u in range(nsub):
        blk_idx = j * nsub + u
        vprev = (key >= qry) & (blk_idx > 0)
        vnext = (key <= qry) & (blk_idx < nb - 1)
        qrows = slice(u * bq, (u + 1) * bq)
        for h in range(N_KV):
            ks = slice(h * LANES, (h + 1) * LANES)
            k2 = jnp.concatenate(kv_blocks(kp_ref, kc_ref, kn_ref, u, ks), axis=0)
            v2x = jnp.concatenate(kv_blocks(vp_ref, vc_ref, vn_ref, u, ks) + [vzero], axis=0)
            qparts, sinks = [], []
            for parity in range(2):
                for p in range(npair):
                    c0 = (h * npair + p) * LANES
                    blk = q_ref[0, qrows, c0:c0 + LANES]
                    qparts.append(jnp.where(lo if parity == 0 else ~lo, blk, zero))
                    sinks.append(jnp.full((1, bq), sink_ref[h * GROUP + 2 * p + parity] * LOG2E, F32))
            qs = jnp.concatenate(qparts, axis=0)
            sink = jnp.concatenate(sinks, axis=1)
            st = lax.dot_general(k2, qs, nt, preferred_element_type=F32)
            sp = jnp.where(vprev, st[:bq], NEG)
            sc = st[bq:2 * bq]
            sn = jnp.where(vnext, st[2 * bq:], NEG)
            m = jnp.max(jnp.maximum(jnp.maximum(sp, sc), sn), axis=0, keepdims=True)
            m = jnp.maximum(m, sink)
            es = jnp.where(sink_row, jnp.exp2(sink - m), 0.0)
            pt = jnp.concatenate(
                [jnp.exp2(sp - m), jnp.exp2(sc - m), jnp.exp2(sn - m), es], axis=0).astype(BF16)
            ot = lax.dot_general(jnp.where(lo_x, v2x, 1.0), pt, tn, preferred_element_type=F32)
            res = ot[:HEAD_DIM] * (1.0 / ot[HEAD_DIM:HEAD_DIM + 1])
            half = nq // 2
            for p in range(npair):
                pair = jnp.concatenate(
                    [res[:, p * bq:(p + 1) * bq], res[:, half + p * bq:half + (p + 1) * bq]], axis=0)
                c0 = (h * npair + p) * LANES
                o_scr[qrows, c0:c0 + LANES] = pair.T.astype(BF16)

    x1 = x_ref[0] + jnp.dot(o_scr[...], w_ref[...], preferred_element_type=F32)
    x1_ref[0] = x1
    h_ref[0] = _rms_rows(x1, g_ref[...]).astype(BF16)


def _attention(sink, q, k, v, x, w_o, gain):
    b, s, d = q.shape
    kd = k.shape[-1]
    bq = WINDOW_BLOCK
    nb = s // bq
    nsub = ATTN_SUB if nb % ATTN_SUB == 0 else 1
    prev = lambda bi, j: (bi, jnp.maximum(j * nsub - 1, 0), 0)
    cur = lambda bi, j: (bi, j, 0)
    nxt = lambda bi, j: (bi, jnp.minimum((j + 1) * nsub, nb - 1), 0)
    edge = lambda im: pl.BlockSpec((1, bq, kd), im)
    own = pl.BlockSpec((1, nsub * bq, kd), cur)
    tile = pl.BlockSpec((1, nsub * bq, d), cur)
    return pl.pallas_call(
        functools.partial(_attn_kernel, nb=nb, nsub=nsub),
        grid=(b, nb // nsub),
        in_specs=[
            pl.BlockSpec(memory_space=pltpu.SMEM),
            tile,
            edge(prev), own, edge(nxt), edge(prev), own, edge(nxt),
            tile,
            pl.BlockSpec((None, d, d), lambda bi, j: (0, 0, 0)),
            pl.BlockSpec((1, d), lambda bi, j: (0, 0)),
        ],
        out_specs=[tile, tile],
        out_shape=[jax.ShapeDtypeStruct((b, s, d), F32), jax.ShapeDtypeStruct((b, s, d), BF16)],
        scratch_shapes=[pltpu.VMEM((nsub * bq, d), BF16)],
        compiler_params=_cparams(("parallel", "parallel")),
        name="window_attention",
    )(sink, q, k, k, k, v, v, v, x, w_o, gain)


def _chan_table(gd):
    k = np.arange(gd)
    ang = 2.0 * np.pi * ((k[:, None] * k[None, :]) % gd) / gd
    sc = 1.0 / math.sqrt(gd)
    return jnp.asarray(np.concatenate([np.cos(ang) * sc, -np.sin(ang) * sc], axis=1), BF16)


def _stage1_table(n1, n2):
    n = n1 * n2
    k1 = np.arange(n1)
    tw = 2.0 * np.pi * ((np.arange(n2)[:, None] * k1[None, :]) % n) / n
    dft = 2.0 * np.pi * ((k1[:, None] * k1[None, :]) % n1) / n1
    ct, st = jnp.asarray(np.cos(tw), F32)[:, :, None], jnp.asarray(np.sin(tw), F32)[:, :, None]
    cd, sd = jnp.asarray(np.cos(dft), F32)[None], jnp.asarray(np.sin(dft), F32)[None]
    c, s = ct * cd - st * sd, st * cd + ct * sd
    top = jnp.concatenate([c, s], axis=-1)
    bot = jnp.concatenate([-s, c], axis=-1)
    return jnp.concatenate([top, bot], axis=-2).astype(BF16)


def _stage2_table(n1, n2):
    k = np.arange(n2)
    ang = 2.0 * np.pi * ((k[:, None] * k[None, :]) % n2) / n2
    sc = 1.0 / math.sqrt(n1 * n2)
    return jnp.asarray(np.stack([np.sin(ang) * sc, np.cos(ang) * sc], axis=-1).reshape(n2, 2 * n2), BF16)


def _rope_lane_tables(s):
    inv_freq = ROPE_THETA ** (-jnp.arange(0, ROT_DIM, 2, dtype=F32) / ROT_DIM)
    ang = jnp.arange(s, dtype=F32)[:, None] * inv_freq[None, :]
    cos, sin = jnp.cos(ang), jnp.sin(ang)
    pad = HEAD_DIM - ROT_DIM
    ones = jnp.ones((s, pad), F32)
    zeros = jnp.zeros((s, pad), F32)
    zh = jnp.zeros((s, ROT_HALF), F32)
    cf = jnp.concatenate([cos, cos, ones], axis=1)
    s1 = jnp.concatenate([-sin, zh, zeros], axis=1)
    s2 = jnp.concatenate([zh, sin, zeros], axis=1)
    rep = LANES // HEAD_DIM
    return tuple(jnp.tile(a, (1, rep)) for a in (cf, s1, s2))


def _segment_ones():
    i = np.arange(MXU_DIM) // HEAD_DIM
    return jnp.asarray((i[:, None] == i[None, :]).astype(np.float32), BF16)


def _fourier_mixer(x, gain, w_out, mlp_gain, tm):
    b, s, d = x.shape
    gd = d // FFT_GROUPS
    n1, n2 = _seq_split(s)
    y = _fourier_stage1(x.reshape(b, n1, n2, d), gain, _chan_table(gd), _stage1_table(n1, n2))
    f = _fourier_stage2(y, _stage2_table(n1, n2))
    return _proj_norm(f.reshape(b * s, d), x.reshape(b * s, d), w_out, mlp_gain, tm)


def _attn_mixer(x2d, b, s, gain, w_qkv, q_gain, k_gain, sink, w_o, mlp_gain, tm):
    t, d = x2d.shape
    rep = MXU_DIM // HEAD_DIM
    cf, s1, s2 = _rope_lane_tables(s)
    q, k, v = _qkv(x2d, gain, w_qkv, _segment_ones(),
                   jnp.tile(q_gain, rep)[None, :], jnp.tile(k_gain, rep)[None, :],
                   cf, s1, s2, s, tm)
    kd = k.shape[-1]
    x1, h = _attention(sink, q.reshape(b, s, d), k.reshape(b, s, kd), v.reshape(b, s, kd),
                       x2d.reshape(b, s, d), w_o, mlp_gain)
    return x1.reshape(t, d), h.reshape(t, d)


def _trunk(x, p, tm, tm_mlp, tf):
    b, s, d = x.shape
    x1, h = _fourier_mixer(x, p["fourier_norm"][0][None, :], p["fourier_w_out"],
                           p["mlp_norm"][0][None, :], tm)
    x2 = _mlp(h, x1, p["mlp_w_up"], p["mlp_w_down"], 0, tm_mlp, tf)
    x3, h = _attn_mixer(x2, b, s, p["attn_norm"][0][None, :], p["attn_w_qkv"], p["attn_q_norm"][0],
                        p["attn_k_norm"][0], p["attn_sink"][0], p["attn_w_o"],
                        p["mlp_norm"][1][None, :], tm)
    x4 = _mlp(h, x3, p["mlp_w_up"], p["mlp_w_down"], 1, tm_mlp, tf)
    return x4.reshape(b, s, d)


def kernel(x_prompt, x_sample, fourier_norm, fourier_w_out, attn_norm, attn_w_qkv, attn_q_norm,
           attn_k_norm, attn_sink, attn_w_o, mlp_norm, mlp_w_up, mlp_w_down):
    p = dict(
        fourier_norm=fourier_norm, fourier_w_out=fourier_w_out.astype(BF16),
        attn_norm=attn_norm, attn_w_qkv=attn_w_qkv.astype(BF16), attn_q_norm=attn_q_norm,
        attn_k_norm=attn_k_norm, attn_sink=attn_sink, attn_w_o=attn_w_o.astype(BF16),
        mlp_norm=mlp_norm, mlp_w_up=mlp_w_up.astype(BF16), mlp_w_down=mlp_w_down.astype(BF16),
    )
    outs = []
    for x in (x_prompt, x_sample):
        t = x.shape[0] * x.shape[1]
        tm = min(512, x.shape[1])
        tm_mlp = min(1024, t)
        tf = min(1024, mlp_w_up.shape[-1] // 2)
        outs.append(_trunk(x, p, tm, tm_mlp, tf))
    return tuple(outs)
```

```python
import functools
import math

import numpy as np
import jax
import jax.numpy as jnp
from jax import lax
from jax.experimental import pallas as pl
from jax.experimental.pallas import tpu as pltpu

F32 = jnp.float32
BF16 = jnp.bfloat16

FFT_GROUPS = 8
HEAD_DIM = 64
N_KV = 4
GROUP = 8
WINDOW_BLOCK = 128
ROPE_THETA = 500000.0
ROT_DIM = HEAD_DIM // 4
ROT_HALF = ROT_DIM // 2
EPS = 1e-6
NEG = -1e30
LOG2E = math.log2(math.e)
ATTN_SUB = 4
MLP_CHUNKS_PER_ITER = 2
LANES = 128
MXU_DIM = 256
VMEM_LIMIT = 60 * 1024 * 1024


def _cparams(sem):
    return pltpu.CompilerParams(dimension_semantics=sem, vmem_limit_bytes=VMEM_LIMIT)


def _rms_rows(x, gain):
    ms = jnp.mean(x * x, axis=-1, keepdims=True)
    return x * lax.rsqrt(ms + EPS) * gain


def _seq_split(s):
    n1 = 1 << ((s.bit_length() - 1 + 1) // 2)
    n2 = s // n1
    assert n1 * n2 == s
    return n1, n2


SUBLANES = 8
BF16_ROWS = 2 * SUBLANES
STAGE1_GROUPS = 2


def _pack_complex(re, im):
    rb = lax.bitcast_convert_type(re.astype(BF16).astype(F32), jnp.uint32)
    ib = lax.bitcast_convert_type(im.astype(BF16).astype(F32), jnp.uint32)
    return rb | (ib >> 16)


def _fourier_stage1_kernel(x_ref, g_ref, cs_ref, tab_ref, o_ref, zs_ref, *, gd):
    n1 = x_ref.shape[1]
    rows = n1 * SUBLANES
    d = x_ref.shape[3]
    nblk = 2 * gd // LANES
    h = _rms_rows(x_ref[0].reshape(rows, d), g_ref[...]).astype(BF16)
    cs = cs_ref[...]
    gp = STAGE1_GROUPS
    for g0 in range(0, FFT_GROUPS, gp):
        slot = (g0 // gp) % 2
        for gi in range(gp):
            g = g0 + gi
            r = jnp.dot(h[:, g * gd:(g + 1) * gd], cs, preferred_element_type=F32)
            for c in range(nblk):
                zs_ref[slot, gi * nblk + c] = r[:, c * LANES:(c + 1) * LANES]
        for j in range(SUBLANES):
            parts = [zs_ref[slot, c, pl.ds(j, n1, stride=SUBLANES), :] for c in range(gp * nblk)]
            zr = jnp.concatenate(
                [parts[gi * nblk + c] for gi in range(gp) for c in range(nblk // 2)], axis=1)
            zi = jnp.concatenate(
                [parts[gi * nblk + nblk // 2 + c] for gi in range(gp) for c in range(nblk // 2)], axis=1)
            zc = jnp.concatenate([zr, zi], axis=0).astype(BF16)
            y = jnp.dot(tab_ref[j], zc, preferred_element_type=F32)
            o_ref[0, j, :, g0 * gd:(g0 + gp) * gd] = _pack_complex(y[:n1], y[n1:])


def _fourier_stage1(x4, gain, cs, tab):
    b, n1, n2, d = x4.shape
    gd = d // FFT_GROUPS
    rows = n1 * SUBLANES
    return pl.pallas_call(
        functools.partial(_fourier_stage1_kernel, gd=gd),
        grid=(b, n2 // SUBLANES),
        in_specs=[
            pl.BlockSpec((1, n1, SUBLANES, d), lambda bi, j: (bi, 0, j, 0)),
            pl.BlockSpec((1, d), lambda bi, j: (0, 0)),
            pl.BlockSpec((gd, 2 * gd), lambda bi, j: (0, 0)),
            pl.BlockSpec((SUBLANES, 2 * n1, 2 * n1), lambda bi, j: (j, 0, 0)),
        ],
        out_specs=pl.BlockSpec((1, SUBLANES, n1, d), lambda bi, j: (bi, j, 0, 0)),
        out_shape=jax.ShapeDtypeStruct((b, n2, n1, d), jnp.uint32),
        scratch_shapes=[pltpu.VMEM((2, STAGE1_GROUPS * 2 * gd // LANES, rows, LANES), F32)],
        compiler_params=_cparams(("parallel", "parallel")),
        name="fourier_stage1",
    )(x4, gain, cs, tab)


def _fourier_stage2_kernel(y_ref, tab_ref, o_ref, ys_ref, fs_ref, *, cw):
    n2 = y_ref.shape[1]
    rows = n2 * SUBLANES
    d = y_ref.shape[3]
    nblk = cw // LANES
    tab = tab_ref[...]
    for ch in range(d // cw):
        slot = ch % 2
        for c in range(nblk):
            lanes = slice(ch * cw + c * LANES, ch * cw + (c + 1) * LANES)
            ys_ref[slot, c] = y_ref[0, :, :, lanes].reshape(rows, LANES)
        for k in range(SUBLANES):
            u = jnp.concatenate(
                [ys_ref[slot, c, pl.ds(k, n2, stride=SUBLANES), :] for c in range(nblk)], axis=1)
            zc = pltpu.bitcast(u, BF16)
            f = jnp.dot(tab, zc, preferred_element_type=F32)
            for c in range(nblk):
                fs_ref[slot, c, pl.ds(k, n2, stride=SUBLANES), :] = f[:, c * LANES:(c + 1) * LANES]
        for c in range(nblk):
            lanes = slice(ch * cw + c * LANES, ch * cw + (c + 1) * LANES)
            o_ref[0, :, :, lanes] = fs_ref[slot, c].reshape(n2, SUBLANES, LANES)


def _fourier_stage2(y, tab):
    b, n2, n1, d = y.shape
    cw = 2 * MXU_DIM
    rows = n2 * SUBLANES
    return pl.pallas_call(
        functools.partial(_fourier_stage2_kernel, cw=cw),
        grid=(b, n1 // SUBLANES),
        in_specs=[
            pl.BlockSpec((1, n2, SUBLANES, d), lambda bi, k: (bi, 0, k, 0)),
            pl.BlockSpec((n2, 2 * n2), lambda bi, k: (0, 0)),
        ],
        out_specs=pl.BlockSpec((1, n2, SUBLANES, d), lambda bi, k: (bi, 0, k, 0)),
        out_shape=jax.ShapeDtypeStruct((b, n2, n1, d), F32),
        scratch_shapes=[pltpu.VMEM((2, cw // LANES, rows, LANES), jnp.uint32),
                        pltpu.VMEM((2, cw // LANES, rows, LANES), F32)],
        compiler_params=_cparams(("parallel", "parallel")),
        name="fourier_stage2",
    )(y, tab)


def _proj_norm_kernel(a_ref, x_ref, w_ref, g_ref, x1_ref, h_ref):
    x1 = x_ref[...] + jnp.dot(a_ref[...].astype(BF16), w_ref[...], preferred_element_type=F32)
    x1_ref[...] = x1
    h_ref[...] = _rms_rows(x1, g_ref[...]).astype(BF16)


def _proj_norm(a, x2d, w, gain, tm):
    t, d = x2d.shape
    return pl.pallas_call(
        _proj_norm_kernel,
        grid=(t // tm,),
        in_specs=[
            pl.BlockSpec((tm, d), lambda i: (i, 0)),
            pl.BlockSpec((tm, d), lambda i: (i, 0)),
            pl.BlockSpec((None, d, d), lambda i: (0, 0, 0)),
            pl.BlockSpec((1, d), lambda i: (0, 0)),
        ],
        out_specs=[
            pl.BlockSpec((tm, d), lambda i: (i, 0)),
            pl.BlockSpec((tm, d), lambda i: (i, 0)),
        ],
        out_shape=[jax.ShapeDtypeStruct((t, d), F32), jax.ShapeDtypeStruct((t, d), BF16)],
        compiler_params=_cparams(("parallel",)),
        name="proj_norm",
    )(a, x2d, w, gain)


def _mlp_kernel(h_ref, x_ref, wu_hbm, wd_hbm, o_ref, wu_buf, wd_buf, sem, *, layer, tf, ts):
    i = pl.program_id(0)
    nchunk = wu_hbm.shape[-1] // tf
    assert nchunk % 2 == 0

    def chunk_copies(j, slot):
        return (
            pltpu.make_async_copy(wu_hbm.at[layer, :, pl.ds(j * tf, tf)], wu_buf.at[slot], sem.at[0, slot]),
            pltpu.make_async_copy(wd_hbm.at[layer, pl.ds(j * tf, tf), :], wd_buf.at[slot], sem.at[1, slot]),
        )

    def start(j, slot):
        for c in chunk_copies(j, slot):
            c.start()

    @pl.when(i == 0)
    def _():
        start(0, 0)

    o_ref[...] = x_ref[...]
    per_iter = min(MLP_CHUNKS_PER_ITER, nchunk)
    niter = nchunk // per_iter
    assert per_iter % 2 == 0 and niter * per_iter == nchunk

    def body(jj, carry):
        for u in range(per_iter):
            j = per_iter * jj + u
            slot = u % 2
            for c in chunk_copies(j, slot):
                c.wait()
            if u + 1 < per_iter:
                start(j + 1, 1 - slot)
            else:
                @pl.when(jj + 1 < niter)
                def _():
                    start(j + 1, 0)

                @pl.when(jnp.logical_and(jj + 1 == niter, i + 1 < pl.num_programs(0)))
                def _():
                    start(0, 0)
            for s in range(tf // ts):
                a = jnp.maximum(
                    jnp.dot(h_ref[...], wu_buf[slot, :, s * ts:(s + 1) * ts], preferred_element_type=F32),
                    0.0)
                o_ref[...] += jnp.dot((a * a).astype(BF16), wd_buf[slot, s * ts:(s + 1) * ts, :],
                                      preferred_element_type=F32)
        return carry

    lax.fori_loop(0, niter, body, 0)


def _mlp(h, x1, w_up, w_down, layer, tm, tf):
    t, d = x1.shape
    ts = min(tf, 2 * MXU_DIM)
    return pl.pallas_call(
        functools.partial(_mlp_kernel, layer=layer, tf=tf, ts=ts),
        grid=(t // tm,),
        in_specs=[
            pl.BlockSpec((tm, d), lambda i: (i, 0)),
            pl.BlockSpec((tm, d), lambda i: (i, 0)),
            pl.BlockSpec(memory_space=pl.ANY),
            pl.BlockSpec(memory_space=pl.ANY),
        ],
        out_specs=pl.BlockSpec((tm, d), lambda i: (i, 0)),
        out_shape=jax.ShapeDtypeStruct((t, d), F32),
        scratch_shapes=[
            pltpu.VMEM((2, d, tf), BF16),
            pltpu.VMEM((2, tf, d), BF16),
            pltpu.SemaphoreType.DMA((2, 2)),
        ],
        compiler_params=_cparams(("arbitrary",)),
        name="sqrelu_mlp",
    )(h, x1, w_up, w_down)


def _qkv_kernel(x_ref, g_ref, w_ref, e_ref, qg_ref, kg_ref, cf_ref, s1_ref, s2_ref,
                q_ref, k_ref, v_ref, *, d, scale):
    h = _rms_rows(x_ref[...], g_ref[...]).astype(BF16)
    e = e_ref[...]
    cf, s1, s2 = cf_ref[...], s1_ref[...], s2_ref[...]
    kvd = N_KV * HEAD_DIM
    lo = lax.broadcasted_iota(jnp.int32, (h.shape[0], LANES), 1) < HEAD_DIM

    def proj(c0, width):
        return jnp.dot(h, w_ref[:, c0:c0 + width], preferred_element_type=F32)

    def head_norm_rope(blk, gain, out_scale):
        ss = jnp.dot((blk * blk).astype(BF16), e, preferred_element_type=F32)
        n = blk * lax.rsqrt(ss * (1.0 / HEAD_DIM) + EPS) * gain
        outs = []
        for u in range(MXU_DIM // LANES):
            xb = n[:, u * LANES:(u + 1) * LANES]
            r = (xb * cf + pltpu.roll(xb, LANES - ROT_HALF, axis=1) * s1
                 + pltpu.roll(xb, ROT_HALF, axis=1) * s2)
            outs.append(r * out_scale if out_scale != 1.0 else r)
        return outs

    def store_duplicated(ref, halves):
        for u, xb in enumerate(halves):
            xr = pltpu.roll(xb, HEAD_DIM, axis=1)
            ref[:, (2 * u) * LANES:(2 * u + 1) * LANES] = jnp.where(lo, xb, xr).astype(BF16)
            ref[:, (2 * u + 1) * LANES:(2 * u + 2) * LANES] = jnp.where(lo, xr, xb).astype(BF16)

    pw = 2 * MXU_DIM
    for c in range(d // pw):
        blk = proj(c * pw, pw)
        outs = (head_norm_rope(blk[:, :MXU_DIM], qg_ref[...], scale)
                + head_norm_rope(blk[:, MXU_DIM:], qg_ref[...], scale))
        for u, xb in enumerate(outs):
            r0 = c * pw + u * LANES
            for tb in range(xb.shape[0] // LANES):
                q_ref[tb, r0:r0 + LANES, :] = xb[tb * LANES:(tb + 1) * LANES].T.astype(BF16)
    assert kvd == MXU_DIM
    kv = proj(d, 2 * kvd)
    store_duplicated(k_ref, head_norm_rope(kv[:, :kvd], kg_ref[...], 1.0))
    store_duplicated(v_ref, [kv[:, kvd + u * LANES:kvd + (u + 1) * LANES] for u in range(kvd // LANES)])


def _qkv(x2d, gain, w, e, qg, kg, cf, s1, s2, seq, tm):
    t, d = x2d.shape
    kd = N_KV * LANES
    nt = seq // tm
    tab_spec = pl.BlockSpec((tm, LANES), lambda i: (i % nt, 0))
    const = lambda shape: pl.BlockSpec(shape, lambda i: (0, 0))
    return pl.pallas_call(
        functools.partial(_qkv_kernel, d=d, scale=LOG2E / math.sqrt(HEAD_DIM)),
        grid=(t // tm,),
        in_specs=[
            pl.BlockSpec((tm, d), lambda i: (i, 0)),
            const((1, d)),
            pl.BlockSpec((None,) + w.shape[1:], lambda i: (0, 0, 0)),
            const((MXU_DIM, MXU_DIM)),
            const((1, MXU_DIM)),
            const((1, MXU_DIM)),
            tab_spec, tab_spec, tab_spec,
        ],
        out_specs=[
            pl.BlockSpec((tm // LANES, d, LANES), lambda i: (i, 0, 0)),
            pl.BlockSpec((tm, kd), lambda i: (i, 0)),
            pl.BlockSpec((tm, kd), lambda i: (i, 0)),
        ],
        out_shape=[
            jax.ShapeDtypeStruct((t // LANES, d, LANES), BF16),
            jax.ShapeDtypeStruct((t, kd), BF16),
            jax.ShapeDtypeStruct((t, kd), BF16),
        ],
        compiler_params=_cparams(("parallel",)),
        name="qkv_norm_rope",
    )(x2d, gain, w, e, qg, kg, cf, s1, s2)


def _attn_kernel(sink_ref, q_ref, kp_ref, kc_ref, kn_ref, vp_ref, vc_ref, vn_ref, x_ref, w_ref, g_ref,
                 x1_ref, h_ref, o_scr, *, nb, nsub):
    j = pl.program_id(1)
    bq = WINDOW_BLOCK
    npair = GROUP // 2
    nq = GROUP * bq
    key = lax.broadcasted_iota(jnp.int32, (bq, nq), 0)
    qry = lax.broadcasted_iota(jnp.int32, (bq, nq), 1) % bq
    qzero = jnp.zeros((LANES - HEAD_DIM, bq), BF16)
    xr = BF16_ROWS
    lo_x = lax.broadcasted_iota(jnp.int32, (3 * bq + xr, LANES), 1) < HEAD_DIM
    sink_row = lax.broadcasted_iota(jnp.int32, (xr, nq), 0) == 0
    vzero = jnp.zeros((xr, LANES), BF16)
    tn = (((0,), (0,)), ((), ()))

    def kv_blocks(prev_ref, cur_ref, next_ref, u, lanes):
        rows = lambda t: slice(t * bq, (t + 1) * bq)
        before = prev_ref[0, :, lanes] if u == 0 else cur_ref[0, rows(u - 1), lanes]
        after = next_ref[0, :, lanes] if u == nsub - 1 else cur_ref[0, rows(u + 1), lanes]
        return [before, cur_ref[0, rows(u), lanes], after]

    for u in range(nsub):
        blk_idx = j * nsub + u
        vprev = (key >= qry) & (blk_idx > 0)
        vnext = (key <= qry) & (blk_idx < nb - 1)
        qrows = slice(u * bq, (u + 1) * bq)
        for h in range(N_KV):
            ks = slice(h * LANES, (h + 1) * LANES)
            k2 = jnp.concatenate(kv_blocks(kp_ref, kc_ref, kn_ref, u, ks), axis=0)
            v2x = jnp.concatenate(kv_blocks(vp_ref, vc_ref, vn_ref, u, ks) + [vzero], axis=0)
            qparts, sinks = [], []
            for g in range(GROUP):
                r0 = (h * GROUP + g) * HEAD_DIM
                qparts.append(jnp.concatenate([q_ref[0, u, r0:r0 + HEAD_DIM, :], qzero], axis=0))
                sinks.append(jnp.full((1, bq), sink_ref[h * GROUP + g] * LOG2E, F32))
            qst = jnp.concatenate(qparts, axis=1)
            sink = jnp.concatenate(sinks, axis=1)
            st = jnp.dot(k2, qst, preferred_element_type=F32)
            sp = jnp.where(vprev, st[:bq], NEG)
            sc = st[bq:2 * bq]
            sn = jnp.where(vnext, st[2 * bq:], NEG)
            m = jnp.max(jnp.maximum(jnp.maximum(sp, sc), sn), axis=0, keepdims=True)
            m = jnp.maximum(m, sink)
            es = jnp.where(sink_row, jnp.exp2(sink - m), 0.0)
            pt = jnp.concatenate(
                [jnp.exp2(sp - m), jnp.exp2(sc - m), jnp.exp2(sn - m), es], axis=0).astype(BF16)
            ot = lax.dot_general(jnp.where(lo_x, v2x, 1.0), pt, tn, preferred_element_type=F32)
            res = ot[:HEAD_DIM] * (1.0 / ot[HEAD_DIM:HEAD_DIM + 1])
            for p in range(npair):
                pair = res[:, 2 * p * bq:(2 * p + 2) * bq]
                pair = jnp.concatenate([pair[:, :bq], pair[:, bq:]], axis=0)
                c0 = (h * npair + p) * LANES
                o_scr[qrows, c0:c0 + LANES] = pair.T.astype(BF16)

    x1 = x_ref[0] + jnp.dot(o_scr[...], w_ref[...], preferred_element_type=F32)
    x1_ref[0] = x1
    h_ref[0] = _rms_rows(x1, g_ref[...]).astype(BF16)


def _attention(sink, q, k, v, x, w_o, gain):
    b, s, d = x.shape
    kd = k.shape[-1]
    bq = WINDOW_BLOCK
    nb = s // bq
    nsub = ATTN_SUB if nb % ATTN_SUB == 0 else 1
    prev = lambda bi, j: (bi, jnp.maximum(j * nsub - 1, 0), 0)
    cur = lambda bi, j: (bi, j, 0)
    nxt = lambda bi, j: (bi, jnp.minimum((j + 1) * nsub, nb - 1), 0)
    edge = lambda im: pl.BlockSpec((1, bq, kd), im)
    own = pl.BlockSpec((1, nsub * bq, kd), cur)
    tile = pl.BlockSpec((1, nsub * bq, d), cur)
    return pl.pallas_call(
        functools.partial(_attn_kernel, nb=nb, nsub=nsub),
        grid=(b, nb // nsub),
        in_specs=[
            pl.BlockSpec(memory_space=pltpu.SMEM),
            pl.BlockSpec((1, nsub, d, bq), lambda bi, j: (bi, j, 0, 0)),
            edge(prev), own, edge(nxt), edge(prev), own, edge(nxt),
            tile,
            pl.BlockSpec((None, d, d), lambda bi, j: (0, 0, 0)),
            pl.BlockSpec((1, d), lambda bi, j: (0, 0)),
        ],
        out_specs=[tile, tile],
        out_shape=[jax.ShapeDtypeStruct((b, s, d), F32), jax.ShapeDtypeStruct((b, s, d), BF16)],
        scratch_shapes=[pltpu.VMEM((nsub * bq, d), BF16)],
        compiler_params=_cparams(("parallel", "parallel")),
        name="window_attention",
    )(sink, q, k, k, k, v, v, v, x, w_o, gain)


def _chan_table(gd):
    k = np.arange(gd)
    ang = 2.0 * np.pi * ((k[:, None] * k[None, :]) % gd) / gd
    sc = 1.0 / math.sqrt(gd)
    return jnp.asarray(np.concatenate([np.cos(ang) * sc, -np.sin(ang) * sc], axis=1), BF16)


def _stage1_table(n1, n2):
    n = n1 * n2
    k1 = np.arange(n1)
    tw = 2.0 * np.pi * ((np.arange(n2)[:, None] * k1[None, :]) % n) / n
    dft = 2.0 * np.pi * ((k1[:, None] * k1[None, :]) % n1) / n1
    ct, st = jnp.asarray(np.cos(tw), F32)[:, :, None], jnp.asarray(np.sin(tw), F32)[:, :, None]
    cd, sd = jnp.asarray(np.cos(dft), F32)[None], jnp.asarray(np.sin(dft), F32)[None]
    c, s = ct * cd - st * sd, st * cd + ct * sd
    top = jnp.concatenate([c, s], axis=-1)
    bot = jnp.concatenate([-s, c], axis=-1)
    return jnp.concatenate([top, bot], axis=-2).astype(BF16)


def _stage2_table(n1, n2):
    k = np.arange(n2)
    ang = 2.0 * np.pi * ((k[:, None] * k[None, :]) % n2) / n2
    sc = 1.0 / math.sqrt(n1 * n2)
    return jnp.asarray(np.stack([np.sin(ang) * sc, np.cos(ang) * sc], axis=-1).reshape(n2, 2 * n2), BF16)


def _rope_lane_tables(s):
    inv_freq = ROPE_THETA ** (-jnp.arange(0, ROT_DIM, 2, dtype=F32) / ROT_DIM)
    ang = jnp.arange(s, dtype=F32)[:, None] * inv_freq[None, :]
    cos, sin = jnp.cos(ang), jnp.sin(ang)
    pad = HEAD_DIM - ROT_DIM
    ones = jnp.ones((s, pad), F32)
    zeros = jnp.zeros((s, pad), F32)
    zh = jnp.zeros((s, ROT_HALF), F32)
    cf = jnp.concatenate([cos, cos, ones], axis=1)
    s1 = jnp.concatenate([-sin, zh, zeros], axis=1)
    s2 = jnp.concatenate([zh, sin, zeros], axis=1)
    rep = LANES // HEAD_DIM
    return tuple(jnp.tile(a, (1, rep)) for a in (cf, s1, s2))


def _segment_ones():
    i = np.arange(MXU_DIM) // HEAD_DIM
    return jnp.asarray((i[:, None] == i[None, :]).astype(np.float32), BF16)


def _fourier_mixer(x, gain, w_out, mlp_gain, tm):
    b, s, d = x.shape
    gd = d // FFT_GROUPS
    n1, n2 = _seq_split(s)
    y = _fourier_stage1(x.reshape(b, n1, n2, d), gain, _chan_table(gd), _stage1_table(n1, n2))
    f = _fourier_stage2(y, _stage2_table(n1, n2))
    return _proj_norm(f.reshape(b * s, d), x.reshape(b * s, d), w_out, mlp_gain, tm)


def _attn_mixer(x2d, b, s, gain, w_qkv, q_gain, k_gain, sink, w_o, mlp_gain, tm):
    t, d = x2d.shape
    rep = MXU_DIM // HEAD_DIM
    cf, s1, s2 = _rope_lane_tables(s)
    q, k, v = _qkv(x2d, gain, w_qkv, _segment_ones(),
                   jnp.tile(q_gain, rep)[None, :], jnp.tile(k_gain, rep)[None, :],
                   cf, s1, s2, s, tm)
    kd = k.shape[-1]
    x1, h = _attention(sink, q.reshape(b, s // WINDOW_BLOCK, d, WINDOW_BLOCK), k.reshape(b, s, kd), v.reshape(b, s, kd),
                       x2d.reshape(b, s, d), w_o, mlp_gain)
    return x1.reshape(t, d), h.reshape(t, d)


def _trunk(x, p, tm, tm_mlp, tf):
    b, s, d = x.shape
    x1, h = _fourier_mixer(x, p["fourier_norm"][0][None, :], p["fourier_w_out"],
                           p["mlp_norm"][0][None, :], tm)
    x2 = _mlp(h, x1, p["mlp_w_up"], p["mlp_w_down"], 0, tm_mlp, tf)
    x3, h = _attn_mixer(x2, b, s, p["attn_norm"][0][None, :], p["attn_w_qkv"], p["attn_q_norm"][0],
                        p["attn_k_norm"][0], p["attn_sink"][0], p["attn_w_o"],
                        p["mlp_norm"][1][None, :], tm)
    x4 = _mlp(h, x3, p["mlp_w_up"], p["mlp_w_down"], 1, tm_mlp, tf)
    return x4.reshape(b, s, d)


def kernel(x_prompt, x_sample, fourier_norm, fourier_w_out, attn_norm, attn_w_qkv, attn_q_norm,
           attn_k_norm, attn_sink, attn_w_o, mlp_norm, mlp_w_up, mlp_w_down):
    p = dict(
        fourier_norm=fourier_norm, fourier_w_out=fourier_w_out.astype(BF16),
        attn_norm=attn_norm, attn_w_qkv=attn_w_qkv.astype(BF16), attn_q_norm=attn_q_norm,
        attn_k_norm=attn_k_norm, attn_sink=attn_sink, attn_w_o=attn_w_o.astype(BF16),
        mlp_norm=mlp_norm, mlp_w_up=mlp_w_up.astype(BF16), mlp_w_down=mlp_w_down.astype(BF16),
    )
    outs = []
    for x in (x_prompt, x_sample):
        t = x.shape[0] * x.shape[1]
        tm = min(512, x.shape[1])
        tm_mlp = min(1024, t)
        tf = min(1024, mlp_w_up.shape[-1] // 2)
        outs.append(_trunk(x, p, tm, tm_mlp, tf))
    return tuple(outs)
```

```python
import functools
import math

import numpy as np
import jax
import jax.numpy as jnp
from jax import lax
from jax.experimental import pallas as pl
from jax.experimental.pallas import tpu as pltpu

F32 = jnp.float32
BF16 = jnp.bfloat16

FFT_GROUPS = 8
HEAD_DIM = 64
N_KV = 4
GROUP = 8
WINDOW_BLOCK = 128
ROPE_THETA = 500000.0
ROT_DIM = HEAD_DIM // 4
ROT_HALF = ROT_DIM // 2
EPS = 1e-6
NEG = -1e30
LOG2E = math.log2(math.e)
ATTN_SUB = 4
MLP_CHUNKS_PER_ITER = 2
LANES = 128
MXU_DIM = 256
VMEM_LIMIT = 60 * 1024 * 1024


def _cparams(sem):
    return pltpu.CompilerParams(dimension_semantics=sem, vmem_limit_bytes=VMEM_LIMIT)


def _rms_rows(x, gain):
    ms = jnp.mean(x * x, axis=-1, keepdims=True)
    return x * lax.rsqrt(ms + EPS) * gain


def _seq_split(s):
    n1 = 1 << ((s.bit_length() - 1 + 1) // 2)
    n2 = s // n1
    assert n1 * n2 == s
    return n1, n2


SUBLANES = 8
BF16_ROWS = 2 * SUBLANES
STAGE1_GROUPS = 2


def _pack_complex(re, im):
    rb = lax.bitcast_convert_type(re.astype(BF16).astype(F32), jnp.uint32)
    ib = lax.bitcast_convert_type(im.astype(BF16).astype(F32), jnp.uint32)
    return rb | (ib >> 16)


def _fourier_stage1_kernel(x_ref, g_ref, cs_ref, tab_ref, o_ref, zs_ref, *, gd):
    n1 = x_ref.shape[1]
    rows = n1 * SUBLANES
    d = x_ref.shape[3]
    nblk = 2 * gd // LANES
    h = _rms_rows(x_ref[0].reshape(rows, d), g_ref[...]).astype(BF16)
    cs = cs_ref[...]
    gp = STAGE1_GROUPS
    for g0 in range(0, FFT_GROUPS, gp):
        slot = (g0 // gp) % 2
        for gi in range(gp):
            g = g0 + gi
            r = jnp.dot(h[:, g * gd:(g + 1) * gd], cs, preferred_element_type=F32)
            for c in range(nblk):
                zs_ref[slot, gi * nblk + c] = r[:, c * LANES:(c + 1) * LANES]
        for j in range(SUBLANES):
            parts = [zs_ref[slot, c, pl.ds(j, n1, stride=SUBLANES), :] for c in range(gp * nblk)]
            zr = jnp.concatenate(
                [parts[gi * nblk + c] for gi in range(gp) for c in range(nblk // 2)], axis=1)
            zi = jnp.concatenate(
                [parts[gi * nblk + nblk // 2 + c] for gi in range(gp) for c in range(nblk // 2)], axis=1)
            zc = jnp.concatenate([zr, zi], axis=0).astype(BF16)
            y = jnp.dot(tab_ref[j], zc, preferred_element_type=F32)
            o_ref[0, j, :, g0 * gd:(g0 + gp) * gd] = _pack_complex(y[:n1], y[n1:])


def _fourier_stage1(x4, gain, cs, tab):
    b, n1, n2, d = x4.shape
    gd = d // FFT_GROUPS
    rows = n1 * SUBLANES
    return pl.pallas_call(
        functools.partial(_fourier_stage1_kernel, gd=gd),
        grid=(b, n2 // SUBLANES),
        in_specs=[
            pl.BlockSpec((1, n1, SUBLANES, d), lambda bi, j: (bi, 0, j, 0)),
            pl.BlockSpec((1, d), lambda bi, j: (0, 0)),
            pl.BlockSpec((gd, 2 * gd), lambda bi, j: (0, 0)),
            pl.BlockSpec((SUBLANES, 2 * n1, 2 * n1), lambda bi, j: (j, 0, 0)),
        ],
        out_specs=pl.BlockSpec((1, SUBLANES, n1, d), lambda bi, j: (bi, j, 0, 0)),
        out_shape=jax.ShapeDtypeStruct((b, n2, n1, d), jnp.uint32),
        scratch_shapes=[pltpu.VMEM((2, STAGE1_GROUPS * 2 * gd // LANES, rows, LANES), F32)],
        compiler_params=_cparams(("parallel", "parallel")),
        name="fourier_stage1",
    )(x4, gain, cs, tab)


def _fourier_stage2_kernel(y_ref, tab_ref, o_ref, ys_ref, fs_ref, *, cw):
    n2 = y_ref.shape[1]
    rows = n2 * SUBLANES
    d = y_ref.shape[3]
    nblk = cw // LANES
    tab = tab_ref[...]
    for ch in range(d // cw):
        slot = ch % 2
        for c in range(nblk):
            lanes = slice(ch * cw + c * LANES, ch * cw + (c + 1) * LANES)
            ys_ref[slot, c] = y_ref[0, :, :, lanes].reshape(rows, LANES)
        for k in range(SUBLANES):
            u = jnp.concatenate(
                [ys_ref[slot, c, pl.ds(k, n2, stride=SUBLANES), :] for c in range(nblk)], axis=1)
            zc = pltpu.bitcast(u, BF16)
            f = jnp.dot(tab, zc, preferred_element_type=F32)
            for c in range(nblk):
                fs_ref[slot, c, pl.ds(k, n2, stride=SUBLANES), :] = f[:, c * LANES:(c + 1) * LANES]
        for c in range(nblk):
            lanes = slice(ch * cw + c * LANES, ch * cw + (c + 1) * LANES)
            o_ref[0, :, :, lanes] = fs_ref[slot, c].reshape(n2, SUBLANES, LANES)


def _fourier_stage2(y, tab):
    b, n2, n1, d = y.shape
    cw = 2 * MXU_DIM
    rows = n2 * SUBLANES
    return pl.pallas_call(
        functools.partial(_fourier_stage2_kernel, cw=cw),
        grid=(b, n1 // SUBLANES),
        in_specs=[
            pl.BlockSpec((1, n2, SUBLANES, d), lambda bi, k: (bi, 0, k, 0)),
            pl.BlockSpec((n2, 2 * n2), lambda bi, k: (0, 0)),
        ],
        out_specs=pl.BlockSpec((1, n2, SUBLANES, d), lambda bi, k: (bi, 0, k, 0)),
        out_shape=jax.ShapeDtypeStruct((b, n2, n1, d), F32),
        scratch_shapes=[pltpu.VMEM((2, cw // LANES, rows, LANES), jnp.uint32),
                        pltpu.VMEM((2, cw // LANES, rows, LANES), F32)],
        compiler_params=_cparams(("parallel", "parallel")),
        name="fourier_stage2",
    )(y, tab)


def _proj_norm_kernel(a_ref, x_ref, w_ref, g_ref, x1_ref, h_ref):
    x1 = x_ref[...] + jnp.dot(a_ref[...].astype(BF16), w_ref[...], preferred_element_type=F32)
    x1_ref[...] = x1
    h_ref[...] = _rms_rows(x1, g_ref[...]).astype(BF16)


def _proj_norm(a, x2d, w, gain, tm):
    t, d = x2d.shape
    return pl.pallas_call(
        _proj_norm_kernel,
        grid=(t // tm,),
        in_specs=[
            pl.BlockSpec((tm, d), lambda i: (i, 0)),
            pl.BlockSpec((tm, d), lambda i: (i, 0)),
            pl.BlockSpec((None, d, d), lambda i: (0, 0, 0)),
            pl.BlockSpec((1, d), lambda i: (0, 0)),
        ],
        out_specs=[
            pl.BlockSpec((tm, d), lambda i: (i, 0)),
            pl.BlockSpec((tm, d), lambda i: (i, 0)),
        ],
        out_shape=[jax.ShapeDtypeStruct((t, d), F32), jax.ShapeDtypeStruct((t, d), BF16)],
        compiler_params=_cparams(("parallel",)),
        name="proj_norm",
    )(a, x2d, w, gain)


def _mlp_kernel(h_ref, x_ref, wu_hbm, wd_hbm, o_ref, wu_buf, wd_buf, sem, *, layer, tf, ts):
    i = pl.program_id(0)
    nchunk = wu_hbm.shape[-1] // tf
    assert nchunk % 2 == 0

    def chunk_copies(j, slot):
        return (
            pltpu.make_async_copy(wu_hbm.at[layer, :, pl.ds(j * tf, tf)], wu_buf.at[slot], sem.at[0, slot]),
            pltpu.make_async_copy(wd_hbm.at[layer, pl.ds(j * tf, tf), :], wd_buf.at[slot], sem.at[1, slot]),
        )

    def start(j, slot):
        for c in chunk_copies(j, slot):
            c.start()

    @pl.when(i == 0)
    def _():
        start(0, 0)

    o_ref[...] = x_ref[...]
    per_iter = min(MLP_CHUNKS_PER_ITER, nchunk)
    niter = nchunk // per_iter
    assert per_iter % 2 == 0 and niter * per_iter == nchunk

    def body(jj, carry):
        for u in range(per_iter):
            j = per_iter * jj + u
            slot = u % 2
            for c in chunk_copies(j, slot):
                c.wait()
            if u + 1 < per_iter:
                start(j + 1, 1 - slot)
            else:
                @pl.when(jj + 1 < niter)
                def _():
                    start(j + 1, 0)

                @pl.when(jnp.logical_and(jj + 1 == niter, i + 1 < pl.num_programs(0)))
                def _():
                    start(0, 0)
            for s in range(tf // ts):
                a = jnp.maximum(
                    jnp.dot(h_ref[...], wu_buf[slot, :, s * ts:(s + 1) * ts], preferred_element_type=F32),
                    0.0)
                o_ref[...] += jnp.dot((a * a).astype(BF16), wd_buf[slot, s * ts:(s + 1) * ts, :],
                                      preferred_element_type=F32)
        return carry

    lax.fori_loop(0, niter, body, 0)


def _mlp(h, x1, w_up, w_down, layer, tm, tf):
    t, d = x1.shape
    ts = min(tf, 2 * MXU_DIM)
    return pl.pallas_call(
        functools.partial(_mlp_kernel, layer=layer, tf=tf, ts=ts),
        grid=(t // tm,),
        in_specs=[
            pl.BlockSpec((tm, d), lambda i: (i, 0)),
            pl.BlockSpec((tm, d), lambda i: (i, 0)),
            pl.BlockSpec(memory_space=pl.ANY),
            pl.BlockSpec(memory_space=pl.ANY),
        ],
        out_specs=pl.BlockSpec((tm, d), lambda i: (i, 0)),
        out_shape=jax.ShapeDtypeStruct((t, d), F32),
        scratch_shapes=[
            pltpu.VMEM((2, d, tf), BF16),
            pltpu.VMEM((2, tf, d), BF16),
            pltpu.SemaphoreType.DMA((2, 2)),
        ],
        compiler_params=_cparams(("arbitrary",)),
        name="sqrelu_mlp",
    )(h, x1, w_up, w_down)


def _qkv_kernel(x_ref, g_ref, w_ref, e_ref, qg_ref, kg_ref, cf_ref, s1_ref, s2_ref,
                q_ref, k_ref, v_ref, *, d, scale):
    h = _rms_rows(x_ref[...], g_ref[...]).astype(BF16)
    e = e_ref[...]
    cf, s1, s2 = cf_ref[...], s1_ref[...], s2_ref[...]
    kvd = N_KV * HEAD_DIM
    lo = lax.broadcasted_iota(jnp.int32, (h.shape[0], LANES), 1) < HEAD_DIM

    def proj(c0, width):
        return jnp.dot(h, w_ref[:, c0:c0 + width], preferred_element_type=F32)

    def head_norm_rope(blk, gain, out_scale):
        ss = jnp.dot((blk * blk).astype(BF16), e, preferred_element_type=F32)
        n = blk * lax.rsqrt(ss * (1.0 / HEAD_DIM) + EPS) * gain
        outs = []
        for u in range(MXU_DIM // LANES):
            xb = n[:, u * LANES:(u + 1) * LANES]
            r = (xb * cf + pltpu.roll(xb, LANES - ROT_HALF, axis=1) * s1
                 + pltpu.roll(xb, ROT_HALF, axis=1) * s2)
            outs.append(r * out_scale if out_scale != 1.0 else r)
        return outs

    def store_duplicated(ref, halves):
        for u, xb in enumerate(halves):
            xr = pltpu.roll(xb, HEAD_DIM, axis=1)
            ref[:, (2 * u) * LANES:(2 * u + 1) * LANES] = jnp.where(lo, xb, xr).astype(BF16)
            ref[:, (2 * u + 1) * LANES:(2 * u + 2) * LANES] = jnp.where(lo, xr, xb).astype(BF16)

    pw = 2 * MXU_DIM
    for c in range(d // pw):
        blk = proj(c * pw, pw)
        outs = (head_norm_rope(blk[:, :MXU_DIM], qg_ref[...], scale)
                + head_norm_rope(blk[:, MXU_DIM:], qg_ref[...], scale))
        q_ref[:, c * pw:(c + 1) * pw] = jnp.concatenate(outs, axis=1).astype(BF16)
    assert kvd == MXU_DIM
    kv = proj(d, 2 * kvd)
    store_duplicated(k_ref, head_norm_rope(kv[:, :kvd], kg_ref[...], 1.0))
    store_duplicated(v_ref, [kv[:, kvd + u * LANES:kvd + (u + 1) * LANES] for u in range(kvd // LANES)])


def _qkv(x2d, gain, w, e, qg, kg, cf, s1, s2, seq, tm):
    t, d = x2d.shape
    kd = N_KV * LANES
    nt = seq // tm
    tab_spec = pl.BlockSpec((tm, LANES), lambda i: (i % nt, 0))
    const = lambda shape: pl.BlockSpec(shape, lambda i: (0, 0))
    return pl.pallas_call(
        functools.partial(_qkv_kernel, d=d, scale=LOG2E / math.sqrt(HEAD_DIM)),
        grid=(t // tm,),
        in_specs=[
            pl.BlockSpec((tm, d), lambda i: (i, 0)),
            const((1, d)),
            pl.BlockSpec((None,) + w.shape[1:], lambda i: (0, 0, 0)),
            const((MXU_DIM, MXU_DIM)),
            const((1, MXU_DIM)),
            const((1, MXU_DIM)),
            tab_spec, tab_spec, tab_spec,
        ],
        out_specs=[
            pl.BlockSpec((tm, d), lambda i: (i, 0)),
            pl.BlockSpec((tm, kd), lambda i: (i, 0)),
            pl.BlockSpec((tm, kd), lambda i: (i, 0)),
        ],
        out_shape=[
            jax.ShapeDtypeStruct((t, d), BF16),
            jax.ShapeDtypeStruct((t, kd), BF16),
            jax.ShapeDtypeStruct((t, kd), BF16),
        ],
        compiler_params=_cparams(("parallel",)),
        name="qkv_norm_rope",
    )(x2d, gain, w, e, qg, kg, cf, s1, s2)


def _attn_kernel(sink_ref, q_ref, kp_ref, kc_ref, kn_ref, vp_ref, vc_ref, vn_ref, x_ref, w_ref, g_ref,
                 x1_ref, h_ref, o_scr, *, nb, nsub):
    j = pl.program_id(1)
    bq = WINDOW_BLOCK
    npair = GROUP // 2
    nq = GROUP * bq
    lo = lax.broadcasted_iota(jnp.int32, (bq, LANES), 1) < HEAD_DIM
    key = lax.broadcasted_iota(jnp.int32, (bq, nq), 0)
    qry = lax.broadcasted_iota(jnp.int32, (bq, nq), 1) % bq
    zero = jnp.zeros((bq, LANES), BF16)
    xr = BF16_ROWS
    lo_x = lax.broadcasted_iota(jnp.int32, (3 * bq + xr, LANES), 1) < HEAD_DIM
    sink_row = lax.broadcasted_iota(jnp.int32, (xr, nq), 0) == 0
    vzero = jnp.zeros((xr, LANES), BF16)
    nt = (((1,), (1,)), ((), ()))
    tn = (((0,), (0,)), ((), ()))

    def kv_blocks(prev_ref, cur_ref, next_ref, u, lanes):
        rows = lambda t: slice(t * bq, (t + 1) * bq)
        before = prev_ref[0, :, lanes] if u == 0 else cur_ref[0, rows(u - 1), lanes]
        after = next_ref[0, :, lanes] if u == nsub - 1 else cur_ref[0, rows(u + 1), lanes]
        return [before, cur_ref[0, rows(u), lanes], after]

    for u in range(nsub):
        blk_idx = j * nsub + u
        vprev = (key >= qry) & (blk_idx > 0)
        vnext = (key <= qry) & (blk_idx < nb - 1)
        qrows = slice(u * bq, (u + 1) * bq)
        for h in range(N_KV):
            ks = slice(h * LANES, (h + 1) * LANES)
            k2 = jnp.concatenate(kv_blocks(kp_ref, kc_ref, kn_ref, u, ks), axis=0)
            v2x = jnp.concatenate(kv_blocks(vp_ref, vc_ref, vn_ref, u, ks) + [vzero], axis=0)
            qparts, sinks = [], []
            for parity in range(2):
                for p in range(npair):
                    c0 = (h * npair + p) * LANES
                    blk = q_ref[0, qrows, c0:c0 + LANES]
                    qparts.append(jnp.where(lo if parity == 0 else ~lo, blk, zero))
                    sinks.append(jnp.full((1, bq), sink_ref[h * GROUP + 2 * p + parity] * LOG2E, F32))
            qs = jnp.concatenate(qparts, axis=0)
            sink = jnp.concatenate(sinks, axis=1)
            st = lax.dot_general(k2, qs, nt, preferred_element_type=F32)
            sp = jnp.where(vprev, st[:bq], NEG)
            sc = st[bq:2 * bq]
            sn = jnp.where(vnext, st[2 * bq:], NEG)
            m = jnp.max(jnp.maximum(jnp.maximum(sp, sc), sn), axis=0, keepdims=True)
            m = jnp.maximum(m, sink)
            es = jnp.where(sink_row, jnp.exp2(sink - m), 0.0)
            pt = jnp.concatenate(
                [jnp.exp2(sp - m), jnp.exp2(sc - m), jnp.exp2(sn - m), es], axis=0).astype(BF16)
            ot = lax.dot_general(jnp.where(lo_x, v2x, 1.0), pt, tn, preferred_element_type=F32)
            res = ot[:HEAD_DIM] * (1.0 / ot[HEAD_DIM:HEAD_DIM + 1])
            half = nq // 2
            for p in range(npair):
                pair = jnp.concatenate(
                    [res[:, p * bq:(p + 1) * bq], res[:, half + p * bq:half + (p + 1) * bq]], axis=0)
                c0 = (h * npair + p) * LANES
                o_scr[qrows, c0:c0 + LANES] = pair.T.astype(BF16)

    x1 = x_ref[0] + jnp.dot(o_scr[...], w_ref[...], preferred_element_type=F32)
    x1_ref[0] = x1
    h_ref[0] = _rms_rows(x1, g_ref[...]).astype(BF16)


def _attention(sink, q, k, v, x, w_o, gain):
    b, s, d = q.shape
    kd = k.shape[-1]
    bq = WINDOW_BLOCK
    nb = s // bq
    nsub = ATTN_SUB if nb % ATTN_SUB == 0 else 1
    prev = lambda bi, j: (bi, jnp.maximum(j * nsub - 1, 0), 0)
    cur = lambda bi, j: (bi, j, 0)
    nxt = lambda bi, j: (bi, jnp.minimum((j + 1) * nsub, nb - 1), 0)
    edge = lambda im: pl.BlockSpec((1, bq, kd), im)
    own = pl.BlockSpec((1, nsub * bq, kd), cur)
    tile = pl.BlockSpec((1, nsub * bq, d), cur)
    return pl.pallas_call(
        functools.partial(_attn_kernel, nb=nb, nsub=nsub),
        grid=(b, nb // nsub),
        in_specs=[
            pl.BlockSpec(memory_space=pltpu.SMEM),
            tile,
            edge(prev), own, edge(nxt), edge(prev), own, edge(nxt),
            tile,
            pl.BlockSpec((None, d, d), lambda bi, j: (0, 0, 0)),
            pl.BlockSpec((1, d), lambda bi, j: (0, 0)),
        ],
        out_specs=[tile, tile],
        out_shape=[jax.ShapeDtypeStruct((b, s, d), F32), jax.ShapeDtypeStruct((b, s, d), BF16)],
        scratch_shapes=[pltpu.VMEM((nsub * bq, d), BF16)],
        compiler_params=_cparams(("parallel", "parallel")),
        name="window_attention",
    )(sink, q, k, k, k, v, v, v, x, w_o, gain)


def _chan_table(gd):
    k = np.arange(gd)
    ang = 2.0 * np.pi * ((k[:, None] * k[None, :]) % gd) / gd
    sc = 1.0 / math.sqrt(gd)
    return jnp.asarray(np.concatenate([np.cos(ang) * sc, -np.sin(ang) * sc], axis=1), BF16)


def _stage1_table(n1, n2):
    n = n1 * n2
    k1 = np.arange(n1)
    tw = 2.0 * np.pi * ((np.arange(n2)[:, None] * k1[None, :]) % n) / n
    dft = 2.0 * np.pi * ((k1[:, None] * k1[None, :]) % n1) / n1
    ct, st = jnp.asarray(np.cos(tw), F32)[:, :, None], jnp.asarray(np.sin(tw), F32)[:, :, None]
    cd, sd = jnp.asarray(np.cos(dft), F32)[None], jnp.asarray(np.sin(dft), F32)[None]
    c, s = ct * cd - st * sd, st * cd + ct * sd
    top = jnp.concatenate([c, s], axis=-1)
    bot = jnp.concatenate([-s, c], axis=-1)
    return jnp.concatenate([top, bot], axis=-2).astype(BF16)


def _stage2_table(n1, n2):
    k = np.arange(n2)
    ang = 2.0 * np.pi * ((k[:, None] * k[None, :]) % n2) / n2
    sc = 1.0 / math.sqrt(n1 * n2)
    return jnp.asarray(np.stack([np.sin(ang) * sc, np.cos(ang) * sc], axis=-1).reshape(n2, 2 * n2), BF16)


def _rope_lane_tables(s):
    inv_freq = ROPE_THETA ** (-jnp.arange(0, ROT_DIM, 2, dtype=F32) / ROT_DIM)
    ang = jnp.arange(s, dtype=F32)[:, None] * inv_freq[None, :]
    cos, sin = jnp.cos(ang), jnp.sin(ang)
    pad = HEAD_DIM - ROT_DIM
    ones = jnp.ones((s, pad), F32)
    zeros = jnp.zeros((s, pad), F32)
    zh = jnp.zeros((s, ROT_HALF), F32)
    cf = jnp.concatenate([cos, cos, ones], axis=1)
    s1 = jnp.concatenate([-sin, zh, zeros], axis=1)
    s2 = jnp.concatenate([zh, sin, zeros], axis=1)
    rep = LANES // HEAD_DIM
    return tuple(jnp.tile(a, (1, rep)) for a in (cf, s1, s2))


def _segment_ones():
    i = np.arange(MXU_DIM) // HEAD_DIM
    return jnp.asarray((i[:, None] == i[None, :]).astype(np.float32), BF16)


def _fourier_mixer(x, gain, w_out, mlp_gain, tm):
    b, s, d = x.shape
    gd = d // FFT_GROUPS
    n1, n2 = _seq_split(s)
    y = _fourier_stage1(x.reshape(b, n1, n2, d), gain, _chan_table(gd), _stage1_table(n1, n2))
    f = _fourier_stage2(y, _stage2_table(n1, n2))
    return _proj_norm(f.reshape(b * s, d), x.reshape(b * s, d), w_out, mlp_gain, tm)


def _attn_mixer(x2d, b, s, gain, w_qkv, q_gain, k_gain, sink, w_o, mlp_gain, tm):
    t, d = x2d.shape
    rep = MXU_DIM // HEAD_DIM
    cf, s1, s2 = _rope_lane_tables(s)
    q, k, v = _qkv(x2d, gain, w_qkv, _segment_ones(),
                   jnp.tile(q_gain, rep)[None, :], jnp.tile(k_gain, rep)[None, :],
                   cf, s1, s2, s, tm)
    kd = k.shape[-1]
    x1, h = _attention(sink, q.reshape(b, s, d), k.reshape(b, s, kd), v.reshape(b, s, kd),
                       x2d.reshape(b, s, d), w_o, mlp_gain)
    return x1.reshape(t, d), h.reshape(t, d)


def _tile_sizes(batch, seq, dff):
    tm = min(2 * MXU_DIM, seq)
    tm_mlp = min(4 * MXU_DIM, batch * seq)
    tf = min(4 * MXU_DIM, dff // 2)
    return tm, tm_mlp, tf


def _trunk(x, p, tm, tm_mlp, tf):
    b, s, d = x.shape
    x1, h = _fourier_mixer(x, p["fourier_norm"][0][None, :], p["fourier_w_out"],
                           p["mlp_norm"][0][None, :], tm)
    x2 = _mlp(h, x1, p["mlp_w_up"], p["mlp_w_down"], 0, tm_mlp, tf)
    x3, h = _attn_mixer(x2, b, s, p["attn_norm"][0][None, :], p["attn_w_qkv"], p["attn_q_norm"][0],
                        p["attn_k_norm"][0], p["attn_sink"][0], p["attn_w_o"],
                        p["mlp_norm"][1][None, :], tm)
    x4 = _mlp(h, x3, p["mlp_w_up"], p["mlp_w_down"], 1, tm_mlp, tf)
    return x4.reshape(b, s, d)


def kernel(x_prompt, x_sample, fourier_norm, fourier_w_out, attn_norm, attn_w_qkv, attn_q_norm,
           attn_k_norm, attn_sink, attn_w_o, mlp_norm, mlp_w_up, mlp_w_down):
    p = dict(
        fourier_norm=fourier_norm, fourier_w_out=fourier_w_out.astype(BF16),
        attn_norm=attn_norm, attn_w_qkv=attn_w_qkv.astype(BF16), attn_q_norm=attn_q_norm,
        attn_k_norm=attn_k_norm, attn_sink=attn_sink, attn_w_o=attn_w_o.astype(BF16),
        mlp_norm=mlp_norm, mlp_w_up=mlp_w_up.astype(BF16), mlp_w_down=mlp_w_down.astype(BF16),
    )
    dff = mlp_w_up.shape[-1]
    return tuple(_trunk(x, p, *_tile_sizes(x.shape[0], x.shape[1], dff)) for x in (x_prompt, x_sample))
```

```python
import functools
import math

import numpy as np
import jax
import jax.numpy as jnp
from jax import lax
from jax.experimental import pallas as pl
from jax.experimental.pallas import tpu as pltpu

F32 = jnp.float32
BF16 = jnp.bfloat16

FFT_GROUPS = 8
HEAD_DIM = 64
N_KV = 4
GROUP = 8
WINDOW_BLOCK = 128
ROPE_THETA = 500000.0
ROT_DIM = HEAD_DIM // 4
ROT_HALF = ROT_DIM // 2
EPS = 1e-6
NEG = -1e30
LOG2E = math.log2(math.e)
ATTN_SUB = 4
MLP_CHUNKS_PER_ITER = 2
LANES = 128
MXU_DIM = 256
VMEM_LIMIT = 60 * 1024 * 1024


def _cparams(sem):
    return pltpu.CompilerParams(dimension_semantics=sem, vmem_limit_bytes=VMEM_LIMIT)


def _rms_rows(x, gain):
    ms = jnp.mean(x * x, axis=-1, keepdims=True)
    return x * lax.rsqrt(ms + EPS) * gain


def _seq_split(s):
    n1 = 1 << ((s.bit_length() - 1 + 1) // 2)
    n2 = s // n1
    assert n1 * n2 == s
    return n1, n2


SUBLANES = 8
BF16_ROWS = 2 * SUBLANES
STAGE1_GROUPS = 2


def _pack_complex(re, im):
    rb = lax.bitcast_convert_type(re.astype(BF16).astype(F32), jnp.uint32)
    ib = lax.bitcast_convert_type(im.astype(BF16).astype(F32), jnp.uint32)
    return rb | (ib >> 16)


def _fourier_stage1_kernel(x_ref, g_ref, cs_ref, tab_ref, o_ref, zs_ref, *, gd):
    n1 = x_ref.shape[1]
    rows = n1 * SUBLANES
    d = x_ref.shape[3]
    nblk = 2 * gd // LANES
    h = _rms_rows(x_ref[0].reshape(rows, d), g_ref[...]).astype(BF16)
    cs = cs_ref[...]
    gp = STAGE1_GROUPS
    for g0 in range(0, FFT_GROUPS, gp):
        slot = (g0 // gp) % 2
        for gi in range(gp):
            g = g0 + gi
            r = jnp.dot(h[:, g * gd:(g + 1) * gd], cs, preferred_element_type=F32)
            for c in range(nblk):
                zs_ref[slot, gi * nblk + c] = r[:, c * LANES:(c + 1) * LANES]
        for j in range(SUBLANES):
            parts = [zs_ref[slot, c, pl.ds(j, n1, stride=SUBLANES), :] for c in range(gp * nblk)]
            zr = jnp.concatenate(
                [parts[gi * nblk + c] for gi in range(gp) for c in range(nblk // 2)], axis=1)
            zi = jnp.concatenate(
                [parts[gi * nblk + nblk // 2 + c] for gi in range(gp) for c in range(nblk // 2)], axis=1)
            zc = jnp.concatenate([zr, zi], axis=0).astype(BF16)
            y = jnp.dot(tab_ref[j], zc, preferred_element_type=F32)
            o_ref[0, j, :, g0 * gd:(g0 + gp) * gd] = _pack_complex(y[:n1], y[n1:])


def _fourier_stage1(x4, gain, cs, tab):
    b, n1, n2, d = x4.shape
    gd = d // FFT_GROUPS
    rows = n1 * SUBLANES
    return pl.pallas_call(
        functools.partial(_fourier_stage1_kernel, gd=gd),
        grid=(b, n2 // SUBLANES),
        in_specs=[
            pl.BlockSpec((1, n1, SUBLANES, d), lambda bi, j: (bi, 0, j, 0)),
            pl.BlockSpec((1, d), lambda bi, j: (0, 0)),
            pl.BlockSpec((gd, 2 * gd), lambda bi, j: (0, 0)),
            pl.BlockSpec((SUBLANES, 2 * n1, 2 * n1), lambda bi, j: (j, 0, 0)),
        ],
        out_specs=pl.BlockSpec((1, SUBLANES, n1, d), lambda bi, j: (bi, j, 0, 0)),
        out_shape=jax.ShapeDtypeStruct((b, n2, n1, d), jnp.uint32),
        scratch_shapes=[pltpu.VMEM((2, STAGE1_GROUPS * 2 * gd // LANES, rows, LANES), F32)],
        compiler_params=_cparams(("parallel", "parallel")),
        name="fourier_stage1",
    )(x4, gain, cs, tab)


def _fourier_stage2_kernel(y_ref, tab_ref, o_ref, ys_ref, fs_ref, *, cw):
    n2 = y_ref.shape[1]
    rows = n2 * SUBLANES
    d = y_ref.shape[3]
    nblk = cw // LANES
    tab = tab_ref[...]
    for ch in range(d // cw):
        slot = ch % 2
        for c in range(nblk):
            lanes = slice(ch * cw + c * LANES, ch * cw + (c + 1) * LANES)
            ys_ref[slot, c] = y_ref[0, :, :, lanes].reshape(rows, LANES)
        for k in range(SUBLANES):
            u = jnp.concatenate(
                [ys_ref[slot, c, pl.ds(k, n2, stride=SUBLANES), :] for c in range(nblk)], axis=1)
            zc = pltpu.bitcast(u, BF16)
            f = jnp.dot(tab, zc, preferred_element_type=F32)
            for c in range(nblk):
                fs_ref[slot, c, pl.ds(k, n2, stride=SUBLANES), :] = f[:, c * LANES:(c + 1) * LANES]
        for c in range(nblk):
            lanes = slice(ch * cw + c * LANES, ch * cw + (c + 1) * LANES)
            o_ref[0, :, :, lanes] = fs_ref[slot, c].reshape(n2, SUBLANES, LANES).astype(o_ref.dtype)


def _fourier_stage2(y, tab):
    b, n2, n1, d = y.shape
    cw = 2 * MXU_DIM
    rows = n2 * SUBLANES
    return pl.pallas_call(
        functools.partial(_fourier_stage2_kernel, cw=cw),
        grid=(b, n1 // SUBLANES),
        in_specs=[
            pl.BlockSpec((1, n2, SUBLANES, d), lambda bi, k: (bi, 0, k, 0)),
            pl.BlockSpec((n2, 2 * n2), lambda bi, k: (0, 0)),
        ],
        out_specs=pl.BlockSpec((1, n2, SUBLANES, d), lambda bi, k: (bi, 0, k, 0)),
        out_shape=jax.ShapeDtypeStruct((b, n2, n1, d), BF16),
        scratch_shapes=[pltpu.VMEM((2, cw // LANES, rows, LANES), jnp.uint32),
                        pltpu.VMEM((2, cw // LANES, rows, LANES), F32)],
        compiler_params=_cparams(("parallel", "parallel")),
        name="fourier_stage2",
    )(y, tab)


def _proj_norm_kernel(a_ref, x_ref, w_ref, g_ref, x1_ref, h_ref):
    x1 = x_ref[...] + jnp.dot(a_ref[...].astype(BF16), w_ref[...], preferred_element_type=F32)
    x1_ref[...] = x1
    h_ref[...] = _rms_rows(x1, g_ref[...]).astype(BF16)


def _proj_norm(a, x2d, w, gain, tm):
    t, d = x2d.shape
    return pl.pallas_call(
        _proj_norm_kernel,
        grid=(t // tm,),
        in_specs=[
            pl.BlockSpec((tm, d), lambda i: (i, 0)),
            pl.BlockSpec((tm, d), lambda i: (i, 0)),
            pl.BlockSpec((None, d, d), lambda i: (0, 0, 0)),
            pl.BlockSpec((1, d), lambda i: (0, 0)),
        ],
        out_specs=[
            pl.BlockSpec((tm, d), lambda i: (i, 0)),
            pl.BlockSpec((tm, d), lambda i: (i, 0)),
        ],
        out_shape=[jax.ShapeDtypeStruct((t, d), F32), jax.ShapeDtypeStruct((t, d), BF16)],
        compiler_params=_cparams(("parallel",)),
        name="proj_norm",
    )(a, x2d, w, gain)


def _mlp_kernel(h_ref, x_ref, wu_hbm, wd_hbm, o_ref, wu_buf, wd_buf, sem, *, layer, tf, ts):
    i = pl.program_id(0)
    nchunk = wu_hbm.shape[-1] // tf
    assert nchunk % 2 == 0

    def chunk_copies(j, slot):
        return (
            pltpu.make_async_copy(wu_hbm.at[layer, :, pl.ds(j * tf, tf)], wu_buf.at[slot], sem.at[0, slot]),
            pltpu.make_async_copy(wd_hbm.at[layer, pl.ds(j * tf, tf), :], wd_buf.at[slot], sem.at[1, slot]),
        )

    def start(j, slot):
        for c in chunk_copies(j, slot):
            c.start()

    @pl.when(i == 0)
    def _():
        start(0, 0)

    o_ref[...] = x_ref[...]
    per_iter = min(MLP_CHUNKS_PER_ITER, nchunk)
    niter = nchunk // per_iter
    assert per_iter % 2 == 0 and niter * per_iter == nchunk

    def body(jj, carry):
        for u in range(per_iter):
            j = per_iter * jj + u
            slot = u % 2
            for c in chunk_copies(j, slot):
                c.wait()
            if u + 1 < per_iter:
                start(j + 1, 1 - slot)
            else:
                @pl.when(jj + 1 < niter)
                def _():
                    start(j + 1, 0)

                @pl.when(jnp.logical_and(jj + 1 == niter, i + 1 < pl.num_programs(0)))
                def _():
                    start(0, 0)
            for s in range(tf // ts):
                a = jnp.maximum(
                    jnp.dot(h_ref[...], wu_buf[slot, :, s * ts:(s + 1) * ts], preferred_element_type=F32),
                    0.0)
                o_ref[...] += jnp.dot((a * a).astype(BF16), wd_buf[slot, s * ts:(s + 1) * ts, :],
                                      preferred_element_type=F32)
        return carry

    lax.fori_loop(0, niter, body, 0)


def _mlp(h, x1, w_up, w_down, layer, tm, tf):
    t, d = x1.shape
    ts = min(tf, 2 * MXU_DIM)
    return pl.pallas_call(
        functools.partial(_mlp_kernel, layer=layer, tf=tf, ts=ts),
        grid=(t // tm,),
        in_specs=[
            pl.BlockSpec((tm, d), lambda i: (i, 0)),
            pl.BlockSpec((tm, d), lambda i: (i, 0)),
            pl.BlockSpec(memory_space=pl.ANY),
            pl.BlockSpec(memory_space=pl.ANY),
        ],
        out_specs=pl.BlockSpec((tm, d), lambda i: (i, 0)),
        out_shape=jax.ShapeDtypeStruct((t, d), F32),
        scratch_shapes=[
            pltpu.VMEM((2, d, tf), BF16),
            pltpu.VMEM((2, tf, d), BF16),
            pltpu.SemaphoreType.DMA((2, 2)),
        ],
        compiler_params=_cparams(("arbitrary",)),
        name="sqrelu_mlp",
    )(h, x1, w_up, w_down)


def _qkv_kernel(x_ref, g_ref, w_ref, e_ref, qg_ref, kg_ref, cf_ref, s1_ref, s2_ref,
                q_ref, k_ref, v_ref, *, d, scale):
    h = _rms_rows(x_ref[...], g_ref[...]).astype(BF16)
    e = e_ref[...]
    cf, s1, s2 = cf_ref[...], s1_ref[...], s2_ref[...]
    kvd = N_KV * HEAD_DIM
    lo = lax.broadcasted_iota(jnp.int32, (h.shape[0], LANES), 1) < HEAD_DIM

    def proj(c0, width):
        return jnp.dot(h, w_ref[:, c0:c0 + width], preferred_element_type=F32)

    def head_norm_rope(blk, gain, out_scale):
        ss = jnp.dot((blk * blk).astype(BF16), e, preferred_element_type=F32)
        n = blk * lax.rsqrt(ss * (1.0 / HEAD_DIM) + EPS) * gain
        outs = []
        for u in range(MXU_DIM // LANES):
            xb = n[:, u * LANES:(u + 1) * LANES]
            r = (xb * cf + pltpu.roll(xb, LANES - ROT_HALF, axis=1) * s1
                 + pltpu.roll(xb, ROT_HALF, axis=1) * s2)
            outs.append(r * out_scale if out_scale != 1.0 else r)
        return outs

    def store_duplicated(ref, halves):
        for u, xb in enumerate(halves):
            xr = pltpu.roll(xb, HEAD_DIM, axis=1)
            ref[:, (2 * u) * LANES:(2 * u + 1) * LANES] = jnp.where(lo, xb, xr).astype(BF16)
            ref[:, (2 * u + 1) * LANES:(2 * u + 2) * LANES] = jnp.where(lo, xr, xb).astype(BF16)

    pw = 2 * MXU_DIM
    for c in range(d // pw):
        blk = proj(c * pw, pw)
        outs = (head_norm_rope(blk[:, :MXU_DIM], qg_ref[...], scale)
                + head_norm_rope(blk[:, MXU_DIM:], qg_ref[...], scale))
        q_ref[:, c * pw:(c + 1) * pw] = jnp.concatenate(outs, axis=1).astype(BF16)
    assert kvd == MXU_DIM
    kv = proj(d, 2 * kvd)
    store_duplicated(k_ref, head_norm_rope(kv[:, :kvd], kg_ref[...], 1.0))
    store_duplicated(v_ref, [kv[:, kvd + u * LANES:kvd + (u + 1) * LANES] for u in range(kvd // LANES)])


def _qkv(x2d, gain, w, e, qg, kg, cf, s1, s2, seq, tm):
    t, d = x2d.shape
    kd = N_KV * LANES
    nt = seq // tm
    tab_spec = pl.BlockSpec((tm, LANES), lambda i: (i % nt, 0))
    const = lambda shape: pl.BlockSpec(shape, lambda i: (0, 0))
    return pl.pallas_call(
        functools.partial(_qkv_kernel, d=d, scale=LOG2E / math.sqrt(HEAD_DIM)),
        grid=(t // tm,),
        in_specs=[
            pl.BlockSpec((tm, d), lambda i: (i, 0)),
            const((1, d)),
            pl.BlockSpec((None,) + w.shape[1:], lambda i: (0, 0, 0)),
            const((MXU_DIM, MXU_DIM)),
            const((1, MXU_DIM)),
            const((1, MXU_DIM)),
            tab_spec, tab_spec, tab_spec,
        ],
        out_specs=[
            pl.BlockSpec((tm, d), lambda i: (i, 0)),
            pl.BlockSpec((tm, kd), lambda i: (i, 0)),
            pl.BlockSpec((tm, kd), lambda i: (i, 0)),
        ],
        out_shape=[
            jax.ShapeDtypeStruct((t, d), BF16),
            jax.ShapeDtypeStruct((t, kd), BF16),
            jax.ShapeDtypeStruct((t, kd), BF16),
        ],
        compiler_params=_cparams(("parallel",)),
        name="qkv_norm_rope",
    )(x2d, gain, w, e, qg, kg, cf, s1, s2)


def _attn_kernel(sink_ref, q_ref, kp_ref, kc_ref, kn_ref, vp_ref, vc_ref, vn_ref, x_ref, w_ref, g_ref,
                 x1_ref, h_ref, o_scr, *, nb, nsub):
    j = pl.program_id(1)
    bq = WINDOW_BLOCK
    npair = GROUP // 2
    nq = GROUP * bq
    lo = lax.broadcasted_iota(jnp.int32, (bq, LANES), 1) < HEAD_DIM
    key = lax.broadcasted_iota(jnp.int32, (bq, nq), 0)
    qry = lax.broadcasted_iota(jnp.int32, (bq, nq), 1) % bq
    zero = jnp.zeros((bq, LANES), BF16)
    xr = BF16_ROWS
    lo_x = lax.broadcasted_iota(jnp.int32, (3 * bq + xr, LANES), 1) < HEAD_DIM
    sink_row = lax.broadcasted_iota(jnp.int32, (xr, nq), 0) == 0
    vzero = jnp.zeros((xr, LANES), BF16)
    nt = (((1,), (1,)), ((), ()))
    tn = (((0,), (0,)), ((), ()))

    def kv_blocks(prev_ref, cur_ref, next_ref, u, lanes):
        rows = lambda t: slice(t * bq, (t + 1) * bq)
        before = prev_ref[0, :, lanes] if u == 0 else cur_ref[0, rows(u - 1), lanes]
        after = next_ref[0, :, lanes] if u == nsub - 1 else cur_ref[0, rows(u + 1), lanes]
        return [before, cur_ref[0, rows(u), lanes], after]

    for u in range(nsub):
        blk_idx = j * nsub + u
        vprev = (key >= qry) & (blk_idx > 0)
        vnext = (key <= qry) & (blk_idx < nb - 1)
        qrows = slice(u * bq, (u + 1) * bq)
        for h in range(N_KV):
            ks = slice(h * LANES, (h + 1) * LANES)
            k2 = jnp.concatenate(kv_blocks(kp_ref, kc_ref, kn_ref, u, ks), axis=0)
            v2x = jnp.concatenate(kv_blocks(vp_ref, vc_ref, vn_ref, u, ks) + [vzero], axis=0)
            qparts, sinks = [], []
            for parity in range(2):
                for p in range(npair):
                    c0 = (h * npair + p) * LANES
                    blk = q_ref[0, qrows, c0:c0 + LANES]
                    qparts.append(jnp.where(lo if parity == 0 else ~lo, blk, zero))
                    sinks.append(jnp.full((1, bq), sink_ref[h * GROUP + 2 * p + parity] * LOG2E, F32))
            qs = jnp.concatenate(qparts, axis=0)
            sink = jnp.concatenate(sinks, axis=1)
            st = lax.dot_general(k2, qs, nt, preferred_element_type=F32)
            sp = jnp.where(vprev, st[:bq], NEG)
            sc = st[bq:2 * bq]
            sn = jnp.where(vnext, st[2 * bq:], NEG)
            m = jnp.max(jnp.maximum(jnp.maximum(sp, sc), sn), axis=0, keepdims=True)
            m = jnp.maximum(m, sink)
            es = jnp.where(sink_row, jnp.exp2(sink - m), 0.0)
            pt = jnp.concatenate(
                [jnp.exp2(sp - m), jnp.exp2(sc - m), jnp.exp2(sn - m), es], axis=0).astype(BF16)
            ot = lax.dot_general(jnp.where(lo_x, v2x, 1.0), pt, tn, preferred_element_type=F32)
            res = ot[:HEAD_DIM] * (1.0 / ot[HEAD_DIM:HEAD_DIM + 1])
            half = nq // 2
            for p in range(npair):
                pair = jnp.concatenate(
                    [res[:, p * bq:(p + 1) * bq], res[:, half + p * bq:half + (p + 1) * bq]], axis=0)
                c0 = (h * npair + p) * LANES
                o_scr[qrows, c0:c0 + LANES] = pair.T.astype(BF16)

    x1 = x_ref[0] + jnp.dot(o_scr[...], w_ref[...], preferred_element_type=F32)
    x1_ref[0] = x1
    h_ref[0] = _rms_rows(x1, g_ref[...]).astype(BF16)


def _attention(sink, q, k, v, x, w_o, gain):
    b, s, d = q.shape
    kd = k.shape[-1]
    bq = WINDOW_BLOCK
    nb = s // bq
    nsub = ATTN_SUB if nb % ATTN_SUB == 0 else 1
    prev = lambda bi, j: (bi, jnp.maximum(j * nsub - 1, 0), 0)
    cur = lambda bi, j: (bi, j, 0)
    nxt = lambda bi, j: (bi, jnp.minimum((j + 1) * nsub, nb - 1), 0)
    edge = lambda im: pl.BlockSpec((1, bq, kd), im)
    own = pl.BlockSpec((1, nsub * bq, kd), cur)
    tile = pl.BlockSpec((1, nsub * bq, d), cur)
    return pl.pallas_call(
        functools.partial(_attn_kernel, nb=nb, nsub=nsub),
        grid=(b, nb // nsub),
        in_specs=[
            pl.BlockSpec(memory_space=pltpu.SMEM),
            tile,
            edge(prev), own, edge(nxt), edge(prev), own, edge(nxt),
            tile,
            pl.BlockSpec((None, d, d), lambda bi, j: (0, 0, 0)),
            pl.BlockSpec((1, d), lambda bi, j: (0, 0)),
        ],
        out_specs=[tile, tile],
        out_shape=[jax.ShapeDtypeStruct((b, s, d), F32), jax.ShapeDtypeStruct((b, s, d), BF16)],
        scratch_shapes=[pltpu.VMEM((nsub * bq, d), BF16)],
        compiler_params=_cparams(("parallel", "parallel")),
        name="window_attention",
    )(sink, q, k, k, k, v, v, v, x, w_o, gain)


def _chan_table(gd):
    k = np.arange(gd)
    ang = 2.0 * np.pi * ((k[:, None] * k[None, :]) % gd) / gd
    sc = 1.0 / math.sqrt(gd)
    return jnp.asarray(np.concatenate([np.cos(ang) * sc, -np.sin(ang) * sc], axis=1), BF16)


def _stage1_table(n1, n2):
    n = n1 * n2
    k1 = np.arange(n1)
    tw = 2.0 * np.pi * ((np.arange(n2)[:, None] * k1[None, :]) % n) / n
    dft = 2.0 * np.pi * ((k1[:, None] * k1[None, :]) % n1) / n1
    ct, st = jnp.asarray(np.cos(tw), F32)[:, :, None], jnp.asarray(np.sin(tw), F32)[:, :, None]
    cd, sd = jnp.asarray(np.cos(dft), F32)[None], jnp.asarray(np.sin(dft), F32)[None]
    c, s = ct * cd - st * sd, st * cd + ct * sd
    top = jnp.concatenate([c, s], axis=-1)
    bot = jnp.concatenate([-s, c], axis=-1)
    return jnp.concatenate([top, bot], axis=-2).astype(BF16)


def _stage2_table(n1, n2):
    k = np.arange(n2)
    ang = 2.0 * np.pi * ((k[:, None] * k[None, :]) % n2) / n2
    sc = 1.0 / math.sqrt(n1 * n2)
    return jnp.asarray(np.stack([np.sin(ang) * sc, np.cos(ang) * sc], axis=-1).reshape(n2, 2 * n2), BF16)


def _rope_lane_tables(s):
    inv_freq = ROPE_THETA ** (-jnp.arange(0, ROT_DIM, 2, dtype=F32) / ROT_DIM)
    ang = jnp.arange(s, dtype=F32)[:, None] * inv_freq[None, :]
    cos, sin = jnp.cos(ang), jnp.sin(ang)
    pad = HEAD_DIM - ROT_DIM
    ones = jnp.ones((s, pad), F32)
    zeros = jnp.zeros((s, pad), F32)
    zh = jnp.zeros((s, ROT_HALF), F32)
    cf = jnp.concatenate([cos, cos, ones], axis=1)
    s1 = jnp.concatenate([-sin, zh, zeros], axis=1)
    s2 = jnp.concatenate([zh, sin, zeros], axis=1)
    rep = LANES // HEAD_DIM
    return tuple(jnp.tile(a, (1, rep)) for a in (cf, s1, s2))


def _segment_ones():
    i = np.arange(MXU_DIM) // HEAD_DIM
    return jnp.asarray((i[:, None] == i[None, :]).astype(np.float32), BF16)


def _fourier_mixer(x, gain, w_out, mlp_gain, tm):
    b, s, d = x.shape
    gd = d // FFT_GROUPS
    n1, n2 = _seq_split(s)
    y = _fourier_stage1(x.reshape(b, n1, n2, d), gain, _chan_table(gd), _stage1_table(n1, n2))
    f = _fourier_stage2(y, _stage2_table(n1, n2))
    return _proj_norm(f.reshape(b * s, d), x.reshape(b * s, d), w_out, mlp_gain, tm)


def _attn_mixer(x2d, b, s, gain, w_qkv, q_gain, k_gain, sink, w_o, mlp_gain, tm):
    t, d = x2d.shape
    rep = MXU_DIM // HEAD_DIM
    cf, s1, s2 = _rope_lane_tables(s)
    q, k, v = _qkv(x2d, gain, w_qkv, _segment_ones(),
                   jnp.tile(q_gain, rep)[None, :], jnp.tile(k_gain, rep)[None, :],
                   cf, s1, s2, s, tm)
    kd = k.shape[-1]
    x1, h = _attention(sink, q.reshape(b, s, d), k.reshape(b, s, kd), v.reshape(b, s, kd),
                       x2d.reshape(b, s, d), w_o, mlp_gain)
    return x1.reshape(t, d), h.reshape(t, d)


def _tile_sizes(batch, seq, dff):
    tm = min(2 * MXU_DIM, seq)
    tm_mlp = min(4 * MXU_DIM, batch * seq)
    tf = min(4 * MXU_DIM, dff // 2)
    return tm, tm_mlp, tf


def _trunk(x, p, tm, tm_mlp, tf):
    b, s, d = x.shape
    x1, h = _fourier_mixer(x, p["fourier_norm"][0][None, :], p["fourier_w_out"],
                           p["mlp_norm"][0][None, :], tm)
    x2 = _mlp(h, x1, p["mlp_w_up"], p["mlp_w_down"], 0, tm_mlp, tf)
    x3, h = _attn_mixer(x2, b, s, p["attn_norm"][0][None, :], p["attn_w_qkv"], p["attn_q_norm"][0],
                        p["attn_k_norm"][0], p["attn_sink"][0], p["attn_w_o"],
                        p["mlp_norm"][1][None, :], tm)
    x4 = _mlp(h, x3, p["mlp_w_up"], p["mlp_w_down"], 1, tm_mlp, tf)
    return x4.reshape(b, s, d)


def kernel(x_prompt, x_sample, fourier_norm, fourier_w_out, attn_norm, attn_w_qkv, attn_q_norm,
           attn_k_norm, attn_sink, attn_w_o, mlp_norm, mlp_w_up, mlp_w_down):
    p = dict(
        fourier_norm=fourier_norm, fourier_w_out=fourier_w_out.astype(BF16),
        attn_norm=attn_norm, attn_w_qkv=attn_w_qkv.astype(BF16), attn_q_norm=attn_q_norm,
        attn_k_norm=attn_k_norm, attn_sink=attn_sink, attn_w_o=attn_w_o.astype(BF16),
        mlp_norm=mlp_norm, mlp_w_up=mlp_w_up.astype(BF16), mlp_w_down=mlp_w_down.astype(BF16),
    )
    dff = mlp_w_up.shape[-1]
    return tuple(_trunk(x, p, *_tile_sizes(x.shape[0], x.shape[1], dff)) for x in (x_prompt, x_sample))
```

```python
import functools
import math

import numpy as np
import jax
import jax.numpy as jnp
from jax import lax
from jax.experimental import pallas as pl
from jax.experimental.pallas import tpu as pltpu

F32 = jnp.float32
BF16 = jnp.bfloat16

FFT_GROUPS = 8
HEAD_DIM = 64
N_KV = 4
GROUP = 8
WINDOW_BLOCK = 128
ROPE_THETA = 500000.0
ROT_DIM = HEAD_DIM // 4
ROT_HALF = ROT_DIM // 2
EPS = 1e-6
NEG = -1e30
LOG2E = math.log2(math.e)
ATTN_SUB = 4
MLP_CHUNKS_PER_ITER = 2
LANES = 128
MXU_DIM = 256
VMEM_LIMIT = 60 * 1024 * 1024


def _cparams(sem):
    return pltpu.CompilerParams(dimension_semantics=sem, vmem_limit_bytes=VMEM_LIMIT)


def _rms_rows(x, gain):
    ms = jnp.mean(x * x, axis=-1, keepdims=True)
    return x * lax.rsqrt(ms + EPS) * gain


def _seq_split(s):
    n1 = 1 << ((s.bit_length() - 1 + 1) // 2)
    n2 = s // n1
    assert n1 * n2 == s
    return n1, n2


SUBLANES = 8
BF16_ROWS = 2 * SUBLANES
STAGE1_GROUPS = 2


def _fourier_stage1_kernel(x_ref, g_ref, cs_ref, tab_ref, o_ref, zs_ref, *, gd):
    n1 = x_ref.shape[1]
    rows = n1 * SUBLANES
    d = x_ref.shape[3]
    nblk = 2 * gd // LANES
    h = _rms_rows(x_ref[0].reshape(rows, d), g_ref[...]).astype(BF16)
    cs = cs_ref[...]
    gp = STAGE1_GROUPS
    for g0 in range(0, FFT_GROUPS, gp):
        slot = (g0 // gp) % 2
        for gi in range(gp):
            g = g0 + gi
            r = jnp.dot(h[:, g * gd:(g + 1) * gd], cs, preferred_element_type=F32)
            for c in range(nblk):
                zs_ref[slot, gi * nblk + c] = r[:, c * LANES:(c + 1) * LANES]
        for j in range(SUBLANES):
            parts = [zs_ref[slot, c, pl.ds(j, n1, stride=SUBLANES), :] for c in range(gp * nblk)]
            zr = jnp.concatenate(
                [parts[gi * nblk + c] for gi in range(gp) for c in range(nblk // 2)], axis=1)
            zi = jnp.concatenate(
                [parts[gi * nblk + nblk // 2 + c] for gi in range(gp) for c in range(nblk // 2)], axis=1)
            zc = jnp.concatenate([zr, zi], axis=0).astype(BF16)
            y = jnp.dot(tab_ref[j], zc, preferred_element_type=F32)
            o_ref[0, j, :, g0 * gd:(g0 + gp) * gd] = pltpu.bitcast(y.astype(BF16), jnp.uint32)


def _fourier_stage1(x4, gain, cs, tab):
    b, n1, n2, d = x4.shape
    gd = d // FFT_GROUPS
    rows = n1 * SUBLANES
    return pl.pallas_call(
        functools.partial(_fourier_stage1_kernel, gd=gd),
        grid=(b, n2 // SUBLANES),
        in_specs=[
            pl.BlockSpec((1, n1, SUBLANES, d), lambda bi, j: (bi, 0, j, 0)),
            pl.BlockSpec((1, d), lambda bi, j: (0, 0)),
            pl.BlockSpec((gd, 2 * gd), lambda bi, j: (0, 0)),
            pl.BlockSpec((SUBLANES, 2 * n1, 2 * n1), lambda bi, j: (j, 0, 0)),
        ],
        out_specs=pl.BlockSpec((1, SUBLANES, n1, d), lambda bi, j: (bi, j, 0, 0)),
        out_shape=jax.ShapeDtypeStruct((b, n2, n1, d), jnp.uint32),
        scratch_shapes=[pltpu.VMEM((2, STAGE1_GROUPS * 2 * gd // LANES, rows, LANES), F32)],
        compiler_params=_cparams(("parallel", "parallel")),
        name="fourier_stage1",
    )(x4, gain, cs, tab)


def _fourier_stage2_kernel(y_ref, tab_ref, o_ref, ys_ref, fs_ref, *, cw):
    n2 = y_ref.shape[1]
    rows = n2 * SUBLANES
    d = y_ref.shape[3]
    nblk = cw // LANES
    tab = tab_ref[...]
    for ch in range(d // cw):
        slot = ch % 2
        for c in range(nblk):
            lanes = slice(ch * cw + c * LANES, ch * cw + (c + 1) * LANES)
            ys_ref[slot, c] = y_ref[0, :, :, lanes].reshape(rows, LANES)
        for k in range(SUBLANES):
            u = jnp.concatenate(
                [ys_ref[slot, c, pl.ds(k, n2, stride=SUBLANES), :] for c in range(nblk)], axis=1)
            zc = pltpu.bitcast(u, BF16)
            f = jnp.dot(tab, zc, preferred_element_type=F32)
            for c in range(nblk):
                fs_ref[slot, c, pl.ds(k, n2, stride=SUBLANES), :] = f[:, c * LANES:(c + 1) * LANES]
        for c in range(nblk):
            lanes = slice(ch * cw + c * LANES, ch * cw + (c + 1) * LANES)
            o_ref[0, :, :, lanes] = fs_ref[slot, c].reshape(n2, SUBLANES, LANES).astype(o_ref.dtype)


def _fourier_stage2(y, tab):
    b, n2, n1, d = y.shape
    cw = 2 * MXU_DIM
    rows = n2 * SUBLANES
    return pl.pallas_call(
        functools.partial(_fourier_stage2_kernel, cw=cw),
        grid=(b, n1 // SUBLANES),
        in_specs=[
            pl.BlockSpec((1, n2, SUBLANES, d), lambda bi, k: (bi, 0, k, 0)),
            pl.BlockSpec((n2, 2 * n2), lambda bi, k: (0, 0)),
        ],
        out_specs=pl.BlockSpec((1, n2, SUBLANES, d), lambda bi, k: (bi, 0, k, 0)),
        out_shape=jax.ShapeDtypeStruct((b, n2, n1, d), BF16),
        scratch_shapes=[pltpu.VMEM((2, cw // LANES, rows, LANES), jnp.uint32),
                        pltpu.VMEM((2, cw // LANES, rows, LANES), F32)],
        compiler_params=_cparams(("parallel", "parallel")),
        name="fourier_stage2",
    )(y, tab)


def _proj_norm_kernel(a_ref, x_ref, w_ref, g_ref, x1_ref, h_ref):
    x1 = x_ref[...] + jnp.dot(a_ref[...].astype(BF16), w_ref[...], preferred_element_type=F32)
    x1_ref[...] = x1
    h_ref[...] = _rms_rows(x1, g_ref[...]).astype(BF16)


def _proj_norm(a, x2d, w, gain, tm):
    t, d = x2d.shape
    return pl.pallas_call(
        _proj_norm_kernel,
        grid=(t // tm,),
        in_specs=[
            pl.BlockSpec((tm, d), lambda i: (i, 0)),
            pl.BlockSpec((tm, d), lambda i: (i, 0)),
            pl.BlockSpec((None, d, d), lambda i: (0, 0, 0)),
            pl.BlockSpec((1, d), lambda i: (0, 0)),
        ],
        out_specs=[
            pl.BlockSpec((tm, d), lambda i: (i, 0)),
            pl.BlockSpec((tm, d), lambda i: (i, 0)),
        ],
        out_shape=[jax.ShapeDtypeStruct((t, d), F32), jax.ShapeDtypeStruct((t, d), BF16)],
        compiler_params=_cparams(("parallel",)),
        name="proj_norm",
    )(a, x2d, w, gain)


def _mlp_kernel(h_ref, x_ref, wu_hbm, wd_hbm, o_ref, wu_buf, wd_buf, sem, *, layer, tf, ts):
    i = pl.program_id(0)
    nchunk = wu_hbm.shape[-1] // tf
    assert nchunk % 2 == 0

    def chunk_copies(j, slot):
        return (
            pltpu.make_async_copy(wu_hbm.at[layer, :, pl.ds(j * tf, tf)], wu_buf.at[slot], sem.at[0, slot]),
            pltpu.make_async_copy(wd_hbm.at[layer, pl.ds(j * tf, tf), :], wd_buf.at[slot], sem.at[1, slot]),
        )

    def start(j, slot):
        for c in chunk_copies(j, slot):
            c.start()

    @pl.when(i == 0)
    def _():
        start(0, 0)

    o_ref[...] = x_ref[...]
    per_iter = min(MLP_CHUNKS_PER_ITER, nchunk)
    niter = nchunk // per_iter
    assert per_iter % 2 == 0 and niter * per_iter == nchunk

    def body(jj, carry):
        for u in range(per_iter):
            j = per_iter * jj + u
            slot = u % 2
            for c in chunk_copies(j, slot):
                c.wait()
            if u + 1 < per_iter:
                start(j + 1, 1 - slot)
            else:
                @pl.when(jj + 1 < niter)
                def _():
                    start(j + 1, 0)

                @pl.when(jnp.logical_and(jj + 1 == niter, i + 1 < pl.num_programs(0)))
                def _():
                    start(0, 0)
            for s in range(tf // ts):
                a = jnp.maximum(
                    jnp.dot(h_ref[...], wu_buf[slot, :, s * ts:(s + 1) * ts], preferred_element_type=F32),
                    0.0)
                o_ref[...] += jnp.dot((a * a).astype(BF16), wd_buf[slot, s * ts:(s + 1) * ts, :],
                                      preferred_element_type=F32)
        return carry

    lax.fori_loop(0, niter, body, 0)


def _mlp(h, x1, w_up, w_down, layer, tm, tf):
    t, d = x1.shape
    ts = min(tf, 2 * MXU_DIM)
    return pl.pallas_call(
        functools.partial(_mlp_kernel, layer=layer, tf=tf, ts=ts),
        grid=(t // tm,),
        in_specs=[
            pl.BlockSpec((tm, d), lambda i: (i, 0)),
            pl.BlockSpec((tm, d), lambda i: (i, 0)),
            pl.BlockSpec(memory_space=pl.ANY),
            pl.BlockSpec(memory_space=pl.ANY),
        ],
        out_specs=pl.BlockSpec((tm, d), lambda i: (i, 0)),
        out_shape=jax.ShapeDtypeStruct((t, d), F32),
        scratch_shapes=[
            pltpu.VMEM((2, d, tf), BF16),
            pltpu.VMEM((2, tf, d), BF16),
            pltpu.SemaphoreType.DMA((2, 2)),
        ],
        compiler_params=_cparams(("arbitrary",)),
        name="sqrelu_mlp",
    )(h, x1, w_up, w_down)


def _qkv_kernel(x_ref, g_ref, w_ref, e_ref, qg_ref, kg_ref, cf_ref, s1_ref, s2_ref,
                q_ref, k_ref, v_ref, *, d, scale):
    h = _rms_rows(x_ref[...], g_ref[...]).astype(BF16)
    e = e_ref[...]
    cf, s1, s2 = cf_ref[...], s1_ref[...], s2_ref[...]
    kvd = N_KV * HEAD_DIM
    lo = lax.broadcasted_iota(jnp.int32, (h.shape[0], LANES), 1) < HEAD_DIM

    def proj(c0, width):
        return jnp.dot(h, w_ref[:, c0:c0 + width], preferred_element_type=F32)

    def head_norm_rope(blk, gain, out_scale):
        ss = jnp.dot((blk * blk).astype(BF16), e, preferred_element_type=F32)
        n = blk * lax.rsqrt(ss * (1.0 / HEAD_DIM) + EPS) * gain
        outs = []
        for u in range(MXU_DIM // LANES):
            xb = n[:, u * LANES:(u + 1) * LANES]
            r = (xb * cf + pltpu.roll(xb, LANES - ROT_HALF, axis=1) * s1
                 + pltpu.roll(xb, ROT_HALF, axis=1) * s2)
            outs.append(r * out_scale if out_scale != 1.0 else r)
        return outs

    def store_duplicated(ref, halves):
        for u, xb in enumerate(halves):
            xr = pltpu.roll(xb, HEAD_DIM, axis=1)
            ref[:, (2 * u) * LANES:(2 * u + 1) * LANES] = jnp.where(lo, xb, xr).astype(BF16)
            ref[:, (2 * u + 1) * LANES:(2 * u + 2) * LANES] = jnp.where(lo, xr, xb).astype(BF16)

    pw = 2 * MXU_DIM
    for c in range(d // pw):
        blk = proj(c * pw, pw)
        outs = (head_norm_rope(blk[:, :MXU_DIM], qg_ref[...], scale)
                + head_norm_rope(blk[:, MXU_DIM:], qg_ref[...], scale))
        q_ref[:, c * pw:(c + 1) * pw] = jnp.concatenate(outs, axis=1).astype(BF16)
    assert kvd == MXU_DIM
    kv = proj(d, 2 * kvd)
    store_duplicated(k_ref, head_norm_rope(kv[:, :kvd], kg_ref[...], 1.0))
    store_duplicated(v_ref, [kv[:, kvd + u * LANES:kvd + (u + 1) * LANES] for u in range(kvd // LANES)])


def _qkv(x2d, gain, w, e, qg, kg, cf, s1, s2, seq, tm):
    t, d = x2d.shape
    kd = N_KV * LANES
    nt = seq // tm
    tab_spec = pl.BlockSpec((tm, LANES), lambda i: (i % nt, 0))
    const = lambda shape: pl.BlockSpec(shape, lambda i: (0, 0))
    return pl.pallas_call(
        functools.partial(_qkv_kernel, d=d, scale=LOG2E / math.sqrt(HEAD_DIM)),
        grid=(t // tm,),
        in_specs=[
            pl.BlockSpec((tm, d), lambda i: (i, 0)),
            const((1, d)),
            pl.BlockSpec((None,) + w.shape[1:], lambda i: (0, 0, 0)),
            const((MXU_DIM, MXU_DIM)),
            const((1, MXU_DIM)),
            const((1, MXU_DIM)),
            tab_spec, tab_spec, tab_spec,
        ],
        out_specs=[
            pl.BlockSpec((tm, d), lambda i: (i, 0)),
            pl.BlockSpec((tm, kd), lambda i: (i, 0)),
            pl.BlockSpec((tm, kd), lambda i: (i, 0)),
        ],
        out_shape=[
            jax.ShapeDtypeStruct((t, d), BF16),
            jax.ShapeDtypeStruct((t, kd), BF16),
            jax.ShapeDtypeStruct((t, kd), BF16),
        ],
        compiler_params=_cparams(("parallel",)),
        name="qkv_norm_rope",
    )(x2d, gain, w, e, qg, kg, cf, s1, s2)


def _attn_kernel(sink_ref, q_ref, kp_ref, kc_ref, kn_ref, vp_ref, vc_ref, vn_ref, x_ref, w_ref, g_ref,
                 x1_ref, h_ref, o_scr, *, nb, nsub):
    j = pl.program_id(1)
    bq = WINDOW_BLOCK
    npair = GROUP // 2
    nq = GROUP * bq
    lo = lax.broadcasted_iota(jnp.int32, (bq, LANES), 1) < HEAD_DIM
    key = lax.broadcasted_iota(jnp.int32, (bq, nq), 0)
    qry = lax.broadcasted_iota(jnp.int32, (bq, nq), 1) % bq
    zero = jnp.zeros((bq, LANES), BF16)
    xr = BF16_ROWS
    lo_x = lax.broadcasted_iota(jnp.int32, (3 * bq + xr, LANES), 1) < HEAD_DIM
    sink_row = lax.broadcasted_iota(jnp.int32, (xr, nq), 0) == 0
    vzero = jnp.zeros((xr, LANES), BF16)
    nt = (((1,), (1,)), ((), ()))
    tn = (((0,), (0,)), ((), ()))

    def kv_blocks(prev_ref, cur_ref, next_ref, u, lanes):
        rows = lambda t: slice(t * bq, (t + 1) * bq)
        before = prev_ref[0, :, lanes] if u == 0 else cur_ref[0, rows(u - 1), lanes]
        after = next_ref[0, :, lanes] if u == nsub - 1 else cur_ref[0, rows(u + 1), lanes]
        return [before, cur_ref[0, rows(u), lanes], after]

    for u in range(nsub):
        blk_idx = j * nsub + u
        vprev = (key >= qry) & (blk_idx > 0)
        vnext = (key <= qry) & (blk_idx < nb - 1)
        qrows = slice(u * bq, (u + 1) * bq)
        for h in range(N_KV):
            ks = slice(h * LANES, (h + 1) * LANES)
            k2 = jnp.concatenate(kv_blocks(kp_ref, kc_ref, kn_ref, u, ks), axis=0)
            v2x = jnp.concatenate(kv_blocks(vp_ref, vc_ref, vn_ref, u, ks) + [vzero], axis=0)
            qparts, sinks = [], []
            for parity in range(2):
                for p in range(npair):
                    c0 = (h * npair + p) * LANES
                    blk = q_ref[0, qrows, c0:c0 + LANES]
                    qparts.append(jnp.where(lo if parity == 0 else ~lo, blk, zero))
                    sinks.append(jnp.full((1, bq), sink_ref[h * GROUP + 2 * p + parity] * LOG2E, F32))
            qs = jnp.concatenate(qparts, axis=0)
            sink = jnp.concatenate(sinks, axis=1)
            st = lax.dot_general(k2, qs, nt, preferred_element_type=F32)
            sp = jnp.where(vprev, st[:bq], NEG)
            sc = st[bq:2 * bq]
            sn = jnp.where(vnext, st[2 * bq:], NEG)
            m = jnp.max(jnp.maximum(jnp.maximum(sp, sc), sn), axis=0, keepdims=True)
            m = jnp.maximum(m, sink)
            es = jnp.where(sink_row, jnp.exp2(sink - m), 0.0)
            pt = jnp.concatenate(
                [jnp.exp2(sp - m), jnp.exp2(sc - m), jnp.exp2(sn - m), es], axis=0).astype(BF16)
            ot = lax.dot_general(jnp.where(lo_x, v2x, 1.0), pt, tn, preferred_element_type=F32)
            res = ot[:HEAD_DIM] * (1.0 / ot[HEAD_DIM:HEAD_DIM + 1])
            half = nq // 2
            for p in range(npair):
                pair = jnp.concatenate(
                    [res[:, p * bq:(p + 1) * bq], res[:, half + p * bq:half + (p + 1) * bq]], axis=0)
                c0 = (h * npair + p) * LANES
                o_scr[qrows, c0:c0 + LANES] = pair.T.astype(BF16)

    x1 = x_ref[0] + jnp.dot(o_scr[...], w_ref[...], preferred_element_type=F32)
    x1_ref[0] = x1
    h_ref[0] = _rms_rows(x1, g_ref[...]).astype(BF16)


def _attention(sink, q, k, v, x, w_o, gain):
    b, s, d = q.shape
    kd = k.shape[-1]
    bq = WINDOW_BLOCK
    nb = s // bq
    nsub = ATTN_SUB if nb % ATTN_SUB == 0 else 1
    prev = lambda bi, j: (bi, jnp.maximum(j * nsub - 1, 0), 0)
    cur = lambda bi, j: (bi, j, 0)
    nxt = lambda bi, j: (bi, jnp.minimum((j + 1) * nsub, nb - 1), 0)
    edge = lambda im: pl.BlockSpec((1, bq, kd), im)
    own = pl.BlockSpec((1, nsub * bq, kd), cur)
    tile = pl.BlockSpec((1, nsub * bq, d), cur)
    return pl.pallas_call(
        functools.partial(_attn_kernel, nb=nb, nsub=nsub),
        grid=(b, nb // nsub),
        in_specs=[
            pl.BlockSpec(memory_space=pltpu.SMEM),
            tile,
            edge(prev), own, edge(nxt), edge(prev), own, edge(nxt),
            tile,
            pl.BlockSpec((None, d, d), lambda bi, j: (0, 0, 0)),
            pl.BlockSpec((1, d), lambda bi, j: (0, 0)),
        ],
        out_specs=[tile, tile],
        out_shape=[jax.ShapeDtypeStruct((b, s, d), F32), jax.ShapeDtypeStruct((b, s, d), BF16)],
        scratch_shapes=[pltpu.VMEM((nsub * bq, d), BF16)],
        compiler_params=_cparams(("parallel", "parallel")),
        name="window_attention",
    )(sink, q, k, k, k, v, v, v, x, w_o, gain)


def _chan_table(gd):
    k = np.arange(gd)
    ang = 2.0 * np.pi * ((k[:, None] * k[None, :]) % gd) / gd
    sc = 1.0 / math.sqrt(gd)
    return jnp.asarray(np.concatenate([np.cos(ang) * sc, -np.sin(ang) * sc], axis=1), BF16)


def _stage1_table(n1, n2):
    n = n1 * n2
    k1 = np.arange(n1)
    tw = 2.0 * np.pi * ((np.arange(n2)[:, None] * k1[None, :]) % n) / n
    dft = 2.0 * np.pi * ((k1[:, None] * k1[None, :]) % n1) / n1
    ct, st = jnp.asarray(np.cos(tw), F32)[:, :, None], jnp.asarray(np.sin(tw), F32)[:, :, None]
    cd, sd = jnp.asarray(np.cos(dft), F32)[None], jnp.asarray(np.sin(dft), F32)[None]
    c, s = ct * cd - st * sd, st * cd + ct * sd
    top = jnp.concatenate([c, s], axis=-1)
    bot = jnp.concatenate([-s, c], axis=-1)
    return jnp.stack([bot, top], axis=2).reshape(n2, 2 * n1, 2 * n1).astype(BF16)


def _stage2_table(n1, n2):
    k = np.arange(n2)
    ang = 2.0 * np.pi * ((k[:, None] * k[None, :]) % n2) / n2
    sc = 1.0 / math.sqrt(n1 * n2)
    return jnp.asarray(np.stack([np.sin(ang) * sc, np.cos(ang) * sc], axis=-1).reshape(n2, 2 * n2), BF16)


def _rope_lane_tables(s):
    inv_freq = ROPE_THETA ** (-jnp.arange(0, ROT_DIM, 2, dtype=F32) / ROT_DIM)
    ang = jnp.arange(s, dtype=F32)[:, None] * inv_freq[None, :]
    cos, sin = jnp.cos(ang), jnp.sin(ang)
    pad = HEAD_DIM - ROT_DIM
    ones = jnp.ones((s, pad), F32)
    zeros = jnp.zeros((s, pad), F32)
    zh = jnp.zeros((s, ROT_HALF), F32)
    cf = jnp.concatenate([cos, cos, ones], axis=1)
    s1 = jnp.concatenate([-sin, zh, zeros], axis=1)
    s2 = jnp.concatenate([zh, sin, zeros], axis=1)
    rep = LANES // HEAD_DIM
    return tuple(jnp.tile(a, (1, rep)) for a in (cf, s1, s2))


def _segment_ones():
    i = np.arange(MXU_DIM) // HEAD_DIM
    return jnp.asarray((i[:, None] == i[None, :]).astype(np.float32), BF16)


def _fourier_mixer(x, gain, w_out, mlp_gain, tm):
    b, s, d = x.shape
    gd = d // FFT_GROUPS
    n1, n2 = _seq_split(s)
    y = _fourier_stage1(x.reshape(b, n1, n2, d), gain, _chan_table(gd), _stage1_table(n1, n2))
    f = _fourier_stage2(y, _stage2_table(n1, n2))
    return _proj_norm(f.reshape(b * s, d), x.reshape(b * s, d), w_out, mlp_gain, tm)


def _attn_mixer(x2d, b, s, gain, w_qkv, q_gain, k_gain, sink, w_o, mlp_gain, tm):
    t, d = x2d.shape
    rep = MXU_DIM // HEAD_DIM
    cf, s1, s2 = _rope_lane_tables(s)
    q, k, v = _qkv(x2d, gain, w_qkv, _segment_ones(),
                   jnp.tile(q_gain, rep)[None, :], jnp.tile(k_gain, rep)[None, :],
                   cf, s1, s2, s, tm)
    kd = k.shape[-1]
    x1, h = _attention(sink, q.reshape(b, s, d), k.reshape(b, s, kd), v.reshape(b, s, kd),
                       x2d.reshape(b, s, d), w_o, mlp_gain)
    return x1.reshape(t, d), h.reshape(t, d)


def _tile_sizes(batch, seq, dff):
    tm = min(2 * MXU_DIM, seq)
    tm_mlp = min(4 * MXU_DIM, batch * seq)
    tf = min(4 * MXU_DIM, dff // 2)
    return tm, tm_mlp, tf


def _trunk(x, p, tm, tm_mlp, tf):
    b, s, d = x.shape
    x1, h = _fourier_mixer(x, p["fourier_norm"][0][None, :], p["fourier_w_out"],
                           p["mlp_norm"][0][None, :], tm)
    x2 = _mlp(h, x1, p["mlp_w_up"], p["mlp_w_down"], 0, tm_mlp, tf)
    x3, h = _attn_mixer(x2, b, s, p["attn_norm"][0][None, :], p["attn_w_qkv"], p["attn_q_norm"][0],
                        p["attn_k_norm"][0], p["attn_sink"][0], p["attn_w_o"],
                        p["mlp_norm"][1][None, :], tm)
    x4 = _mlp(h, x3, p["mlp_w_up"], p["mlp_w_down"], 1, tm_mlp, tf)
    return x4.reshape(b, s, d)


def kernel(x_prompt, x_sample, fourier_norm, fourier_w_out, attn_norm, attn_w_qkv, attn_q_norm,
           attn_k_norm, attn_sink, attn_w_o, mlp_norm, mlp_w_up, mlp_w_down):
    p = dict(
        fourier_norm=fourier_norm, fourier_w_out=fourier_w_out.astype(BF16),
        attn_norm=attn_norm, attn_w_qkv=attn_w_qkv.astype(BF16), attn_q_norm=attn_q_norm,
        attn_k_norm=attn_k_norm, attn_sink=attn_sink, attn_w_o=attn_w_o.astype(BF16),
        mlp_norm=mlp_norm, mlp_w_up=mlp_w_up.astype(BF16), mlp_w_down=mlp_w_down.astype(BF16),
    )
    dff = mlp_w_up.shape[-1]
    return tuple(_trunk(x, p, *_tile_sizes(x.shape[0], x.shape[1], dff)) for x in (x_prompt, x_sample))
```

```python
import functools
import math

import numpy as np
import jax
import jax.numpy as jnp
from jax import lax
from jax.experimental import pallas as pl
from jax.experimental.pallas import tpu as pltpu

F32 = jnp.float32
BF16 = jnp.bfloat16

FFT_GROUPS = 8
HEAD_DIM = 64
N_KV = 4
GROUP = 8
WINDOW_BLOCK = 128
ROPE_THETA = 500000.0
ROT_DIM = HEAD_DIM // 4
ROT_HALF = ROT_DIM // 2
EPS = 1e-6
NEG = -1e30
LOG2E = math.log2(math.e)
ATTN_SUB = 4
MLP_CHUNKS_PER_ITER = 2
LANES = 128
MXU_DIM = 256
VMEM_LIMIT = 60 * 1024 * 1024


def _cparams(sem):
    return pltpu.CompilerParams(dimension_semantics=sem, vmem_limit_bytes=VMEM_LIMIT)


def _rms_rows(x, gain):
    ms = jnp.mean(x * x, axis=-1, keepdims=True)
    return x * lax.rsqrt(ms + EPS) * gain


def _seq_split(s):
    n1 = 1 << ((s.bit_length() - 1 + 1) // 2)
    n2 = s // n1
    assert n1 * n2 == s
    return n1, n2


SUBLANES = 8
BF16_ROWS = 2 * SUBLANES
STAGE1_GROUPS = 2


def _fourier_stage1_kernel(x_ref, g_ref, cs_ref, tab_ref, o_ref, zs_ref, *, gd):
    n1 = x_ref.shape[1]
    rows = n1 * SUBLANES
    d = x_ref.shape[3]
    nblk = 2 * gd // LANES
    h = _rms_rows(x_ref[0].reshape(rows, d), g_ref[...]).astype(BF16)
    cs = cs_ref[...]
    gp = STAGE1_GROUPS
    for g0 in range(0, FFT_GROUPS, gp):
        slot = (g0 // gp) % 2
        for gi in range(gp):
            g = g0 + gi
            r = jnp.dot(h[:, g * gd:(g + 1) * gd], cs, preferred_element_type=F32)
            for c in range(nblk):
                zs_ref[slot, gi * nblk + c] = r[:, c * LANES:(c + 1) * LANES]
        for j in range(SUBLANES):
            parts = [zs_ref[slot, c, pl.ds(j, n1, stride=SUBLANES), :] for c in range(gp * nblk)]
            zr = jnp.concatenate(
                [parts[gi * nblk + c] for gi in range(gp) for c in range(nblk // 2)], axis=1)
            zi = jnp.concatenate(
                [parts[gi * nblk + nblk // 2 + c] for gi in range(gp) for c in range(nblk // 2)], axis=1)
            zc = jnp.concatenate([zr, zi], axis=0).astype(BF16)
            y = jnp.dot(tab_ref[j], zc, preferred_element_type=F32)
            o_ref[0, j, :, g0 * gd:(g0 + gp) * gd] = pltpu.bitcast(y.astype(BF16), jnp.uint32)


def _fourier_stage1(x4, gain, cs, tab):
    b, n1, n2, d = x4.shape
    gd = d // FFT_GROUPS
    rows = n1 * SUBLANES
    return pl.pallas_call(
        functools.partial(_fourier_stage1_kernel, gd=gd),
        grid=(b, n2 // SUBLANES),
        in_specs=[
            pl.BlockSpec((1, n1, SUBLANES, d), lambda bi, j: (bi, 0, j, 0)),
            pl.BlockSpec((1, d), lambda bi, j: (0, 0)),
            pl.BlockSpec((gd, 2 * gd), lambda bi, j: (0, 0)),
            pl.BlockSpec((SUBLANES, 2 * n1, 2 * n1), lambda bi, j: (j, 0, 0)),
        ],
        out_specs=pl.BlockSpec((1, SUBLANES, n1, d), lambda bi, j: (bi, j, 0, 0)),
        out_shape=jax.ShapeDtypeStruct((b, n2, n1, d), jnp.uint32),
        scratch_shapes=[pltpu.VMEM((2, STAGE1_GROUPS * 2 * gd // LANES, rows, LANES), F32)],
        compiler_params=_cparams(("parallel", "parallel")),
        name="fourier_stage1",
    )(x4, gain, cs, tab)


def _fourier_stage2_kernel(y_ref, tab_ref, o_ref, ys_ref, fs_ref, *, cw):
    n2 = y_ref.shape[1]
    rows = n2 * SUBLANES
    d = y_ref.shape[3]
    nblk = cw // LANES
    tab = tab_ref[...]
    for ch in range(d // cw):
        slot = ch % 2
        for c in range(nblk):
            lanes = slice(ch * cw + c * LANES, ch * cw + (c + 1) * LANES)
            ys_ref[slot, c] = y_ref[0, :, :, lanes].reshape(rows, LANES)
        for k in range(SUBLANES):
            u = jnp.concatenate(
                [ys_ref[slot, c, pl.ds(k, n2, stride=SUBLANES), :] for c in range(nblk)], axis=1)
            zc = pltpu.bitcast(u, BF16)
            f = jnp.dot(tab, zc, preferred_element_type=F32)
            for c in range(nblk):
                fs_ref[slot, c, pl.ds(k, n2, stride=SUBLANES), :] = f[:, c * LANES:(c + 1) * LANES]
        for c in range(nblk):
            lanes = slice(ch * cw + c * LANES, ch * cw + (c + 1) * LANES)
            o_ref[0, :, :, lanes] = fs_ref[slot, c].reshape(n2, SUBLANES, LANES).astype(o_ref.dtype)


def _fourier_stage2(y, tab):
    b, n2, n1, d = y.shape
    cw = 2 * MXU_DIM
    rows = n2 * SUBLANES
    return pl.pallas_call(
        functools.partial(_fourier_stage2_kernel, cw=cw),
        grid=(b, n1 // SUBLANES),
        in_specs=[
            pl.BlockSpec((1, n2, SUBLANES, d), lambda bi, k: (bi, 0, k, 0)),
            pl.BlockSpec((n2, 2 * n2), lambda bi, k: (0, 0)),
        ],
        out_specs=pl.BlockSpec((1, n2, SUBLANES, d), lambda bi, k: (bi, 0, k, 0)),
        out_shape=jax.ShapeDtypeStruct((b, n2, n1, d), BF16),
        scratch_shapes=[pltpu.VMEM((2, cw // LANES, rows, LANES), jnp.uint32),
                        pltpu.VMEM((2, cw // LANES, rows, LANES), F32)],
        compiler_params=_cparams(("parallel", "parallel")),
        name="fourier_stage2",
    )(y, tab)


def _proj_norm_kernel(a_ref, x_ref, w_ref, g_ref, x1_ref, h_ref):
    x1 = x_ref[...] + jnp.dot(a_ref[...].astype(BF16), w_ref[...], preferred_element_type=F32)
    x1_ref[...] = x1
    h_ref[...] = _rms_rows(x1, g_ref[...]).astype(BF16)


def _proj_norm(a, x2d, w, gain, tm):
    t, d = x2d.shape
    return pl.pallas_call(
        _proj_norm_kernel,
        grid=(t // tm,),
        in_specs=[
            pl.BlockSpec((tm, d), lambda i: (i, 0)),
            pl.BlockSpec((tm, d), lambda i: (i, 0)),
            pl.BlockSpec((None, d, d), lambda i: (0, 0, 0)),
            pl.BlockSpec((1, d), lambda i: (0, 0)),
        ],
        out_specs=[
            pl.BlockSpec((tm, d), lambda i: (i, 0)),
            pl.BlockSpec((tm, d), lambda i: (i, 0)),
        ],
        out_shape=[jax.ShapeDtypeStruct((t, d), F32), jax.ShapeDtypeStruct((t, d), BF16)],
        compiler_params=_cparams(("parallel",)),
        name="proj_norm",
    )(a, x2d, w, gain)


def _mlp_kernel(h_ref, x_ref, wu_hbm, wd_hbm, o_ref, wu_buf, wd_buf, sem, *, layer, tf, ts):
    i = pl.program_id(0)
    nchunk = wu_hbm.shape[-1] // tf
    assert nchunk % 2 == 0

    def chunk_copies(j, slot):
        return (
            pltpu.make_async_copy(wu_hbm.at[layer, :, pl.ds(j * tf, tf)], wu_buf.at[slot], sem.at[0, slot]),
            pltpu.make_async_copy(wd_hbm.at[layer, pl.ds(j * tf, tf), :], wd_buf.at[slot], sem.at[1, slot]),
        )

    def start(j, slot):
        for c in chunk_copies(j, slot):
            c.start()

    @pl.when(i == 0)
    def _():
        start(0, 0)

    o_ref[...] = x_ref[...]
    per_iter = min(MLP_CHUNKS_PER_ITER, nchunk)
    niter = nchunk // per_iter
    assert per_iter % 2 == 0 and niter * per_iter == nchunk

    def body(jj, carry):
        for u in range(per_iter):
            j = per_iter * jj + u
            slot = u % 2
            for c in chunk_copies(j, slot):
                c.wait()
            if u + 1 < per_iter:
                start(j + 1, 1 - slot)
            else:
                @pl.when(jj + 1 < niter)
                def _():
                    start(j + 1, 0)

                @pl.when(jnp.logical_and(jj + 1 == niter, i + 1 < pl.num_programs(0)))
                def _():
                    start(0, 0)
            for s in range(tf // ts):
                a = jnp.maximum(
                    jnp.dot(h_ref[...], wu_buf[slot, :, s * ts:(s + 1) * ts], preferred_element_type=F32),
                    0.0)
                o_ref[...] += jnp.dot((a * a).astype(BF16), wd_buf[slot, s * ts:(s + 1) * ts, :],
                                      preferred_element_type=F32)
        return carry

    lax.fori_loop(0, niter, body, 0)


def _mlp(h, x1, w_up, w_down, layer, tm, tf):
    t, d = x1.shape
    ts = min(tf, 2 * MXU_DIM)
    return pl.pallas_call(
        functools.partial(_mlp_kernel, layer=layer, tf=tf, ts=ts),
        grid=(t // tm,),
        in_specs=[
            pl.BlockSpec((tm, d), lambda i: (i, 0)),
            pl.BlockSpec((tm, d), lambda i: (i, 0)),
            pl.BlockSpec(memory_space=pl.ANY),
            pl.BlockSpec(memory_space=pl.ANY),
        ],
        out_specs=pl.BlockSpec((tm, d), lambda i: (i, 0)),
        out_shape=jax.ShapeDtypeStruct((t, d), F32),
        scratch_shapes=[
            pltpu.VMEM((2, d, tf), BF16),
            pltpu.VMEM((2, tf, d), BF16),
            pltpu.SemaphoreType.DMA((2, 2)),
        ],
        compiler_params=_cparams(("arbitrary",)),
        name="sqrelu_mlp",
    )(h, x1, w_up, w_down)


def _qkv_kernel(x_ref, g_ref, w_ref, e_ref, qg_ref, kg_ref, cf_ref, s1_ref, s2_ref,
                q_ref, k_ref, v_ref, *, d, scale):
    h = _rms_rows(x_ref[...], g_ref[...]).astype(BF16)
    e = e_ref[...]
    cf, s1, s2 = cf_ref[...], s1_ref[...], s2_ref[...]
    kvd = N_KV * HEAD_DIM
    lo = lax.broadcasted_iota(jnp.int32, (h.shape[0], LANES), 1) < HEAD_DIM

    def proj(c0, width):
        return jnp.dot(h, w_ref[:, c0:c0 + width], preferred_element_type=F32)

    def head_norm_rope(blk, gain, out_scale):
        ss = jnp.dot((blk * blk).astype(BF16), e, preferred_element_type=F32)
        n = blk * lax.rsqrt(ss * (1.0 / HEAD_DIM) + EPS) * gain
        outs = []
        for u in range(MXU_DIM // LANES):
            xb = n[:, u * LANES:(u + 1) * LANES]
            r = (xb * cf + pltpu.roll(xb, LANES - ROT_HALF, axis=1) * s1
                 + pltpu.roll(xb, ROT_HALF, axis=1) * s2)
            outs.append(r * out_scale if out_scale != 1.0 else r)
        return outs

    def store_duplicated(ref, halves):
        for u, xb in enumerate(halves):
            xr = pltpu.roll(xb, HEAD_DIM, axis=1)
            ref[:, (2 * u) * LANES:(2 * u + 1) * LANES] = jnp.where(lo, xb, xr).astype(BF16)
            ref[:, (2 * u + 1) * LANES:(2 * u + 2) * LANES] = jnp.where(lo, xr, xb).astype(BF16)

    pw = 2 * MXU_DIM
    for c in range(d // pw):
        blk = proj(c * pw, pw)
        outs = (head_norm_rope(blk[:, :MXU_DIM], qg_ref[...], scale)
                + head_norm_rope(blk[:, MXU_DIM:], qg_ref[...], scale))
        q_ref[:, c * pw:(c + 1) * pw] = jnp.concatenate(outs, axis=1).astype(BF16)
    assert kvd == MXU_DIM
    kv = proj(d, 2 * kvd)
    store_duplicated(k_ref, head_norm_rope(kv[:, :kvd], kg_ref[...], 1.0))
    store_duplicated(v_ref, [kv[:, kvd + u * LANES:kvd + (u + 1) * LANES] for u in range(kvd // LANES)])


def _qkv(x2d, gain, w, e, qg, kg, cf, s1, s2, seq, tm):
    t, d = x2d.shape
    kd = N_KV * LANES
    nt = seq // tm
    tab_spec = pl.BlockSpec((tm, LANES), lambda i: (i % nt, 0))
    const = lambda shape: pl.BlockSpec(shape, lambda i: (0, 0))
    return pl.pallas_call(
        functools.partial(_qkv_kernel, d=d, scale=LOG2E / math.sqrt(HEAD_DIM)),
        grid=(t // tm,),
        in_specs=[
            pl.BlockSpec((tm, d), lambda i: (i, 0)),
            const((1, d)),
            pl.BlockSpec((None,) + w.shape[1:], lambda i: (0, 0, 0)),
            const((MXU_DIM, MXU_DIM)),
            const((1, MXU_DIM)),
            const((1, MXU_DIM)),
            tab_spec, tab_spec, tab_spec,
        ],
        out_specs=[
            pl.BlockSpec((tm, d), lambda i: (i, 0)),
            pl.BlockSpec((tm, kd), lambda i: (i, 0)),
            pl.BlockSpec((tm, kd), lambda i: (i, 0)),
        ],
        out_shape=[
            jax.ShapeDtypeStruct((t, d), BF16),
            jax.ShapeDtypeStruct((t, kd), BF16),
            jax.ShapeDtypeStruct((t, kd), BF16),
        ],
        compiler_params=_cparams(("parallel",)),
        name="qkv_norm_rope",
    )(x2d, gain, w, e, qg, kg, cf, s1, s2)


def _attn_kernel(sink_ref, q_ref, kp_ref, kc_ref, kn_ref, vp_ref, vc_ref, vn_ref, x_ref, w_ref, g_ref,
                 x1_ref, h_ref, o_scr, *, nb, nsub):
    j = pl.program_id(1)
    bq = WINDOW_BLOCK
    npair = GROUP // 2
    nq = GROUP * bq
    lo = lax.broadcasted_iota(jnp.int32, (bq, LANES), 1) < HEAD_DIM
    key = lax.broadcasted_iota(jnp.int32, (bq, nq), 0)
    qry = lax.broadcasted_iota(jnp.int32, (bq, nq), 1) % bq
    zero = jnp.zeros((bq, LANES), BF16)
    xr = BF16_ROWS
    lo_x = lax.broadcasted_iota(jnp.int32, (3 * bq + xr, LANES), 1) < HEAD_DIM
    sink_row = lax.broadcasted_iota(jnp.int32, (xr, nq), 0) == 0
    vzero = jnp.zeros((xr, LANES), BF16)
    nt = (((1,), (1,)), ((), ()))
    tn = (((0,), (0,)), ((), ()))

    def kv_blocks(prev_ref, cur_ref, next_ref, u, lanes):
        rows = lambda t: slice(t * bq, (t + 1) * bq)
        before = prev_ref[0, :, lanes] if u == 0 else cur_ref[0, rows(u - 1), lanes]
        after = next_ref[0, :, lanes] if u == nsub - 1 else cur_ref[0, rows(u + 1), lanes]
        return [before, cur_ref[0, rows(u), lanes], after]

    for u in range(nsub):
        blk_idx = j * nsub + u
        vprev = (key >= qry) & (blk_idx > 0)
        vnext = (key <= qry) & (blk_idx < nb - 1)
        qrows = slice(u * bq, (u + 1) * bq)
        for h in range(N_KV):
            ks = slice(h * LANES, (h + 1) * LANES)
            k2 = jnp.concatenate(kv_blocks(kp_ref, kc_ref, kn_ref, u, ks), axis=0)
            v2x = jnp.concatenate(kv_blocks(vp_ref, vc_ref, vn_ref, u, ks) + [vzero], axis=0)
            qparts, sinks = [], []
            for parity in range(2):
                for p in range(npair):
                    c0 = (h * npair + p) * LANES
                    blk = q_ref[0, qrows, c0:c0 + LANES]
                    qparts.append(jnp.where(lo if parity == 0 else ~lo, blk, zero))
                    sinks.append(jnp.full((1, bq), sink_ref[h * GROUP + 2 * p + parity] * LOG2E, F32))
            qs = jnp.concatenate(qparts, axis=0)
            sink = jnp.concatenate(sinks, axis=1)
            st = lax.dot_general(k2, qs, nt, preferred_element_type=F32)
            sp = jnp.where(vprev, st[:bq], NEG)
            sc = st[bq:2 * bq]
            sn = jnp.where(vnext, st[2 * bq:], NEG)
            m = jnp.max(jnp.maximum(jnp.maximum(sp, sc), sn), axis=0, keepdims=True)
            m = jnp.maximum(m, sink)
            es = jnp.where(sink_row, jnp.exp2(sink - m), 0.0)
            pt = jnp.concatenate(
                [jnp.exp2(sp - m), jnp.exp2(sc - m), jnp.exp2(sn - m), es], axis=0).astype(BF16)
            ot = lax.dot_general(jnp.where(lo_x, v2x, 1.0), pt, tn, preferred_element_type=F32)
            res = ot[:HEAD_DIM] * (1.0 / ot[HEAD_DIM:HEAD_DIM + 1])
            half = nq // 2
            for p in range(npair):
                pair = jnp.concatenate(
                    [res[:, p * bq:(p + 1) * bq], res[:, half + p * bq:half + (p + 1) * bq]], axis=0)
                c0 = (h * npair + p) * LANES
                o_scr[qrows, c0:c0 + LANES] = pair.T.astype(BF16)

    x1 = x_ref[0] + jnp.dot(o_scr[...], w_ref[...], preferred_element_type=F32)
    x1_ref[0] = x1
    h_ref[0] = _rms_rows(x1, g_ref[...]).astype(BF16)


def _attention(sink, q, k, v, x, w_o, gain):
    b, s, d = q.shape
    kd = k.shape[-1]
    bq = WINDOW_BLOCK
    nb = s // bq
    nsub = ATTN_SUB if nb % ATTN_SUB == 0 else 1
    prev = lambda bi, j: (bi, jnp.maximum(j * nsub - 1, 0), 0)
    cur = lambda bi, j: (bi, j, 0)
    nxt = lambda bi, j: (bi, jnp.minimum((j + 1) * nsub, nb - 1), 0)
    edge = lambda im: pl.BlockSpec((1, bq, kd), im)
    own = pl.BlockSpec((1, nsub * bq, kd), cur)
    tile = pl.BlockSpec((1, nsub * bq, d), cur)
    return pl.pallas_call(
        functools.partial(_attn_kernel, nb=nb, nsub=nsub),
        grid=(b, nb // nsub),
        in_specs=[
            pl.BlockSpec(memory_space=pltpu.SMEM),
            tile,
            edge(prev), own, edge(nxt), edge(prev), own, edge(nxt),
            tile,
            pl.BlockSpec((None, d, d), lambda bi, j: (0, 0, 0)),
            pl.BlockSpec((1, d), lambda bi, j: (0, 0)),
        ],
        out_specs=[tile, tile],
        out_shape=[jax.ShapeDtypeStruct((b, s, d), F32), jax.ShapeDtypeStruct((b, s, d), BF16)],
        scratch_shapes=[pltpu.VMEM((nsub * bq, d), BF16)],
        compiler_params=_cparams(("parallel", "parallel")),
        name="window_attention",
    )(sink, q, k, k, k, v, v, v, x, w_o, gain)


def _chan_table(gd):
    k = np.arange(gd)
    ang = 2.0 * np.pi * ((k[:, None] * k[None, :]) % gd) / gd
    sc = 1.0 / math.sqrt(gd)
    return jnp.asarray(np.concatenate([np.cos(ang) * sc, -np.sin(ang) * sc], axis=1), BF16)


def _stage1_table(n1, n2):
    n = n1 * n2
    k1 = np.arange(n1)
    tw = 2.0 * np.pi * ((np.arange(n2)[:, None] * k1[None, :]) % n) / n
    dft = 2.0 * np.pi * ((k1[:, None] * k1[None, :]) % n1) / n1
    rep = lambda a: np.repeat(a, 2, axis=-2)
    ct, st = jnp.asarray(rep(np.cos(tw)[:, :, None]), F32), jnp.asarray(rep(np.sin(tw)[:, :, None]), F32)
    cd, sd = jnp.asarray(rep(np.cos(dft))[None], F32), jnp.asarray(rep(np.sin(dft))[None], F32)
    c, s = ct * cd - st * sd, st * cd + ct * sd
    real_row = jnp.asarray((np.arange(2 * n1) % 2 == 1)[None, :, None])
    left = jnp.where(real_row, c, -s)
    right = jnp.where(real_row, s, c)
    return jnp.concatenate([left, right], axis=-1).astype(BF16)


def _stage2_table(n1, n2):
    k = np.arange(n2)
    ang = 2.0 * np.pi * ((k[:, None] * k[None, :]) % n2) / n2
    sc = 1.0 / math.sqrt(n1 * n2)
    return jnp.asarray(np.stack([np.sin(ang) * sc, np.cos(ang) * sc], axis=-1).reshape(n2, 2 * n2), BF16)


def _rope_lane_tables(s):
    inv_freq = ROPE_THETA ** (-jnp.arange(0, ROT_DIM, 2, dtype=F32) / ROT_DIM)
    ang = jnp.arange(s, dtype=F32)[:, None] * inv_freq[None, :]
    cos, sin = jnp.cos(ang), jnp.sin(ang)
    pad = HEAD_DIM - ROT_DIM
    ones = jnp.ones((s, pad), F32)
    zeros = jnp.zeros((s, pad), F32)
    zh = jnp.zeros((s, ROT_HALF), F32)
    cf = jnp.concatenate([cos, cos, ones], axis=1)
    s1 = jnp.concatenate([-sin, zh, zeros], axis=1)
    s2 = jnp.concatenate([zh, sin, zeros], axis=1)
    rep = LANES // HEAD_DIM
    return tuple(jnp.tile(a, (1, rep)) for a in (cf, s1, s2))


def _segment_ones():
    i = np.arange(MXU_DIM) // HEAD_DIM
    return jnp.asarray((i[:, None] == i[None, :]).astype(np.float32), BF16)


def _fourier_mixer(x, gain, w_out, mlp_gain, tm):
    b, s, d = x.shape
    gd = d // FFT_GROUPS
    n1, n2 = _seq_split(s)
    y = _fourier_stage1(x.reshape(b, n1, n2, d), gain, _chan_table(gd), _stage1_table(n1, n2))
    f = _fourier_stage2(y, _stage2_table(n1, n2))
    return _proj_norm(f.reshape(b * s, d), x.reshape(b * s, d), w_out, mlp_gain, tm)


def _attn_mixer(x2d, b, s, gain, w_qkv, q_gain, k_gain, sink, w_o, mlp_gain, tm):
    t, d = x2d.shape
    rep = MXU_DIM // HEAD_DIM
    cf, s1, s2 = _rope_lane_tables(s)
    q, k, v = _qkv(x2d, gain, w_qkv, _segment_ones(),
                   jnp.tile(q_gain, rep)[None, :], jnp.tile(k_gain, rep)[None, :],
                   cf, s1, s2, s, tm)
    kd = k.shape[-1]
    x1, h = _attention(sink, q.reshape(b, s, d), k.reshape(b, s, kd), v.reshape(b, s, kd),
                       x2d.reshape(b, s, d), w_o, mlp_gain)
    return x1.reshape(t, d), h.reshape(t, d)


def _tile_sizes(batch, seq, dff):
    tm = min(2 * MXU_DIM, seq)
    tm_mlp = min(4 * MXU_DIM, batch * seq)
    tf = min(4 * MXU_DIM, dff // 2)
    return tm, tm_mlp, tf


def _trunk(x, p, tm, tm_mlp, tf):
    b, s, d = x.shape
    x1, h = _fourier_mixer(x, p["fourier_norm"][0][None, :], p["fourier_w_out"],
                           p["mlp_norm"][0][None, :], tm)
    x2 = _mlp(h, x1, p["mlp_w_up"], p["mlp_w_down"], 0, tm_mlp, tf)
    x3, h = _attn_mixer(x2, b, s, p["attn_norm"][0][None, :], p["attn_w_qkv"], p["attn_q_norm"][0],
                        p["attn_k_norm"][0], p["attn_sink"][0], p["attn_w_o"],
                        p["mlp_norm"][1][None, :], tm)
    x4 = _mlp(h, x3, p["mlp_w_up"], p["mlp_w_down"], 1, tm_mlp, tf)
    return x4.reshape(b, s, d)


def kernel(x_prompt, x_sample, fourier_norm, fourier_w_out, attn_norm, attn_w_qkv, attn_q_norm,
           attn_k_norm, attn_sink, attn_w_o, mlp_norm, mlp_w_up, mlp_w_down):
    p = dict(
        fourier_norm=fourier_norm, fourier_w_out=fourier_w_out.astype(BF16),
        attn_norm=attn_norm, attn_w_qkv=attn_w_qkv.astype(BF16), attn_q_norm=attn_q_norm,
        attn_k_norm=attn_k_norm, attn_sink=attn_sink, attn_w_o=attn_w_o.astype(BF16),
        mlp_norm=mlp_norm, mlp_w_up=mlp_w_up.astype(BF16), mlp_w_down=mlp_w_down.astype(BF16),
    )
    dff = mlp_w_up.shape[-1]
    return tuple(_trunk(x, p, *_tile_sizes(x.shape[0], x.shape[1], dff)) for x in (x_prompt, x_sample))
```
